```python
import numpy as np
import jax
import jax.numpy as jnp
from jax import lax

D_MODEL = 1024
BATCH = 2
SEQ = 8192
DEPTH = 4
DEC_BATCH = 128
DEC_SEQ = 1
PAST_LEN = 2048
PAGE_SIZE = 128

HEAD_DIM = 64
MIX_WIDTH = D_MODEL
RWKV_WIDTH = MIX_WIDTH // 2
NSA_WIDTH = MIX_WIDTH - RWKV_WIDTH
RWKV_HEADS = RWKV_WIDTH // HEAD_DIM
NSA_HEADS = NSA_WIDTH // HEAD_DIM
NSA_KV_HEADS = 2
HEADS_PER_KV = NSA_HEADS // NSA_KV_HEADS
KV_WIDTH = NSA_KV_HEADS * HEAD_DIM
W_LORA = 64
A_LORA = 64
G_LORA = 128
SHIFT_COLS = 3 * RWKV_WIDTH + W_LORA + A_LORA + G_LORA
NSA_COLS = NSA_WIDTH + 6 * KV_WIDTH + 3 * NSA_HEADS
N_IN = SHIFT_COLS + NSA_COLS
CMP_BLOCK = 32
CMP_STRIDE = 16
CMP_HIDDEN = 128
SEL_BLOCK = 64
SEL_TOP_N = 16
WINDOW = 512
Q_BLOCK = 128
D_FF = -(-8 * D_MODEL // (3 * 256)) * 256
ROPE_THETA = 10000.0
NORM_EPS = 1e-6
GN_EPS = 64e-5
NEG_INF = -1e30
FORCE = 1e9

kernel_name = 'hymba_rwkv7_nsa_decoder_step'


def rms_norm(x, g):
    xf = x.astype(jnp.float32)
    y = xf * lax.rsqrt(jnp.mean(xf * xf, axis=-1, keepdims=True) + NORM_EPS)
    return (y * g.astype(jnp.float32)).astype(x.dtype)


def rope(x, pos):
    half = HEAD_DIM // 2
    inv_freq = ROPE_THETA ** (-jnp.arange(half, dtype=jnp.float32) / half)
    ang = pos.astype(jnp.float32)[:, None] * inv_freq[None, :]
    cos = jnp.cos(ang)[None, :, None, :]
    sin = jnp.sin(ang)[None, :, None, :]
    xf = x.astype(jnp.float32)
    x1, x2 = xf[..., :half], xf[..., half:]
    return jnp.concatenate([x1 * cos - x2 * sin, x2 * cos + x1 * sin], axis=-1).astype(x.dtype)


def ada_modulation(c, w, b):
    m = jax.nn.silu(c) @ w + b
    return jnp.split(m[:, None, :], 6, axis=-1)


def rwkv_mix(p_r, shift_prev, S0, lp):
    B, T = p_r.shape[:2]
    f32 = jnp.float32
    prev = jnp.concatenate([shift_prev[:, None].astype(p_r.dtype), p_r[:, :-1]], axis=1)
    xs = p_r + lp['rwkv_mu'] * (prev - p_r)
    splits = [RWKV_WIDTH, 2 * RWKV_WIDTH, 3 * RWKV_WIDTH, 3 * RWKV_WIDTH + W_LORA, 3 * RWKV_WIDTH + W_LORA + A_LORA]
    r, k, v, wd, ad, gd = jnp.split(xs, splits, axis=-1)
    w = -jax.nn.softplus(-(lp['rwkv_w0'] + jnp.tanh(wd) @ lp['rwkv_w_up'])) - 0.5
    a = jax.nn.sigmoid(lp['rwkv_a0'] + ad @ lp['rwkv_a_up'])
    g = jax.nn.sigmoid(gd) @ lp['rwkv_g_up']
    heads = lambda t: t.reshape(B, T, RWKV_HEADS, HEAD_DIM).astype(f32)
    kk = heads(k * lp['rwkv_k_k'])
    kk = kk / jnp.maximum(jnp.sqrt(jnp.sum(kk * kk, axis=-1, keepdims=True)), 1e-12)
    k = heads(k * (1.0 + (a - 1.0) * lp['rwkv_k_a']))
    r, v, a = heads(r), heads(v), heads(a)
    decay = jnp.exp(-jnp.exp(heads(w)))
    b = kk * a

    def step(S, inp):
        r_t, w_t, k_t, v_t, kk_t, b_t = inp
        sa = jnp.einsum('bhvk,bhk->bhv', S, -kk_t)
        S = S * w_t[:, :, None, :] + sa[..., None] * b_t[:, :, None, :] + v_t[..., None] * k_t[:, :, None, :]
        return S, jnp.einsum('bhvk,bhk->bhv', S, r_t)

    tm = lambda t: t.transpose(1, 0, 2, 3)
    S_new, y = lax.scan(step, S0.astype(f32), (tm(r), tm(decay), tm(k), tm(v), tm(kk), tm(b)))
    y = y.transpose(1, 0, 2, 3)
    mu = jnp.mean(y, axis=-1, keepdims=True)
    var = jnp.mean(jnp.square(y - mu), axis=-1, keepdims=True)
    y = ((y - mu) * lax.rsqrt(var + GN_EPS)).reshape(B, T, RWKV_WIDTH) * lp['rwkv_ln_w'] + lp['rwkv_ln_b']
    bonus = jnp.sum(r * k * lp['rwkv_r_k'].astype(f32), axis=-1, keepdims=True) * v
    y = (y + bonus.reshape(B, T, RWKV_WIDTH)) * g
    return y.astype(p_r.dtype), S_new, p_r[:, -1]


def nsa_project(p_nsa, pos, qk_norm):
    B, T = p_nsa.shape[:2]
    offs = [NSA_WIDTH + i * KV_WIDTH for i in range(7)]
    q, kc, vc, ks, vs, kw, vw, gl = jnp.split(p_nsa, offs, axis=-1)
    kvh = lambda t: t.reshape(B, T, NSA_KV_HEADS, HEAD_DIM)
    q = rms_norm(q.reshape(B, T, NSA_HEADS, HEAD_DIM), qk_norm[0])
    q_r = rope(q, pos)
    ks = rope(rms_norm(kvh(ks), qk_norm[2]), pos)
    kw = rope(rms_norm(kvh(kw), qk_norm[3]), pos)
    cmp_rows = jnp.stack([kvh(kc), kvh(vc)], axis=2)
    sel_rows = jnp.stack([ks, kvh(vs)], axis=2)
    win_rows = jnp.stack([kw, kvh(vw)], axis=2)
    gates = jax.nn.sigmoid(gl.reshape(B, T, NSA_HEADS, 3))
    return q, q_r, gates, cmp_rows, sel_rows, win_rows


def compress_branch(cmp_rows, lp):
    B, T = cmp_rows.shape[:2]
    nc = (T - CMP_BLOCK) // CMP_STRIDE + 1
    idx = np.arange(nc)[:, None] * CMP_STRIDE + np.arange(CMP_BLOCK)[None, :]
    pos_emb = lp['cmp_pos'].transpose(1, 0, 2)[None, None, :, :, None, :]
    blk = cmp_rows[:, idx] + pos_emb
    blk = blk.transpose(0, 1, 3, 4, 2, 5).reshape(B, nc, 2, NSA_KV_HEADS, CMP_BLOCK * HEAD_DIM)
    h = jax.nn.gelu(jnp.einsum('bnkgi,kij->bnkgj', blk, lp['cmp_w1']) + lp['cmp_b1'][None, None, :, None, :])
    out = jnp.einsum('bnkgj,kjd->bnkgd', h, lp['cmp_w2']) + lp['cmp_b2'][None, None, :, None, :]
    kc = rms_norm(out[:, :, 0], lp['qk_norm'][1])
    vc = out[:, :, 1]
    return kc, vc, jnp.asarray(idx[:, -1], dtype=jnp.int32)


def to_sel_blocks(rows):
    B, T = rows.shape[:2]
    ns = -(-T // SEL_BLOCK)
    rows = jnp.pad(rows, ((0, 0), (0, ns * SEL_BLOCK - T), (0, 0), (0, 0)))
    return rows.reshape(B, ns, SEL_BLOCK, NSA_KV_HEADS, HEAD_DIM).transpose(0, 3, 1, 2, 4)


def cmp_sel_overlap(nc, ns):
    s = np.arange(nc)[:, None] * CMP_STRIDE
    j = np.arange(ns)[None, :] * SEL_BLOCK
    return jnp.asarray(((s < j + SEL_BLOCK) & (s + CMP_BLOCK > j)).astype(np.float32))


def nsa_attend(q, q_r, gates, qpos, kc, vc, cmp_end, ks_blk, vs_blk, kw, vw, kwpos):
    B, Tq = q.shape[:2]
    f32 = jnp.float32
    scale = HEAD_DIM ** -0.5
    grp = lambda t: t.reshape(B, Tq, NSA_KV_HEADS, HEADS_PER_KV, HEAD_DIM)
    qc, qr = grp(q), grp(q_r)
    s_c = jnp.einsum('bqghd,bngd->bghqn', qc, kc, preferred_element_type=f32) * scale
    m_c = cmp_end[None, :] <= qpos[:, None]
    p_c = jnp.where(m_c, jax.nn.softmax(jnp.where(m_c, s_c, NEG_INF), axis=-1), 0.0)
    o_c = jnp.einsum('bghqn,bngd->bqghd', p_c.astype(vc.dtype), vc)
    nc, ns = kc.shape[1], ks_blk.shape[2]
    imp = jnp.einsum('bghqn,ns->bgqs', p_c, cmp_sel_overlap(nc, ns))
    blk = jnp.arange(ns, dtype=jnp.int32)[None, :]
    cur = (qpos // SEL_BLOCK)[:, None]
    valid = blk * SEL_BLOCK <= qpos[:, None]
    forced = (blk == 0) | (blk == cur) | (blk == cur - 1)
    score = jnp.where(valid, jnp.where(forced, FORCE, imp), NEG_INF)
    top_vals, top_idx = lax.top_k(score, min(SEL_TOP_N, ns))
    bi = jnp.arange(B)[:, None, None, None]
    gi = jnp.arange(NSA_KV_HEADS)[None, :, None, None]
    k_sel = ks_blk[bi, gi, top_idx]
    v_sel = vs_blk[bi, gi, top_idx]
    kpos = top_idx[..., None] * SEL_BLOCK + jnp.arange(SEL_BLOCK, dtype=jnp.int32)
    m_s = (top_vals > 0.5 * NEG_INF)[..., None] & (kpos <= qpos[None, None, :, None, None])
    s_s = jnp.einsum('bqghd,bgqnkd->bghqnk', qr, k_sel, preferred_element_type=f32) * scale
    s_s = jnp.where(m_s[:, :, None], s_s, NEG_INF)
    p_s = jax.nn.softmax(s_s.reshape(s_s.shape[:4] + (-1,)), axis=-1).reshape(s_s.shape)
    o_s = jnp.einsum('bghqnk,bgqnkd->bqghd', p_s.astype(v_sel.dtype), v_sel)
    m_w = (kwpos[None, :] <= qpos[:, None]) & (kwpos[None, :] > qpos[:, None] - WINDOW) & (kwpos[None, :] >= 0)
    s_w = jnp.einsum('bqghd,bkgd->bghqk', qr, kw, preferred_element_type=f32) * scale
    p_w = jax.nn.softmax(jnp.where(m_w, s_w, NEG_INF), axis=-1)
    o_w = jnp.einsum('bghqk,bkgd->bqghd', p_w.astype(vw.dtype), vw)
    g = gates.reshape(B, Tq, NSA_KV_HEADS, HEADS_PER_KV, 3)
    o = g[..., 0:1] * o_c + g[..., 1:2] * o_s + g[..., 2:3] * o_w
    return o.reshape(B, Tq, NSA_WIDTH)


def nsa_prompt(p_nsa, lp):
    B, T = p_nsa.shape[:2]
    pos = jnp.arange(T, dtype=jnp.int32)
    q, q_r, gates, cmp_rows, sel_rows, win_rows = nsa_project(p_nsa, pos, lp['qk_norm'])
    kc, vc, cmp_end = compress_branch(cmp_rows, lp)
    ks_blk, vs_blk = to_sel_blocks(sel_rows[:, :, 0]), to_sel_blocks(sel_rows[:, :, 1])
    win_pad = jnp.pad(win_rows, ((0, 0), (WINDOW, 0), (0, 0), (0, 0), (0, 0)))

    def one_block(i):
        start = i * Q_BLOCK
        take = lambda a: lax.dynamic_slice_in_dim(a, start, Q_BLOCK, axis=1)
        kv_w = lax.dynamic_slice_in_dim(win_pad, start, WINDOW + Q_BLOCK, axis=1)
        qpos = start + jnp.arange(Q_BLOCK, dtype=jnp.int32)
        kwpos = start - WINDOW + jnp.arange(WINDOW + Q_BLOCK, dtype=jnp.int32)
        return nsa_attend(take(q), take(q_r), take(gates), qpos, kc, vc, cmp_end, ks_blk, vs_blk,
                          kv_w[:, :, 0], kv_w[:, :, 1], kwpos)

    y = lax.map(one_block, jnp.arange(T // Q_BLOCK, dtype=jnp.int32))
    y = y.transpose(1, 0, 2, 3).reshape(B, T, NSA_WIDTH)
    return y, (cmp_rows, sel_rows, win_rows[:, -min(WINDOW, T):])


def nsa_sample(p_nsa, past_cmp, past_sel, win_buf, past_len, lp):
    B, T = p_nsa.shape[:2]
    pos = past_len + jnp.arange(T, dtype=jnp.int32)
    q, q_r, gates, cmp_rows, sel_rows, win_rows = nsa_project(p_nsa, pos, lp['qk_norm'])
    all_cmp = jnp.concatenate([past_cmp.astype(cmp_rows.dtype), cmp_rows], axis=1)
    kc, vc, cmp_end = compress_branch(all_cmp, lp)
    all_sel = jnp.concatenate([past_sel.astype(sel_rows.dtype), sel_rows], axis=1)
    ks_blk, vs_blk = to_sel_blocks(all_sel[:, :, 0]), to_sel_blocks(all_sel[:, :, 1])
    wb = win_buf.shape[1]
    all_win = jnp.concatenate([win_buf.astype(win_rows.dtype), win_rows], axis=1)
    kwpos = past_len - wb + jnp.arange(wb + T, dtype=jnp.int32)
    y = nsa_attend(q, q_r, gates, pos, kc, vc, cmp_end, ks_blk, vs_blk, all_win[:, :, 0], all_win[:, :, 1], kwpos)
    return y, (cmp_rows, sel_rows, all_win[:, -wb:])


def trunk_layer(x, c, shift_prev, S0, nsa_fn, lp):
    sh1, sc1, g1, sh2, sc2, g2 = ada_modulation(c, lp['ada_w'], lp['ada_b'])
    h = rms_norm(x, lp['norm_mix']) * (1.0 + sc1) + sh1
    p = h @ lp['w_in']
    y_r, S_new, shift_new = rwkv_mix(p[..., :SHIFT_COLS], shift_prev, S0, lp)
    y_n, nsa_new = nsa_fn(p[..., SHIFT_COLS:])
    x = x + g1 * (jnp.concatenate([y_r, y_n], axis=-1) @ lp['w_out'])
    h = rms_norm(x, lp['norm_ffn']) * (1.0 + sc2) + sh2
    gate, up = jnp.split(h @ lp['ffn_w_up'], 2, axis=-1)
    x = x + g2 * ((jax.nn.silu(gate) * up) @ lp['ffn_w_down'])
    return x, S_new, shift_new, nsa_new


def setup_inputs(seed: int = 0) -> dict:
    key = jax.random.key(seed)
    ks = iter(jax.random.split(key, 64))
    f32 = jnp.float32
    nrm = lambda shape, scale: jax.random.normal(next(ks), shape, f32) * scale
    uni = lambda shape, lo, hi: jax.random.uniform(next(ks), shape, f32, lo, hi)
    n_pages = PAST_LEN // PAGE_SIZE
    used = DEC_BATCH * n_pages
    n_pool = used + max(1, used // 4)
    win_buf = min(WINDOW, PAST_LEN)
    page_table = jax.random.permutation(next(ks), n_pool)[:used].reshape(DEC_BATCH, n_pages).astype(jnp.int32)
    D = D_MODEL
    return {
        'x_prompt': nrm((BATCH, SEQ, D), 1.0),
        'x_sample': nrm((DEC_BATCH, DEC_SEQ, D), 1.0),
        'cache_cmp_kv': nrm((DEPTH, n_pool, PAGE_SIZE, 2, NSA_KV_HEADS, HEAD_DIM), 1.0),
        'cache_sel_kv': nrm((DEPTH, n_pool, PAGE_SIZE, 2, NSA_KV_HEADS, HEAD_DIM), 1.0),
        'cache_win_kv': nrm((DEPTH, DEC_BATCH, win_buf, 2, NSA_KV_HEADS, HEAD_DIM), 1.0),
        'state_rwkv': nrm((DEPTH, DEC_BATCH, RWKV_HEADS, HEAD_DIM, HEAD_DIM), 1.0),
        'state_shift': nrm((DEPTH, DEC_BATCH, SHIFT_COLS), 1.0),
        'page_table': page_table,
        'c_prompt': nrm((BATCH, D), 1.0),
        'c_sample': nrm((DEC_BATCH, D), 1.0),
        'ada_w': nrm((DEPTH, D, 6 * D), 0.5 * D ** -0.5),
        'ada_b': nrm((DEPTH, 6 * D), 0.01),
        'norm_mix': 1.0 + nrm((DEPTH, D), 0.05),
        'norm_ffn': 1.0 + nrm((DEPTH, D), 0.05),
        'w_in': nrm((DEPTH, D, N_IN), D ** -0.5),
        'rwkv_mu': uni((DEPTH, SHIFT_COLS), 0.0, 1.0),
        'rwkv_w0': uni((DEPTH, RWKV_WIDTH), -6.5, -1.5),
        'rwkv_w_up': nrm((DEPTH, W_LORA, RWKV_WIDTH), 0.1 * W_LORA ** -0.5),
        'rwkv_a0': nrm((DEPTH, RWKV_WIDTH), 0.1),
        'rwkv_a_up': nrm((DEPTH, A_LORA, RWKV_WIDTH), 0.5 * A_LORA ** -0.5),
        'rwkv_g_up': nrm((DEPTH, G_LORA, RWKV_WIDTH), G_LORA ** -0.5),
        'rwkv_k_k': 0.85 + nrm((DEPTH, RWKV_WIDTH), 0.05),
        'rwkv_k_a': 1.0 + nrm((DEPTH, RWKV_WIDTH), 0.05),
        'rwkv_r_k': nrm((DEPTH, RWKV_HEADS, HEAD_DIM), 0.1),
        'rwkv_ln_w': 1.0 + nrm((DEPTH, RWKV_WIDTH), 0.05),
        'rwkv_ln_b': nrm((DEPTH, RWKV_WIDTH), 0.01),
        'qk_norm': 1.0 + nrm((DEPTH, 4, HEAD_DIM), 0.05),
        'cmp_pos': nrm((DEPTH, 2, CMP_BLOCK, HEAD_DIM), 0.1),
        'cmp_w1': nrm((DEPTH, 2, CMP_BLOCK * HEAD_DIM, CMP_HIDDEN), (CMP_BLOCK * HEAD_DIM) ** -0.5),
        'cmp_b1': nrm((DEPTH, 2, CMP_HIDDEN), 0.01),
        'cmp_w2': nrm((DEPTH, 2, CMP_HIDDEN, HEAD_DIM), CMP_HIDDEN ** -0.5),
        'cmp_b2': nrm((DEPTH, 2, HEAD_DIM), 0.01),
        'w_out': nrm((DEPTH, MIX_WIDTH, D), MIX_WIDTH ** -0.5),
        'ffn_w_up': nrm((DEPTH, D, 2 * D_FF), D ** -0.5),
        'ffn_w_down': nrm((DEPTH, D_FF, D), D_FF ** -0.5),
    }


def reference(x_prompt, x_sample, cache_cmp_kv, cache_sel_kv, cache_win_kv, state_rwkv, state_shift, page_table,
              c_prompt, c_sample, ada_w, ada_b, norm_mix, norm_ffn, w_in, rwkv_mu, rwkv_w0, rwkv_w_up, rwkv_a0,
              rwkv_a_up, rwkv_g_up, rwkv_k_k, rwkv_k_a, rwkv_r_k, rwkv_ln_w, rwkv_ln_b, qk_norm, cmp_pos, cmp_w1,
              cmp_b1, cmp_w2, cmp_b2, w_out, ffn_w_up, ffn_w_down):
    n_seq, n_pages = page_table.shape
    past_len = n_pages * cache_cmp_kv.shape[2]
    y_p, y_s = x_prompt, x_sample
    B = x_prompt.shape[0]
    cmp_p, cmp_s, sel_p, sel_s, win_p, win_s, rw_p, rw_s, sh_p, sh_s = ([] for _ in range(10))
    for l in range(DEPTH):
        lp = dict(ada_w=ada_w[l], ada_b=ada_b[l], norm_mix=norm_mix[l], norm_ffn=norm_ffn[l], w_in=w_in[l],
                  rwkv_mu=rwkv_mu[l], rwkv_w0=rwkv_w0[l], rwkv_w_up=rwkv_w_up[l], rwkv_a0=rwkv_a0[l],
                  rwkv_a_up=rwkv_a_up[l], rwkv_g_up=rwkv_g_up[l], rwkv_k_k=rwkv_k_k[l], rwkv_k_a=rwkv_k_a[l],
                  rwkv_r_k=rwkv_r_k[l], rwkv_ln_w=rwkv_ln_w[l], rwkv_ln_b=rwkv_ln_b[l], qk_norm=qk_norm[l],
                  cmp_pos=cmp_pos[l], cmp_w1=cmp_w1[l], cmp_b1=cmp_b1[l], cmp_w2=cmp_w2[l], cmp_b2=cmp_b2[l],
                  w_out=w_out[l], ffn_w_up=ffn_w_up[l], ffn_w_down=ffn_w_down[l])
        shift0 = jnp.zeros((B, SHIFT_COLS), x_prompt.dtype)
        S0 = jnp.zeros((B, RWKV_HEADS, HEAD_DIM, HEAD_DIM), jnp.float32)
        y_p, S_pn, sh_pn, (c_rows, s_rows, w_rows) = trunk_layer(
            y_p, c_prompt, shift0, S0, lambda p: nsa_prompt(p, lp), lp)
        cmp_p.append(c_rows); sel_p.append(s_rows); win_p.append(w_rows); rw_p.append(S_pn); sh_p.append(sh_pn)
        past_cmp = cache_cmp_kv[l][page_table].reshape(n_seq, past_len, 2, NSA_KV_HEADS, HEAD_DIM)
        past_sel = cache_sel_kv[l][page_table].reshape(n_seq, past_len, 2, NSA_KV_HEADS, HEAD_DIM)
        win_buf = cache_win_kv[l]
        y_s, S_sn, sh_sn, (c_rows, s_rows, w_rows) = trunk_layer(
            y_s, c_sample, state_shift[l], state_rwkv[l],
            lambda p: nsa_sample(p, past_cmp, past_sel, win_buf, past_len, lp), lp)
        cmp_s.append(c_rows); sel_s.append(s_rows); win_s.append(w_rows); rw_s.append(S_sn); sh_s.append(sh_sn)
    new_cmp_prompt = jnp.stack(cmp_p)
    new_cmp_sample = jnp.stack(cmp_s)
    new_sel_prompt = jnp.stack(sel_p)
    new_sel_sample = jnp.stack(sel_s)
    new_win_prompt = jnp.stack(win_p)
    new_win_sample = jnp.stack(win_s)
    new_rwkv_prompt = jnp.stack(rw_p)
    new_rwkv_sample = jnp.stack(rw_s)
    new_shift_prompt = jnp.stack(sh_p)
    new_shift_sample = jnp.stack(sh_s)
    return (y_p, y_s, new_cmp_prompt, new_cmp_sample, new_sel_prompt, new_sel_sample, new_win_prompt,
            new_win_sample, new_rwkv_prompt, new_rwkv_sample, new_shift_prompt, new_shift_sample)
```

```python
import functools

import numpy as np
import jax
import jax.numpy as jnp
from jax import lax
from jax.experimental import pallas as pl
from jax.experimental.pallas import tpu as pltpu

F32 = jnp.float32
BF16 = jnp.bfloat16

HEAD_DIM = 64
RWKV_WIDTH = 512
NSA_WIDTH = 512
RWKV_HEADS = 8
NSA_HEADS = 8
KV_WIDTH = 128
SHIFT_COLS = 1792
NSA_COLS = 1304
NSA_COLS_PAD = 1408
CMP_BLOCK = 32
CMP_STRIDE = 16
CMP_HIDDEN = 128
SEL_BLOCK = 64
SEL_TOP_N = 16
WINDOW = 512
ROPE_THETA = 10000.0
NORM_EPS = 1e-6
GN_EPS = 64e-5
NEG_INF = -1e30
FORCE = 1e9
LANES = 128
RWKV_CHUNK = 64
Q_TILE = 128
VMEM_LIMIT = 56 * 1024 * 1024


def _cparams(sem):
    return pltpu.CompilerParams(dimension_semantics=sem, vmem_limit_bytes=VMEM_LIMIT)


def _split2(x):
    hi = x.astype(BF16)
    lo = (x - hi.astype(F32)).astype(BF16)
    return hi, lo


def _split3(x):
    hi = x.astype(BF16)
    r = x - hi.astype(F32)
    mid = r.astype(BF16)
    lo = (r - mid.astype(F32)).astype(BF16)
    return hi, mid, lo


def _dg(a, b, ca=1, cb=0):
    return lax.dot_general(a, b, (((ca,), (cb,)), ((), ())), preferred_element_type=F32)


def _mm(a, b, ca=1, cb=0):
    ah, al = _split2(a)
    bh, bl = _split2(b)
    return _dg(ah, bh, ca, cb) + _dg(ah, bl, ca, cb) + _dg(al, bh, ca, cb)


def _mm_w(ah, al, wh, wl):
    return _dg(ah, wh) + _dg(ah, wl) + _dg(al, wh)


def _mm_xr(a, e, ca=1, cb=0):
    a0, a1, a2 = _split3(a)
    return _dg(a0, e, ca, cb) + _dg(a1, e, ca, cb) + _dg(a2, e, ca, cb)


def _mm_xl(e, b, ca=1, cb=0):
    b0, b1, b2 = _split3(b)
    return _dg(e, b0, ca, cb) + _dg(e, b1, ca, cb) + _dg(e, b2, ca, cb)


def _iota(shape, dim):
    return lax.broadcasted_iota(jnp.int32, shape, dim)


def _head_blockdiag():
    return (_iota((LANES, LANES), 0) // HEAD_DIM == _iota((LANES, LANES), 1) // HEAD_DIM).astype(BF16)


def _segsum(x, bd):
    w = x.shape[1]
    outs = [_mm_xr(x[:, j * LANES:(j + 1) * LANES], bd) for j in range(w // LANES)]
    return outs[0] if len(outs) == 1 else jnp.concatenate(outs, axis=1)


def _sigmoid(x):
    return 1.0 / (1.0 + jnp.exp(-x))


def _norm_mod(x, g, sc, sh):
    ms = jnp.mean(x * x, axis=-1, keepdims=True)
    y = x * lax.rsqrt(ms + NORM_EPS) * g
    return y * (1.0 + sc) + sh


def _topk_mask(score, n):
    rows, width = score.shape
    lanef = _iota((rows, width), 1).astype(F32)
    sel = jnp.zeros_like(score)
    sc = score
    for _ in range(n):
        m = jnp.max(sc, axis=-1, keepdims=True)
        idx = jnp.min(jnp.where(sc == m, lanef, float(width)), axis=-1, keepdims=True)
        hit = lanef == idx
        sel = jnp.where(hit & (m > 0.5 * NEG_INF), 1.0, sel)
        sc = jnp.where(hit, -3e38, sc)
    return sel


def _ada_kernel(c_ref, w_ref, b_ref, o_ref):
    c = c_ref[...]
    o_ref[0] = _mm(c * _sigmoid(c), w_ref[0]) + b_ref[0]


def _ada_mods(c_all, ada_w, ada_b):
    depth, d, n = ada_w.shape
    bc = c_all.shape[0]
    tn = 1024
    return pl.pallas_call(
        _ada_kernel,
        out_shape=jax.ShapeDtypeStruct((depth, bc, n), F32),
        grid=(depth, n // tn),
        in_specs=[pl.BlockSpec((bc, d), lambda l, j: (0, 0)),
                  pl.BlockSpec((1, d, tn), lambda l, j: (l, 0, j)),
                  pl.BlockSpec((1, 1, tn), lambda l, j: (l, 0, j))],
        out_specs=pl.BlockSpec((1, bc, tn), lambda l, j: (l, 0, j)),
        compiler_params=_cparams(("arbitrary", "arbitrary")),
        name="ada_mods",
    )(c_all, ada_w, ada_b.reshape(depth, 1, n))


def _in_proj_kernel(x_ref, g_ref, sc_ref, sh_ref, wh_ref, wl_ref, o_ref):
    h = _norm_mod(x_ref[...], g_ref[...], sc_ref[0], sh_ref[0])
    hh, hl = _split2(h)
    o_ref[...] = _mm_w(hh, hl, wh_ref[...], wl_ref[...])


def _in_proj(x, g, sc, sh, wh, wl, tm, tpb):
    m, d = x.shape
    n = wh.shape[1]
    r = sc.shape[1]
    return pl.pallas_call(
        _in_proj_kernel,
        out_shape=jax.ShapeDtypeStruct((m, n), F32),
        grid=(m // tm,),
        in_specs=[pl.BlockSpec((tm, d), lambda i: (i, 0)),
                  pl.BlockSpec((1, d), lambda i: (0, 0)),
                  pl.BlockSpec((1, r, d), lambda i: (i // tpb, 0, 0)),
                  pl.BlockSpec((1, r, d), lambda i: (i // tpb, 0, 0)),
                  pl.BlockSpec((d, n), lambda i: (0, 0)),
                  pl.BlockSpec((d, n), lambda i: (0, 0))],
        out_specs=pl.BlockSpec((tm, n), lambda i: (i, 0)),
        compiler_params=_cparams(("arbitrary",)),
        name="in_proj",
    )(x, g, sc, sh, wh, wl)


def _out_proj_kernel(x_ref, a1_ref, a2_ref, gate_ref, w1h, w1l, w2h, w2l, o_ref):
    a1h, a1l = _split2(a1_ref[...])
    a2h, a2l = _split2(a2_ref[...])
    acc = _mm_w(a1h, a1l, w1h[...], w1l[...]) + _mm_w(a2h, a2l, w2h[...], w2l[...])
    o_ref[...] = x_ref[...] + gate_ref[0] * acc


def _out_proj(x, a1, a2, gate, w1h, w1l, w2h, w2l, tm, tpb):
    m, d = x.shape
    k1 = a1.shape[1]
    k2 = a2.shape[1]
    r = gate.shape[1]
    wspec = lambda k: pl.BlockSpec((k, d), lambda i: (0, 0))
    return pl.pallas_call(
        _out_proj_kernel,
        out_shape=jax.ShapeDtypeStruct((m, d), F32),
        grid=(m // tm,),
        in_specs=[pl.BlockSpec((tm, d), lambda i: (i, 0)),
                  pl.BlockSpec((tm, k1), lambda i: (i, 0)),
                  pl.BlockSpec((tm, k2), lambda i: (i, 0)),
                  pl.BlockSpec((1, r, d), lambda i: (i // tpb, 0, 0)),
                  wspec(k1), wspec(k1), wspec(k2), wspec(k2)],
        out_specs=pl.BlockSpec((tm, d), lambda i: (i, 0)),
        compiler_params=_cparams(("arbitrary",)),
        name="out_proj",
    )(x, a1, a2, gate, w1h, w1l, w2h, w2l)


def _ffn_kernel(x_ref, g_ref, sc_ref, sh_ref, gate_ref, wgh, wgl, wuh, wul, wdh, wdl, o_ref, hh_s, hl_s, acc_s):
    j = pl.program_id(1)

    @pl.when(j == 0)
    def _():
        h = _norm_mod(x_ref[...], g_ref[...], sc_ref[0], sh_ref[0])
        hh, hl = _split2(h)
        hh_s[...] = hh
        hl_s[...] = hl
        acc_s[...] = jnp.zeros_like(acc_s)

    hh = hh_s[...]
    hl = hl_s[...]
    gate = _mm_w(hh, hl, wgh[...], wgl[...])
    up = _mm_w(hh, hl, wuh[...], wul[...])
    act = gate * _sigmoid(gate) * up
    ah, al = _split2(act)
    acc_s[...] += _mm_w(ah, al, wdh[...], wdl[...])

    @pl.when(j == pl.num_programs(1) - 1)
    def _():
        o_ref[...] = x_ref[...] + gate_ref[0] * acc_s[...]


def _ffn(x, g, sc, sh, gate, wuh, wul, wdh, wdl, tm, tpb):
    m, d = x.shape
    dff = wdh.shape[0]
    tf = 256
    nf = dff // tf
    r = sc.shape[1]
    mod = pl.BlockSpec((1, r, d), lambda i, j: (i // tpb, 0, 0))
    return pl.pallas_call(
        _ffn_kernel,
        out_shape=jax.ShapeDtypeStruct((m, d), F32),
        grid=(m // tm, nf),
        in_specs=[pl.BlockSpec((tm, d), lambda i, j: (i, 0)),
                  pl.BlockSpec((1, d), lambda i, j: (0, 0)),
                  mod, mod, mod,
                  pl.BlockSpec((d, tf), lambda i, j: (0, j)),
                  pl.BlockSpec((d, tf), lambda i, j: (0, j)),
                  pl.BlockSpec((d, tf), lambda i, j: (0, j + nf)),
                  pl.BlockSpec((d, tf), lambda i, j: (0, j + nf)),
                  pl.BlockSpec((tf, d), lambda i, j: (j, 0)),
                  pl.BlockSpec((tf, d), lambda i, j: (j, 0))],
        out_specs=pl.BlockSpec((tm, d), lambda i, j: (i, 0)),
        scratch_shapes=[pltpu.VMEM((tm, d), BF16), pltpu.VMEM((tm, d), BF16), pltpu.VMEM((tm, d), F32)],
        compiler_params=_cparams(("arbitrary", "arbitrary")),
        name="ffn",
    )(x, g, sc, sh, gate, wuh, wul, wuh, wul, wdh, wdl)


def _rwkv_prep_kernel(p_ref, prev_ref, mu_ref, wch, wcl, guh, gul, w0_ref, a0_ref, kk_ref, ka_ref,
                      r_o, ld_o, k_o, v_o, a_o, b_o, g_o, *scratch, seq_mode, tpb):
    p = p_ref[...]
    tm = p.shape[0]
    if seq_mode:
        carry = scratch[0]
        i = pl.program_id(0)

        @pl.when(i % tpb == 0)
        def _():
            carry[...] = prev_ref[0]

        rolled = pltpu.roll(p, 1, 0)
        prev = jnp.where(_iota((tm, 1), 0) == 0, carry[...], rolled)
        carry[...] = p[tm - 1:tm, :]
    else:
        prev = prev_ref[...]
    xs = p + mu_ref[...] * (prev - p)
    w_ = RWKV_WIDTH
    r = xs[:, 0:w_]
    k = xs[:, w_:2 * w_]
    v = xs[:, 2 * w_:3 * w_]
    z = xs[:, 3 * w_:3 * w_ + LANES]
    gd = xs[:, 3 * w_ + LANES:3 * w_ + 2 * LANES]
    z = jnp.where(_iota((tm, LANES), 1) < 64, jnp.tanh(z), z)
    zh, zl = _split2(z)
    wa = _mm_w(zh, zl, wch[...], wcl[...])
    zw = -(w0_ref[...] + wa[:, 0:w_])
    softplus = jnp.maximum(zw, 0.0) + jnp.log1p(jnp.exp(-jnp.abs(zw)))
    wlog = -softplus - 0.5
    a_sig = _sigmoid(a0_ref[...] + wa[:, w_:2 * w_])
    sh_, sl_ = _split2(_sigmoid(gd))
    g = _mm_w(sh_, sl_, guh[...], gul[...])
    bd = _head_blockdiag()
    kkv = k * kk_ref[...]
    nrm = jnp.sqrt(_segsum(kkv * kkv, bd))
    kkn = kkv / jnp.maximum(nrm, 1e-12)
    r_o[...] = r
    ld_o[...] = -jnp.exp(wlog)
    k_o[...] = k * (1.0 + (a_sig - 1.0) * ka_ref[...])
    v_o[...] = v
    a_o[...] = -kkn
    b_o[...] = kkn * a_sig
    g_o[...] = g


def _rwkv_prep(p_r, prev, lw, tm, tpb, seq_mode):
    m = p_r.shape[0]
    w_ = RWKV_WIDTH
    row = lambda n: pl.BlockSpec((1, n), lambda i: (0, 0))
    full = lambda a: pl.BlockSpec(a.shape, lambda i: (0, 0))
    if seq_mode:
        prev_spec = pl.BlockSpec((1, 1, SHIFT_COLS), lambda i: (i // tpb, 0, 0))
        scratch = [pltpu.VMEM((1, SHIFT_COLS), F32)]
    else:
        prev_spec = pl.BlockSpec((tm, SHIFT_COLS), lambda i: (i, 0))
        scratch = []
    out = jax.ShapeDtypeStruct((m, w_), F32)
    ospec = pl.BlockSpec((tm, w_), lambda i: (i, 0))
    return pl.pallas_call(
        functools.partial(_rwkv_prep_kernel, seq_mode=seq_mode, tpb=tpb),
        out_shape=[out] * 7,
        grid=(m // tm,),
        in_specs=[pl.BlockSpec((tm, SHIFT_COLS), lambda i: (i, 0)), prev_spec, row(SHIFT_COLS),
                  full(lw['wc_h']), full(lw['wc_l']), full(lw['gu_h']), full(lw['gu_l']),
                  row(w_), row(w_), row(w_), row(w_)],
        out_specs=[ospec] * 7,
        scratch_shapes=scratch,
        compiler_params=_cparams(("arbitrary",)),
        name="rwkv_prep",
    )(p_r, prev, lw['mu'], lw['wc_h'], lw['wc_l'], lw['gu_h'], lw['gu_l'], lw['w0'], lw['a0'], lw['k_k'], lw['k_a'])


def _rwkv_chunk_kernel(r_ref, ld_ref, k_ref, v_ref, a_ref, b_ref, q1_o, y0_o, g_o, ha_o):
    c = RWKV_CHUNK
    hd = HEAD_DIM
    ld = ld_ref[0]
    row = _iota((c, c), 0)
    col = _iota((c, c), 1)
    incl = row >= col
    strict = row > col
    eye = row == col
    lc = _mm_xl(incl.astype(BF16), ld)
    lx = lc - ld
    lend = lc[c - 1:c, :]
    p_in = jnp.exp(lc)
    inv_p = jnp.exp(-lc)
    p_to_end = jnp.exp(lend - lc)
    p_end = jnp.exp(lend)
    r = r_ref[0]
    k = k_ref[0]
    v = v_ref[0]
    b = b_ref[0]
    rt = r * p_in
    at = a_ref[0] * jnp.exp(lx)
    bt = b * inv_p
    kt = k * inv_p
    bh = b * p_to_end
    kh = k * p_to_end
    for h in range(RWKV_HEADS):
        sl = slice(hd * h, hd * (h + 1))
        at_h = at[:, sl]
        rt_h = rt[:, sl]
        v_h = v[:, sl]
        lhs = jnp.concatenate([at_h, rt_h], axis=0)
        xb = _mm(lhs, bt[:, sl], 1, 1)
        xk = _mm(lhs, kt[:, sl], 1, 1)
        a_ab = jnp.where(strict, xb[0:c], 0.0)
        a_rb = jnp.where(incl, xb[c:2 * c], 0.0)
        a_ak = jnp.where(strict, xk[0:c], 0.0)
        a_rk = jnp.where(incl, xk[c:2 * c], 0.0)
        x = jnp.where(eye, 1.0, 0.0) + a_ab
        mp = a_ab
        for _ in range(int(np.log2(c)) - 1):
            mp = _mm(mp, mp)
            x = x + _mm(mp, x)
        w1 = _mm(x, at_h)
        u0 = _mm(x, _mm(a_ak, v_h))
        q1_o[0, :, sl] = rt_h + _mm(a_rb, w1)
        y0_o[0, :, sl] = _mm(a_rb, u0) + _mm(a_rk, v_h)
        bh_h = bh[:, sl]
        g_o[0, 0, :, sl] = jnp.where(eye, p_end[:, sl], 0.0) + _mm(bh_h, w1, 0, 0)
        ha_o[0, 0, :, sl] = _mm(bh_h, u0, 0, 0) + _mm(kh[:, sl], v_h, 0, 0)


def _rwkv_chunk(r, ld, k, v, a, b):
    bsz, t, w_ = r.shape
    c = RWKV_CHUNK
    nch = t // c
    ispec = pl.BlockSpec((1, c, w_), lambda i, j: (i, j, 0))
    sspec = pl.BlockSpec((1, 1, HEAD_DIM, w_), lambda i, j: (i, j, 0, 0))
    return pl.pallas_call(
        _rwkv_chunk_kernel,
        out_shape=[jax.ShapeDtypeStruct((bsz, t, w_), F32)] * 2 + [jax.ShapeDtypeStruct((bsz, nch, HEAD_DIM, w_), F32)] * 2,
        grid=(bsz, nch),
        in_specs=[ispec] * 6,
        out_specs=[ispec, ispec, sspec, sspec],
        compiler_params=_cparams(("arbitrary", "arbitrary")),
        name="rwkv_chunk",
    )(r, ld, k, v, a, b)


def _rwkv_scan_kernel(q1_ref, y0_ref, g_ref, ha_ref, y_o, hf_o, h_s):
    j = pl.program_id(0)

    @pl.when(j == 0)
    def _():
        h_s[...] = jnp.zeros_like(h_s)

    hd = HEAD_DIM
    for bi in range(q1_ref.shape[0]):
        for h in range(RWKV_HEADS):
            sl = slice(hd * h, hd * (h + 1))
            h0 = h_s[bi, :, sl]
            y_o[bi, :, sl] = _mm(q1_ref[bi, :, sl], h0) + y0_ref[bi, :, sl]
            h_s[bi, :, sl] = _mm(g_ref[bi, 0, :, sl], h0) + ha_ref[bi, 0, :, sl]

    @pl.when(j == pl.num_programs(0) - 1)
    def _():
        hf_o[...] = h_s[...]


def _rwkv_scan(q1, y0, g, ha):
    bsz, t, w_ = q1.shape
    c = RWKV_CHUNK
    ispec = pl.BlockSpec((bsz, c, w_), lambda j: (0, j, 0))
    sspec = pl.BlockSpec((bsz, 1, HEAD_DIM, w_), lambda j: (0, j, 0, 0))
    return pl.pallas_call(
        _rwkv_scan_kernel,
        out_shape=[jax.ShapeDtypeStruct((bsz, t, w_), F32), jax.ShapeDtypeStruct((bsz, HEAD_DIM, w_), F32)],
        grid=(t // c,),
        in_specs=[ispec, ispec, sspec, sspec],
        out_specs=[ispec, pl.BlockSpec((bsz, HEAD_DIM, w_), lambda j: (0, 0, 0))],
        scratch_shapes=[pltpu.VMEM((bsz, HEAD_DIM, w_), F32)],
        compiler_params=_cparams(("arbitrary",)),
        name="rwkv_scan",
    )(q1, y0, g, ha)


def _rwkv_step_kernel(s_ref, r_ref, ld_ref, k_ref, a_ref, b_ref, v_ref, so_ref, y_ref):
    s = s_ref[...]
    sa = jnp.sum(s * a_ref[...], axis=-1, keepdims=True)
    s_new = s * jnp.exp(ld_ref[...]) + sa * b_ref[...] + v_ref[...] * k_ref[...]
    so_ref[...] = s_new
    y_ref[...] = jnp.sum(s_new * r_ref[...], axis=-1, keepdims=True)


def _rwkv_step(state, layer, r, ld, k, a, b, v):
    bs = r.shape[0]
    nh, hd = RWKV_HEADS, HEAD_DIM
    bt = 8
    rowv = lambda x: x.reshape(bs, nh, 1, hd)
    vspec = pl.BlockSpec((bt, nh, 1, hd), lambda i: (i, 0, 0, 0))
    cspec = pl.BlockSpec((bt, nh, hd, 1), lambda i: (i, 0, 0, 0))
    off = layer * (bs // bt)
    return pl.pallas_call(
        _rwkv_step_kernel,
        out_shape=[jax.ShapeDtypeStruct((bs, nh, hd, hd), F32), jax.ShapeDtypeStruct((bs, nh, hd, 1), F32)],
        grid=(bs // bt,),
        in_specs=[pl.BlockSpec((bt, nh, hd, hd), lambda i: (i + off, 0, 0, 0)),
                  vspec, vspec, vspec, vspec, vspec, cspec],
        out_specs=[pl.BlockSpec((bt, nh, hd, hd), lambda i: (i, 0, 0, 0)), cspec],
        compiler_params=_cparams(("arbitrary",)),
        name="rwkv_step",
    )(state, rowv(r), rowv(ld), rowv(k), rowv(a), rowv(b), v.reshape(bs, nh, hd, 1))


def _rwkv_post_kernel(y_ref, r_ref, k_ref, v_ref, g_ref, rk_ref, lw_ref, lb_ref, o_ref):
    bd = _head_blockdiag()
    y = y_ref[...]
    inv = 1.0 / HEAD_DIM
    mu = _segsum(y, bd) * inv
    d = y - mu
    var = _segsum(d * d, bd) * inv
    yn = d * lax.rsqrt(var + GN_EPS) * lw_ref[...] + lb_ref[...]
    v = v_ref[...]
    bonus = _segsum(r_ref[...] * k_ref[...] * rk_ref[...], bd) * v
    o_ref[...] = (yn + bonus) * g_ref[...]


def _rwkv_post(y, r, k, v, g, lw, tm):
    m, w_ = y.shape
    tile = pl.BlockSpec((tm, w_), lambda i: (i, 0))
    row = pl.BlockSpec((1, w_), lambda i: (0, 0))
    return pl.pallas_call(
        _rwkv_post_kernel,
        out_shape=jax.ShapeDtypeStruct((m, w_), F32),
        grid=(m // tm,),
        in_specs=[tile] * 5 + [row] * 3,
        out_specs=tile,
        compiler_params=_cparams(("arbitrary",)),
        name="rwkv_post",
    )(y, r, k, v, g, lw['r_k'], lw['ln_w'], lw['ln_b'])


def _nsa_proj_kernel(p_ref, cos_ref, sin_ref, nw_ref, q_o, qr_o, cmp_o, sel_o, win_o, gt_o):
    tm = p_ref.shape[0]
    cos = cos_ref[...]
    sin = sin_ref[...]
    bd = _head_blockdiag()
    first = (_iota((tm, LANES), 1) % HEAD_DIM) < HEAD_DIM // 2
    nw = nw_ref[...]

    def norm(x, w):
        ms = _segsum(x * x, bd) * (1.0 / HEAD_DIM)
        return x * lax.rsqrt(ms + NORM_EPS) * w

    def rope(x):
        rot = jnp.where(first, pltpu.roll(x, LANES - HEAD_DIM // 2, 1), pltpu.roll(x, HEAD_DIM // 2, 1))
        return x * cos + rot * sin

    for j in range(NSA_WIDTH // LANES):
        sl = slice(LANES * j, LANES * (j + 1))
        xn = norm(p_ref[:, sl], nw[0:1])
        q_o[:, sl] = xn
        qr_o[:, sl] = rope(xn)
    o = NSA_WIDTH
    cmp_o[...] = p_ref[:, o:o + 2 * KV_WIDTH]
    sel_o[:, 0:LANES] = rope(norm(p_ref[:, o + 256:o + 384], nw[2:3]))
    sel_o[:, LANES:2 * LANES] = p_ref[:, o + 384:o + 512]
    win_o[:, 0:LANES] = rope(norm(p_ref[:, o + 512:o + 640], nw[3:4]))
    win_o[:, LANES:2 * LANES] = p_ref[:, o + 640:o + 768]
    gt_o[...] = _sigmoid(p_ref[:, o + 768:o + 896])


def _nsa_proj(p_n, cos_t, sin_t, nw, tm, tpb):
    m = p_n.shape[0]
    tt = cos_t.shape[0] // tm
    tab = pl.BlockSpec((tm, LANES), lambda i: (i % tt, 0))
    shapes = [(m, NSA_WIDTH), (m, NSA_WIDTH), (m, 256), (m, 256), (m, 256), (m, LANES)]
    return pl.pallas_call(
        _nsa_proj_kernel,
        out_shape=[jax.ShapeDtypeStruct(s, F32) for s in shapes],
        grid=(m // tm,),
        in_specs=[pl.BlockSpec((tm, NSA_COLS_PAD), lambda i: (i, 0)), tab, tab,
                  pl.BlockSpec((4, LANES), lambda i: (0, 0))],
        out_specs=[pl.BlockSpec((tm, s[1]), lambda i: (i, 0)) for s in shapes],
        compiler_params=_cparams(("arbitrary",)),
        name="nsa_proj",
    )(p_n, cos_t, sin_t, nw)


def _cmp1_kernel(*refs, n_parts, paged):
    if paged:
        refs = refs[1:]
    parts = refs[:n_parts]
    wh_ref, wl_ref, o_ref, xs = refs[n_parts:]
    gpp = parts[0].shape[1] // CMP_STRIDE
    for j, pr in enumerate(parts):
        for l in range(CMP_STRIDE):
            xs[gpp * j:gpp * (j + 1), LANES * l:LANES * (l + 1)] = pr[0, pl.ds(l, gpp, stride=CMP_STRIDE), :]
    xh, xl = _split2(xs[...])
    o_ref[0, 0] = _mm_w(xh, xl, wh_ref[0], wl_ref[0])


def _cmp1_prompt(cmp_rows, wh, wl):
    bsz, t, _ = cmp_rows.shape
    rows = min(2048, t)
    ng = rows // CMP_STRIDE
    kdim = CMP_STRIDE * LANES
    return pl.pallas_call(
        functools.partial(_cmp1_kernel, n_parts=1, paged=False),
        out_shape=jax.ShapeDtypeStruct((bsz, 2, t // CMP_STRIDE, 512), F32),
        grid=(bsz, t // rows, 2),
        in_specs=[pl.BlockSpec((1, rows, LANES), lambda b, i, kv: (b, i, kv)),
                  pl.BlockSpec((1, kdim, 512), lambda b, i, kv: (kv, 0, 0)),
                  pl.BlockSpec((1, kdim, 512), lambda b, i, kv: (kv, 0, 0))],
        out_specs=pl.BlockSpec((1, 1, ng, 512), lambda b, i, kv: (b, kv, i, 0)),
        scratch_shapes=[pltpu.VMEM((ng, kdim), F32)],
        compiler_params=_cparams(("arbitrary", "arbitrary", "arbitrary")),
        name="cmp1_prompt",
    )(cmp_rows, wh, wl)


def _cmp1_sample(page_table, cache, page_off, wh, wl):
    bs, n_pages = page_table.shape
    page = cache.shape[1]
    ng = n_pages * page // CMP_STRIDE
    kdim = CMP_STRIDE * LANES

    def page_spec(j):
        return pl.BlockSpec((1, page, LANES), lambda b, kv, pt: (pt[b, j] + page_off, 0, kv))

    return pl.pallas_call(
        functools.partial(_cmp1_kernel, n_parts=n_pages, paged=True),
        out_shape=jax.ShapeDtypeStruct((bs, 2, ng, 512), F32),
        grid_spec=pltpu.PrefetchScalarGridSpec(
            num_scalar_prefetch=1,
            grid=(bs, 2),
            in_specs=[page_spec(j) for j in range(n_pages)] + [
                pl.BlockSpec((1, kdim, 512), lambda b, kv, pt: (kv, 0, 0)),
                pl.BlockSpec((1, kdim, 512), lambda b, kv, pt: (kv, 0, 0))],
            out_specs=pl.BlockSpec((1, 1, ng, 512), lambda b, kv, pt: (b, kv, 0, 0)),
            scratch_shapes=[pltpu.VMEM((ng, kdim), F32)]),
        compiler_params=_cparams(("arbitrary", "arbitrary")),
        name="cmp1_sample",
    )(page_table, *([cache] * n_pages), wh, wl)


def _cmp2_kernel(ab_ref, pos_ref, w1_ref, b1_ref, w2_ref, b2_ref, nw_ref, o_ref):
    kv = pl.program_id(1)
    ab = ab_ref[0, 0]
    ng = ab.shape[0]
    a_part = ab[:, 0:256]
    b_next = pltpu.roll(ab[:, 256:512], ng - 1, 0)
    posb = jnp.broadcast_to(pos_ref[0], (8, pos_ref.shape[2]))
    c1 = _mm(posb, w1_ref[0])[0:1] + b1_ref[0]
    h = a_part + b_next + jnp.concatenate([c1, c1], axis=1)
    h = 0.5 * h * (1.0 + jnp.tanh(np.sqrt(2.0 / np.pi) * (h + 0.044715 * (h * h * h))))
    out = _mm(h, w2_ref[0]) + b2_ref[0]
    ms = _segsum(out * out, _head_blockdiag()) * (1.0 / HEAD_DIM)
    normed = out * lax.rsqrt(ms + NORM_EPS) * nw_ref[...]
    o_ref[0, 0] = jnp.where(kv == 0, normed, out)


def _cmp2(ab, lw):
    bsz, _, ng, _ = ab.shape
    kdim = CMP_BLOCK * HEAD_DIM
    return pl.pallas_call(
        _cmp2_kernel,
        out_shape=jax.ShapeDtypeStruct((bsz, 2, ng, LANES), F32),
        grid=(bsz, 2),
        in_specs=[pl.BlockSpec((1, 1, ng, 512), lambda b, kv: (b, kv, 0, 0)),
                  pl.BlockSpec((1, 1, kdim), lambda b, kv: (kv, 0, 0)),
                  pl.BlockSpec((1, kdim, CMP_HIDDEN), lambda b, kv: (kv, 0, 0)),
                  pl.BlockSpec((1, 1, CMP_HIDDEN), lambda b, kv: (kv, 0, 0)),
                  pl.BlockSpec((1, 2 * CMP_HIDDEN, LANES), lambda b, kv: (kv, 0, 0)),
                  pl.BlockSpec((1, 1, LANES), lambda b, kv: (kv, 0, 0)),
                  pl.BlockSpec((1, LANES), lambda b, kv: (0, 0))],
        out_specs=pl.BlockSpec((1, 1, ng, LANES), lambda b, kv: (b, kv, 0, 0)),
        compiler_params=_cparams(("arbitrary", "arbitrary")),
        name="cmp2",
    )(ab, lw['cmp_pos'], lw['cmp_w1'], lw['cmp_b1'], lw['cmp_w2bd'], lw['cmp_b2t'], lw['nw_kc'])


def _masked_softmax(s, mask):
    s = jnp.where(mask, s, NEG_INF)
    mx = jnp.max(s, axis=-1, keepdims=True)
    e = jnp.where(mask, jnp.exp(s - mx), 0.0)
    den = jnp.sum(e, axis=-1, keepdims=True)
    return e / jnp.maximum(den, 1e-30)


def _selection_mask(imp, qpos):
    shape = imp.shape
    blk = _iota(shape, 1)
    cur = qpos // SEL_BLOCK
    valid = blk * SEL_BLOCK <= qpos
    forced = (blk == 0) | (blk == cur) | (blk == cur - 1)
    score = jnp.where(valid, jnp.where(forced, FORCE, imp), NEG_INF)
    return _topk_mask(score, SEL_TOP_N)


def _attn_prompt_kernel(q_ref, qr_ref, gt_ref, kc_ref, vc_ref, ks_ref, vs_ref, *rest, ck):
    nwb = WINDOW // Q_TILE + 1
    kw_refs = rest[0:nwb]
    vw_refs = rest[nwb:2 * nwb]
    ov_ref, o_ref, qg_s, qrg_s, m_s, l_s, acc_s = rest[2 * nwb:]
    qi = pl.program_id(1)
    tq = Q_TILE
    hpg = NSA_HEADS // 2
    scale = HEAD_DIM ** -0.5
    ng = kc_ref.shape[2]
    lo_lanes = _iota((tq, LANES), 1) < HEAD_DIM
    qpos = qi * tq + _iota((tq, 1), 0)
    qpos4 = jnp.concatenate([qpos] * hpg, axis=0)
    gt = gt_ref[0]
    kc = kc_ref[0, 0]
    vc = vc_ref[0, 0]
    kw = jnp.concatenate([r[0] for r in kw_refs], axis=0)
    vw = jnp.concatenate([r[0] for r in vw_refs], axis=0)
    kwpos = jnp.concatenate([(qi - (nwb - 1) + j) * tq + _iota((1, tq), 1) for j in range(nwb)], axis=1)
    for g in range(2):
        keep = lo_lanes if g == 0 else jnp.logical_not(lo_lanes)
        for hh in range(hpg):
            h = hpg * g + hh
            cb = h // 2
            for src, dst in ((q_ref, qg_s), (qr_ref, qrg_s)):
                blk = src[0, :, LANES * cb:LANES * (cb + 1)]
                if (h % 2 == 0) != (g == 0):
                    blk = pltpu.roll(blk, HEAD_DIM, 1)
                dst[tq * hh:tq * (hh + 1), :] = jnp.where(keep, blk, 0.0)
        qg = qg_s[...]
        qrg = qrg_s[...]
        s = _mm(qg, kc, 1, 1) * scale
        mc = (_iota((hpg * tq, ng), 1) * CMP_STRIDE + (CMP_BLOCK - 1)) <= qpos4
        p = _masked_softmax(s, mc)
        oc = _mm(p, vc)
        psum = p[0:tq] + p[tq:2 * tq] + p[2 * tq:3 * tq] + p[3 * tq:4 * tq]
        imp = _mm_xr(psum, ov_ref[...])
        selb = _selection_mask(imp, qpos).astype(BF16)
        m_s[...] = jnp.full(m_s.shape, NEG_INF, F32)
        l_s[...] = jnp.zeros(l_s.shape, F32)
        acc_s[...] = jnp.zeros(acc_s.shape, F32)

        def body(c, carry):
            st = pl.multiple_of(c * ck, ck)
            kch = ks_ref[0, pl.ds(st, ck), :]
            vch = vs_ref[0, pl.ds(st, ck), :]
            sc = _mm(qrg, kch, 1, 1) * scale
            kpos = st + _iota((tq, ck), 1)
            expand = (_iota((LANES, ck), 0) == (st + _iota((LANES, ck), 1)) // SEL_BLOCK).astype(BF16)
            ok = (_dg(selb, expand) > 0.5) & (kpos <= qpos)
            ok4 = jnp.concatenate([ok] * hpg, axis=0)
            sc = jnp.where(ok4, sc, NEG_INF)
            m_old = m_s[...]
            m_new = jnp.maximum(m_old, jnp.max(sc, axis=-1, keepdims=True))
            alpha = jnp.exp(m_old - m_new)
            pe = jnp.where(ok4, jnp.exp(sc - m_new), 0.0)
            l_s[...] = alpha * l_s[...] + jnp.sum(pe, axis=-1, keepdims=True)
            acc_s[...] = alpha * acc_s[...] + _mm(pe, vch)
            m_s[...] = m_new
            return carry

        lax.fori_loop(0, (qi * tq + tq + ck - 1) // ck, body, 0)
        osel = acc_s[...] / l_s[...]
        sw = _mm(qrg, kw, 1, 1) * scale
        mw = (kwpos <= qpos4) & (kwpos > qpos4 - WINDOW) & (kwpos >= 0)
        ow = _mm(_masked_softmax(sw, mw), vw)
        for hh in range(hpg):
            h = hpg * g + hh
            rs = slice(tq * hh, tq * (hh + 1))
            og = (gt[:, 3 * h:3 * h + 1] * oc[rs] + gt[:, 3 * h + 1:3 * h + 2] * osel[rs]
                  + gt[:, 3 * h + 2:3 * h + 3] * ow[rs])
            o_ref[0, :, HEAD_DIM * h:HEAD_DIM * (h + 1)] = og[:, HEAD_DIM * g:HEAD_DIM * (g + 1)]


def _overlap_matrix(ng):
    s = np.arange(ng)[:, None] * CMP_STRIDE
    j = np.arange(LANES)[None, :] * SEL_BLOCK
    return jnp.asarray(((s < j + SEL_BLOCK) & (s + CMP_BLOCK > j)).astype(np.float32), dtype=BF16)


def _attn_prompt(q, qr, gt, ckv, sel, win):
    bsz, t, _ = q.shape
    tq = Q_TILE
    ng = ckv.shape[2]
    ck = min(512, t)
    nwb = WINDOW // tq + 1
    hpg = NSA_HEADS // 2
    assert t // SEL_BLOCK <= LANES and t % ck == 0

    def wspec(j, col):
        return pl.BlockSpec((1, tq, LANES), lambda b, i: (b, jnp.maximum(i - (nwb - 1) + j, 0), col))

    tile = lambda w: pl.BlockSpec((1, tq, w), lambda b, i: (b, i, 0))
    return pl.pallas_call(
        functools.partial(_attn_prompt_kernel, ck=ck),
        out_shape=jax.ShapeDtypeStruct((bsz, t, NSA_WIDTH), F32),
        grid=(bsz, t // tq),
        in_specs=[tile(NSA_WIDTH), tile(NSA_WIDTH), tile(LANES),
                  pl.BlockSpec((1, 1, ng, LANES), lambda b, i: (b, 0, 0, 0)),
                  pl.BlockSpec((1, 1, ng, LANES), lambda b, i: (b, 1, 0, 0)),
                  pl.BlockSpec((1, t, LANES), lambda b, i: (b, 0, 0)),
                  pl.BlockSpec((1, t, LANES), lambda b, i: (b, 0, 1))]
                 + [wspec(j, 0) for j in range(nwb)] + [wspec(j, 1) for j in range(nwb)]
                 + [pl.BlockSpec((ng, LANES), lambda b, i: (0, 0))],
        out_specs=tile(NSA_WIDTH),
        scratch_shapes=[pltpu.VMEM((hpg * tq, LANES), F32), pltpu.VMEM((hpg * tq, LANES), F32),
                        pltpu.VMEM((hpg * tq, 1), F32), pltpu.VMEM((hpg * tq, 1), F32),
                        pltpu.VMEM((hpg * tq, LANES), F32)],
        compiler_params=_cparams(("arbitrary", "arbitrary")),
        name="attn_prompt",
    )(q, qr, gt, ckv, ckv, sel, sel, *([win] * (2 * nwb)), _overlap_matrix(ng))


def _attn_sample_kernel(pt_ref, q_ref, qr_ref, gt_ref, ckv_ref, *rest, past_len, n_pages):
    pages = rest[:n_pages]
    nsel_ref, win_ref, nwin_ref, ov_ref, o_ref, wout_ref = rest[n_pages:]
    nh = NSA_HEADS
    scale = HEAD_DIM ** -0.5
    qpos = past_len
    q = q_ref[0]
    qr = qr_ref[0]
    gt = gt_ref[0]
    kc = ckv_ref[0, 0]
    vc = ckv_ref[0, 1]
    ng = kc.shape[0]
    s = _mm(q, kc, 1, 1) * scale
    mc = (_iota((nh, ng), 1) * CMP_STRIDE + (CMP_BLOCK - 1)) <= qpos
    p = _masked_softmax(s, mc)
    oc = _mm(p, vc)
    same_group = (_iota((nh, nh), 0) // (nh // 2) == _iota((nh, nh), 1) // (nh // 2)).astype(BF16)
    imp = _mm_xr(_mm_xl(same_group, p), ov_ref[...])
    selb = _selection_mask(imp, qpos).astype(BF16)
    page = pages[0].shape[1]
    nsel = nsel_ref[0]
    s_new = jnp.sum(qr * nsel[:, 0:LANES], axis=-1, keepdims=True) * scale
    scores = []
    oks = []
    m = s_new
    for j, pg in enumerate(pages):
        sj = _mm(qr, pg[0, :, 0:LANES], 1, 1) * scale
        expand = (_iota((LANES, page), 0) == (j * page + _iota((LANES, page), 1)) // SEL_BLOCK).astype(BF16)
        ok = _dg(selb, expand) > 0.5
        sj = jnp.where(ok, sj, NEG_INF)
        m = jnp.maximum(m, jnp.max(sj, axis=-1, keepdims=True))
        scores.append(sj)
        oks.append(ok)
    e_new = jnp.exp(s_new - m)
    den = e_new
    acc = e_new * nsel[:, LANES:2 * LANES]
    for j, pg in enumerate(pages):
        e = jnp.where(oks[j], jnp.exp(scores[j] - m), 0.0)
        den = den + jnp.sum(e, axis=-1, keepdims=True)
        acc = acc + _mm(e, pg[0, :, LANES:2 * LANES])
    osel = acc / den
    wb = win_ref.shape[1]
    wnd = win_ref[0]
    nwin = nwin_ref[0]
    sw = _mm(qr, wnd[:, 0:LANES], 1, 1) * scale
    kwpos = past_len - wb + _iota((1, wb), 1)
    mw = (kwpos <= qpos) & (kwpos > qpos - WINDOW) & (kwpos >= 0)
    sw = jnp.where(mw, sw, NEG_INF)
    sw_new = jnp.sum(qr * nwin[:, 0:LANES], axis=-1, keepdims=True) * scale
    mx = jnp.maximum(jnp.max(sw, axis=-1, keepdims=True), sw_new)
    ew = jnp.where(mw, jnp.exp(sw - mx), 0.0)
    ew_new = jnp.exp(sw_new - mx)
    ow = (_mm(ew, wnd[:, LANES:2 * LANES]) + ew_new * nwin[:, LANES:2 * LANES]) / (
        jnp.sum(ew, axis=-1, keepdims=True) + ew_new)
    o_ref[0] = gt[:, 0:1] * oc + gt[:, 1:2] * osel + gt[:, 2:3] * ow
    shifted = pltpu.roll(wnd, wb - 1, 0)
    wout_ref[0] = jnp.where(_iota((wb, 1), 0) == wb - 1, nwin, shifted)


def _attn_sample(page_table, q, qr, gt, ckv, cache_sel, page_off, nsel, cache_win, win_off, nwin, past_len):
    bs, n_pages = page_table.shape
    page = cache_sel.shape[1]
    wb = cache_win.shape[1]
    ng = ckv.shape[2]
    nh = NSA_HEADS
    assert wb == WINDOW and past_len // SEL_BLOCK + 1 <= LANES

    def page_spec(j):
        return pl.BlockSpec((1, page, 256), lambda b, pt: (pt[b, j] + page_off, 0, 0))

    vec = lambda w: pl.BlockSpec((1, nh, w), lambda b, pt: (b, 0, 0))
    rowspec = pl.BlockSpec((1, 1, 256), lambda b, pt: (b, 0, 0))
    return pl.pallas_call(
        functools.partial(_attn_sample_kernel, past_len=past_len, n_pages=n_pages),
        out_shape=[jax.ShapeDtypeStruct((bs, nh, LANES), F32), jax.ShapeDtypeStruct((bs, wb, 256), F32)],
        grid_spec=pltpu.PrefetchScalarGridSpec(
            num_scalar_prefetch=1,
            grid=(bs,),
            in_specs=[vec(LANES), vec(LANES), vec(LANES),
                      pl.BlockSpec((1, 2, ng, LANES), lambda b, pt: (b, 0, 0, 0))]
                     + [page_spec(j) for j in range(n_pages)]
                     + [rowspec,
                        pl.BlockSpec((1, wb, 256), lambda b, pt: (b + win_off, 0, 0)),
                        rowspec,
                        pl.BlockSpec((ng, LANES), lambda b, pt: (0, 0))],
            out_specs=[vec(LANES), pl.BlockSpec((1, wb, 256), lambda b, pt: (b, 0, 0))]),
        compiler_params=_cparams(("arbitrary",)),
        name="attn_sample",
    )(page_table, q, qr, gt, ckv, *([cache_sel] * n_pages), nsel, cache_win, nwin, _overlap_matrix(ng))


def _hilo(w):
    hi = w.astype(BF16)
    return hi, (w - hi.astype(F32)).astype(BF16)


def _tile_heads(v, reps):
    return jnp.tile(v.reshape(1, -1), (1, reps))


def _layer_weights(l, norm_mix, norm_ffn, w_in, rwkv_mu, rwkv_w0, rwkv_w_up, rwkv_a0, rwkv_a_up, rwkv_g_up,
                   rwkv_k_k, rwkv_k_a, rwkv_r_k, rwkv_ln_w, rwkv_ln_b, qk_norm, cmp_pos, cmp_w1, cmp_b1, cmp_w2,
                   cmp_b2, w_out, ffn_w_up, ffn_w_down):
    lw = {}
    d = w_in.shape[1]
    lw['norm_mix'] = norm_mix[l].reshape(1, d)
    lw['norm_ffn'] = norm_ffn[l].reshape(1, d)
    lw['wr_h'], lw['wr_l'] = _hilo(w_in[l][:, :SHIFT_COLS])
    w_n = jnp.pad(w_in[l][:, SHIFT_COLS:], ((0, 0), (0, NSA_COLS_PAD - NSA_COLS)))
    lw['wn_h'], lw['wn_l'] = _hilo(w_n)
    lw['mu'] = rwkv_mu[l].reshape(1, -1)
    lw['w0'] = rwkv_w0[l].reshape(1, -1)
    lw['a0'] = rwkv_a0[l].reshape(1, -1)
    lw['k_k'] = rwkv_k_k[l].reshape(1, -1)
    lw['k_a'] = rwkv_k_a[l].reshape(1, -1)
    lw['r_k'] = rwkv_r_k[l].reshape(1, -1)
    lw['ln_w'] = rwkv_ln_w[l].reshape(1, -1)
    lw['ln_b'] = rwkv_ln_b[l].reshape(1, -1)
    zeros = jnp.zeros_like(rwkv_w_up[l])
    wc = jnp.concatenate([jnp.concatenate([rwkv_w_up[l], zeros], axis=1),
                          jnp.concatenate([zeros, rwkv_a_up[l]], axis=1)], axis=0)
    lw['wc_h'], lw['wc_l'] = _hilo(wc)
    lw['gu_h'], lw['gu_l'] = _hilo(rwkv_g_up[l])
    qn = qk_norm[l]
    lw['nw'] = jnp.tile(qn, (1, 2))
    lw['nw_kc'] = jnp.tile(qn[1:2], (1, 2))
    w1 = cmp_w1[l].reshape(2, 2, CMP_STRIDE, HEAD_DIM, CMP_HIDDEN)
    eye = jnp.eye(2, dtype=F32)
    w1g = jnp.einsum('khldj,gf->klgdhfj', w1, eye)
    w1g = w1g.reshape(2, CMP_STRIDE * 2 * HEAD_DIM, 2 * 2 * CMP_HIDDEN)
    lw['c1_h'], lw['c1_l'] = _hilo(w1g)
    lw['cmp_pos'] = cmp_pos[l].reshape(2, 1, CMP_BLOCK * HEAD_DIM)
    lw['cmp_w1'] = cmp_w1[l]
    lw['cmp_b1'] = cmp_b1[l].reshape(2, 1, CMP_HIDDEN)
    w2 = cmp_w2[l]
    lw['cmp_w2bd'] = jnp.einsum('kjd,gf->kgjfd', w2, eye).reshape(2, 2 * CMP_HIDDEN, 2 * HEAD_DIM)
    lw['cmp_b2t'] = jnp.tile(cmp_b2[l].reshape(2, 1, HEAD_DIM), (1, 1, 2))
    lw['wo1_h'], lw['wo1_l'] = _hilo(w_out[l][:RWKV_WIDTH])
    lw['wo2_h'], lw['wo2_l'] = _hilo(w_out[l][RWKV_WIDTH:])
    lw['wu_h'], lw['wu_l'] = _hilo(ffn_w_up[l])
    lw['wd_h'], lw['wd_l'] = _hilo(ffn_w_down[l])
    return lw


def _rope_tables(pos):
    half = HEAD_DIM // 2
    inv_freq = ROPE_THETA ** (-jnp.arange(half, dtype=F32) / half)
    ang = pos.astype(F32)[:, None] * inv_freq[None, :]
    cos = jnp.cos(ang)
    sin = jnp.sin(ang)
    return jnp.tile(cos, (1, 4)), jnp.tile(jnp.concatenate([-sin, sin], axis=1), (1, 2))


def _mods6(m):
    d = m.shape[-1] // 6
    return [m[..., i * d:(i + 1) * d] for i in range(6)]


def _prompt_layer(x, mods, lw, tables):
    bsz, t, d = x.shape
    m = bsz * t
    tm = min(512, t)
    tpb = t // tm
    sh1, sc1, g1, sh2, sc2, g2 = [a.reshape(bsz, 1, d) for a in _mods6(mods)]
    x2 = x.reshape(m, d)
    p_r = _in_proj(x2, lw['norm_mix'], sc1, sh1, lw['wr_h'], lw['wr_l'], tm, tpb)
    p_n = _in_proj(x2, lw['norm_mix'], sc1, sh1, lw['wn_h'], lw['wn_l'], tm, tpb)
    shift0 = jnp.zeros((bsz, 1, SHIFT_COLS), F32)
    r, ld, k, v, a, b, g = _rwkv_prep(p_r, shift0, lw, tm, tpb, True)
    seq = lambda z: z.reshape(bsz, t, RWKV_WIDTH)
    q1, y0, gm, ha = _rwkv_chunk(seq(r), seq(ld), seq(k), seq(v), seq(a), seq(b))
    y, h_fin = _rwkv_scan(q1, y0, gm, ha)
    y_r = _rwkv_post(y.reshape(m, RWKV_WIDTH), r, k, v, g, lw, tm)
    s_new = h_fin.reshape(bsz, HEAD_DIM, RWKV_HEADS, HEAD_DIM).transpose(0, 2, 3, 1)
    shift_new = p_r.reshape(bsz, t, SHIFT_COLS)[:, -1]
    q, qr, cmp_rows, sel_rows, win_rows, gt = _nsa_proj(p_n, tables[0], tables[1], lw['nw'], tm, tpb)
    seqw = lambda z: z.reshape(bsz, t, z.shape[-1])
    ab = _cmp1_prompt(seqw(cmp_rows), lw['c1_h'], lw['c1_l'])
    ckv = _cmp2(ab, lw)
    y_n = _attn_prompt(seqw(q), seqw(qr), seqw(gt), ckv, seqw(sel_rows), seqw(win_rows))
    x2 = _out_proj(x2, y_r, y_n.reshape(m, NSA_WIDTH), g1, lw['wo1_h'], lw['wo1_l'], lw['wo2_h'], lw['wo2_l'], tm, tpb)
    x2 = _ffn(x2, lw['norm_ffn'], sc2, sh2, g2, lw['wu_h'], lw['wu_l'], lw['wd_h'], lw['wd_l'], tm, tpb)
    rows6 = lambda z: z.reshape(bsz, t, 2, 2, HEAD_DIM)
    wlen = min(WINDOW, t)
    return (x2.reshape(bsz, t, d), s_new, shift_new, rows6(cmp_rows), rows6(sel_rows), rows6(win_rows)[:, -wlen:])


def _sample_layer(x, mods, lw, tables, l, state_rwkv, state_shift, page_table, cache_cmp, cache_sel, cache_win,
                  n_pool, past_len):
    bs, d = x.shape
    tm = bs
    sh1, sc1, g1, sh2, sc2, g2 = [a.reshape(1, bs, d) for a in _mods6(mods)]
    p_r = _in_proj(x, lw['norm_mix'], sc1, sh1, lw['wr_h'], lw['wr_l'], tm, 1)
    p_n = _in_proj(x, lw['norm_mix'], sc1, sh1, lw['wn_h'], lw['wn_l'], tm, 1)
    r, ld, k, v, a, b, g = _rwkv_prep(p_r, state_shift[l], lw, tm, 1, False)
    s_new, y = _rwkv_step(state_rwkv, l, r, ld, k, a, b, v)
    y_r = _rwkv_post(y.reshape(bs, RWKV_WIDTH), r, k, v, g, lw, tm)
    q, qr, cmp_rows, sel_rows, win_rows, gt = _nsa_proj(p_n, tables[0], tables[1], lw['nw'], tm, 1)
    ab = _cmp1_sample(page_table, cache_cmp, l * n_pool, lw['c1_h'], lw['c1_l'])
    ckv = _cmp2(ab, lw)
    grp = (jnp.arange(NSA_HEADS) // (NSA_HEADS // 2))[None, :, None]
    half = (jnp.arange(LANES) // HEAD_DIM)[None, None, :]

    def pad_heads(z):
        z = z.reshape(bs, NSA_HEADS, HEAD_DIM)
        return jnp.where(grp == half, jnp.tile(z, (1, 1, 2)), 0.0)

    gates = jnp.pad(gt[:, :3 * NSA_HEADS].reshape(bs, NSA_HEADS, 3), ((0, 0), (0, 0), (0, LANES - 3)))
    o, win_new = _attn_sample(page_table, pad_heads(q), pad_heads(qr), gates, ckv, cache_sel, l * n_pool,
                              sel_rows.reshape(bs, 1, 256), cache_win, l * bs, win_rows.reshape(bs, 1, 256), past_len)
    o = o.reshape(bs, NSA_HEADS, 2, HEAD_DIM)
    y_n = jnp.concatenate([o[:, :NSA_HEADS // 2, 0], o[:, NSA_HEADS // 2:, 1]], axis=1).reshape(bs, NSA_WIDTH)
    x = _out_proj(x, y_r, y_n, g1, lw['wo1_h'], lw['wo1_l'], lw['wo2_h'], lw['wo2_l'], tm, 1)
    x = _ffn(x, lw['norm_ffn'], sc2, sh2, g2, lw['wu_h'], lw['wu_l'], lw['wd_h'], lw['wd_l'], tm, 1)
    rows6 = lambda z: z.reshape(bs, 1, 2, 2, HEAD_DIM)
    return (x, s_new, p_r, rows6(cmp_rows), rows6(sel_rows), win_new.reshape(bs, -1, 2, 2, HEAD_DIM))


def kernel(x_prompt, x_sample, cache_cmp_kv, cache_sel_kv, cache_win_kv, state_rwkv, state_shift, page_table,
           c_prompt, c_sample, ada_w, ada_b, norm_mix, norm_ffn, w_in, rwkv_mu, rwkv_w0, rwkv_w_up, rwkv_a0,
           rwkv_a_up, rwkv_g_up, rwkv_k_k, rwkv_k_a, rwkv_r_k, rwkv_ln_w, rwkv_ln_b, qk_norm, cmp_pos, cmp_w1,
           cmp_b1, cmp_w2, cmp_b2, w_out, ffn_w_up, ffn_w_down):
    depth = ada_w.shape[0]
    bsz, t, d = x_prompt.shape
    bs = x_sample.shape[0]
    n_pool, page = cache_cmp_kv.shape[1], cache_cmp_kv.shape[2]
    n_pages = page_table.shape[1]
    past_len = n_pages * page
    wb = cache_win_kv.shape[2]
    bc = -(-(bsz + bs) // 8) * 8
    c_all = jnp.pad(jnp.concatenate([c_prompt, c_sample], axis=0), ((0, bc - bsz - bs), (0, 0)))
    mods = _ada_mods(c_all, ada_w, ada_b)
    tab_p = _rope_tables(jnp.arange(t, dtype=jnp.int32))
    tab_s = _rope_tables(jnp.full((bs,), past_len, jnp.int32))
    cache_cmp = cache_cmp_kv.reshape(depth * n_pool, page, 256)
    cache_sel = cache_sel_kv.reshape(depth * n_pool, page, 256)
    cache_win = cache_win_kv.reshape(depth * bs, wb, 256)
    st_rwkv = state_rwkv.reshape((depth * bs,) + state_rwkv.shape[2:])
    y_p = x_prompt
    y_s = x_sample.reshape(bs, d)
    outs_p = []
    outs_s = []
    for l in range(depth):
        lw = _layer_weights(l, norm_mix, norm_ffn, w_in, rwkv_mu, rwkv_w0, rwkv_w_up, rwkv_a0, rwkv_a_up,
                            rwkv_g_up, rwkv_k_k, rwkv_k_a, rwkv_r_k, rwkv_ln_w, rwkv_ln_b, qk_norm, cmp_pos,
                            cmp_w1, cmp_b1, cmp_w2, cmp_b2, w_out, ffn_w_up, ffn_w_down)
        res_p = _prompt_layer(y_p, mods[l, :bsz], lw, tab_p)
        y_p = res_p[0]
        outs_p.append(res_p[1:])
        res_s = _sample_layer(y_s, mods[l, bsz:bsz + bs], lw, tab_s, l, st_rwkv, state_shift, page_table,
                              cache_cmp, cache_sel, cache_win, n_pool, past_len)
        y_s = res_s[0]
        outs_s.append(res_s[1:])
    stack = lambda outs, i: jnp.stack([o[i] for o in outs])
    return (y_p, y_s.reshape(bs, 1, d),
            stack(outs_p, 2), stack(outs_s, 2),
            stack(outs_p, 3), stack(outs_s, 3),
            stack(outs_p, 4), stack(outs_s, 4),
            stack(outs_p, 0), stack(outs_s, 0),
            stack(outs_p, 1), stack(outs_s, 1))
```

```python
import functools

import numpy as np
import jax
import jax.numpy as jnp
from jax import lax
from jax.experimental import pallas as pl
from jax.experimental.pallas import tpu as pltpu

F32 = jnp.float32
BF16 = jnp.bfloat16

HEAD_DIM = 64
RWKV_WIDTH = 512
NSA_WIDTH = 512
RWKV_HEADS = 8
NSA_HEADS = 8
KV_WIDTH = 128
SHIFT_COLS = 1792
NSA_COLS = 1304
NSA_COLS_PAD = 1408
CMP_BLOCK = 32
CMP_STRIDE = 16
CMP_HIDDEN = 128
SEL_BLOCK = 64
SEL_TOP_N = 16
WINDOW = 512
ROPE_THETA = 10000.0
NORM_EPS = 1e-6
GN_EPS = 64e-5
NEG_INF = -1e30
FORCE = 1e9
LANES = 128
RWKV_CHUNK = 64
Q_TILE = 128
VMEM_LIMIT = 56 * 1024 * 1024


def _cparams(sem):
    return pltpu.CompilerParams(dimension_semantics=sem, vmem_limit_bytes=VMEM_LIMIT)


def _split2(x):
    hi = x.astype(BF16)
    lo = (x - hi.astype(F32)).astype(BF16)
    return hi, lo


def _split3(x):
    hi = x.astype(BF16)
    r = x - hi.astype(F32)
    mid = r.astype(BF16)
    lo = (r - mid.astype(F32)).astype(BF16)
    return hi, mid, lo


def _dg(a, b, ca=1, cb=0):
    return lax.dot_general(a, b, (((ca,), (cb,)), ((), ())), preferred_element_type=F32)


def _mm(a, b, ca=1, cb=0):
    ah, al = _split2(a)
    bh, bl = _split2(b)
    return _dg(ah, bh, ca, cb) + _dg(ah, bl, ca, cb) + _dg(al, bh, ca, cb)


def _mm_w(ah, al, wh, wl):
    return _dg(ah, wh) + _dg(ah, wl) + _dg(al, wh)


def _mmp(a2, b2, ca=1, cb=0):
    return _dg(a2[0], b2[0], ca, cb) + _dg(a2[0], b2[1], ca, cb) + _dg(a2[1], b2[0], ca, cb)


def _mm_xr(a, e, ca=1, cb=0):
    a0, a1, a2 = _split3(a)
    return _dg(a0, e, ca, cb) + _dg(a1, e, ca, cb) + _dg(a2, e, ca, cb)


def _mm_xl(e, b, ca=1, cb=0):
    b0, b1, b2 = _split3(b)
    return _dg(e, b0, ca, cb) + _dg(e, b1, ca, cb) + _dg(e, b2, ca, cb)


def _iota(shape, dim):
    return lax.broadcasted_iota(jnp.int32, shape, dim)


def _head_blockdiag():
    return (_iota((LANES, LANES), 0) // HEAD_DIM == _iota((LANES, LANES), 1) // HEAD_DIM).astype(BF16)


def _segsum(x, bd):
    w = x.shape[1]
    outs = [_mm_xr(x[:, j * LANES:(j + 1) * LANES], bd) for j in range(w // LANES)]
    return outs[0] if len(outs) == 1 else jnp.concatenate(outs, axis=1)


def _sigmoid(x):
    return 1.0 / (1.0 + jnp.exp(-x))


def _norm_mod(x, g, sc, sh):
    ms = jnp.mean(x * x, axis=-1, keepdims=True)
    y = x * lax.rsqrt(ms + NORM_EPS) * g
    return y * (1.0 + sc) + sh


def _topk_mask(score, n, axis):
    size = score.shape[axis]
    pos = _iota(score.shape, axis).astype(F32)
    sel = jnp.zeros_like(score)
    sc = score
    for _ in range(n):
        m = jnp.max(sc, axis=axis, keepdims=True)
        idx = jnp.min(jnp.where(sc == m, pos, float(size)), axis=axis, keepdims=True)
        hit = pos == idx
        sel = jnp.where(hit & (m > 0.5 * NEG_INF), 1.0, sel)
        sc = jnp.where(hit, -3e38, sc)
    return sel


def _ada_kernel(c_ref, w_ref, b_ref, o_ref):
    c = c_ref[...]
    o_ref[0] = _mm(c * _sigmoid(c), w_ref[0]) + b_ref[0]


def _ada_mods(c_all, ada_w, ada_b):
    depth, d, n = ada_w.shape
    bc = c_all.shape[0]
    tn = 1024
    return pl.pallas_call(
        _ada_kernel,
        out_shape=jax.ShapeDtypeStruct((depth, bc, n), F32),
        grid=(depth, n // tn),
        in_specs=[pl.BlockSpec((bc, d), lambda l, j: (0, 0)),
                  pl.BlockSpec((1, d, tn), lambda l, j: (l, 0, j)),
                  pl.BlockSpec((1, 1, tn), lambda l, j: (l, 0, j))],
        out_specs=pl.BlockSpec((1, bc, tn), lambda l, j: (l, 0, j)),
        compiler_params=_cparams(("arbitrary", "arbitrary")),
        name="ada_mods",
    )(c_all, ada_w, ada_b.reshape(depth, 1, n))


def _in_proj_kernel(x_ref, g_ref, sc_ref, sh_ref, *refs):
    w_refs, o_ref = refs[:-1], refs[-1]
    h = _norm_mod(x_ref[...], g_ref[...], sc_ref[0], sh_ref[0])
    if len(w_refs) == 1:
        o_ref[...] = _dg(h.astype(BF16), w_refs[0][...])
    else:
        hh, hl = _split2(h)
        o_ref[...] = _mm_w(hh, hl, w_refs[0][...], w_refs[1][...])


def _in_proj(x, g, sc, sh, ws, tm, tpb):
    m, d = x.shape
    n = ws[0].shape[1]
    r = sc.shape[1]
    return pl.pallas_call(
        _in_proj_kernel,
        out_shape=jax.ShapeDtypeStruct((m, n), F32),
        grid=(m // tm,),
        in_specs=[pl.BlockSpec((tm, d), lambda i: (i, 0)),
                  pl.BlockSpec((1, d), lambda i: (0, 0)),
                  pl.BlockSpec((1, r, d), lambda i: (i // tpb, 0, 0)),
                  pl.BlockSpec((1, r, d), lambda i: (i // tpb, 0, 0))]
                 + [pl.BlockSpec((d, n), lambda i: (0, 0)) for _ in ws],
        out_specs=pl.BlockSpec((tm, n), lambda i: (i, 0)),
        compiler_params=_cparams(("arbitrary",)),
        name="in_proj",
    )(x, g, sc, sh, *ws)


def _out_proj_kernel(x_ref, a1_ref, a2_ref, gate_ref, w1, w2, o_ref):
    acc = _dg(a1_ref[...].astype(BF16), w1[...]) + _dg(a2_ref[...].astype(BF16), w2[...])
    o_ref[...] = x_ref[...] + gate_ref[0] * acc


def _out_proj(x, a1, a2, gate, w1, w2, tm, tpb):
    m, d = x.shape
    k1 = a1.shape[1]
    k2 = a2.shape[1]
    r = gate.shape[1]
    return pl.pallas_call(
        _out_proj_kernel,
        out_shape=jax.ShapeDtypeStruct((m, d), F32),
        grid=(m // tm,),
        in_specs=[pl.BlockSpec((tm, d), lambda i: (i, 0)),
                  pl.BlockSpec((tm, k1), lambda i: (i, 0)),
                  pl.BlockSpec((tm, k2), lambda i: (i, 0)),
                  pl.BlockSpec((1, r, d), lambda i: (i // tpb, 0, 0)),
                  pl.BlockSpec((k1, d), lambda i: (0, 0)),
                  pl.BlockSpec((k2, d), lambda i: (0, 0))],
        out_specs=pl.BlockSpec((tm, d), lambda i: (i, 0)),
        compiler_params=_cparams(("arbitrary",)),
        name="out_proj",
    )(x, a1, a2, gate, w1, w2)


def _ffn_kernel(x_ref, g_ref, sc_ref, sh_ref, gate_ref, wg, wu, wd, o_ref, h_s, acc_s):
    j = pl.program_id(1)

    @pl.when(j == 0)
    def _():
        h_s[...] = _norm_mod(x_ref[...], g_ref[...], sc_ref[0], sh_ref[0]).astype(BF16)
        acc_s[...] = jnp.zeros_like(acc_s)

    h = h_s[...]
    gate = _dg(h, wg[...])
    up = _dg(h, wu[...])
    act = gate * _sigmoid(gate) * up
    acc_s[...] += _dg(act.astype(BF16), wd[...])

    @pl.when(j == pl.num_programs(1) - 1)
    def _():
        o_ref[...] = x_ref[...] + gate_ref[0] * acc_s[...]


def _ffn(x, g, sc, sh, gate, w_up, w_down, tm, tpb):
    m, d = x.shape
    dff = w_down.shape[0]
    tf = 256
    nf = dff // tf
    r = sc.shape[1]
    mod = pl.BlockSpec((1, r, d), lambda i, j: (i // tpb, 0, 0))
    return pl.pallas_call(
        _ffn_kernel,
        out_shape=jax.ShapeDtypeStruct((m, d), F32),
        grid=(m // tm, nf),
        in_specs=[pl.BlockSpec((tm, d), lambda i, j: (i, 0)),
                  pl.BlockSpec((1, d), lambda i, j: (0, 0)),
                  mod, mod, mod,
                  pl.BlockSpec((d, tf), lambda i, j: (0, j)),
                  pl.BlockSpec((d, tf), lambda i, j: (0, j + nf)),
                  pl.BlockSpec((tf, d), lambda i, j: (j, 0))],
        out_specs=pl.BlockSpec((tm, d), lambda i, j: (i, 0)),
        scratch_shapes=[pltpu.VMEM((tm, d), BF16), pltpu.VMEM((tm, d), F32)],
        compiler_params=_cparams(("arbitrary", "arbitrary")),
        name="ffn",
    )(x, g, sc, sh, gate, w_up, w_up, w_down)


def _rwkv_prep_kernel(p_ref, prev_ref, mu_ref, wch, wcl, guh, gul, w0_ref, a0_ref, kk_ref, ka_ref,
                      r_o, ld_o, k_o, v_o, a_o, b_o, g_o, *scratch, seq_mode, tpb):
    p = p_ref[...]
    tm = p.shape[0]
    if seq_mode:
        carry = scratch[0]
        i = pl.program_id(0)

        @pl.when(i % tpb == 0)
        def _():
            carry[...] = prev_ref[0]

        rolled = pltpu.roll(p, 1, 0)
        prev = jnp.where(_iota((tm, 1), 0) == 0, carry[...], rolled)
        carry[...] = p[tm - 1:tm, :]
    else:
        prev = prev_ref[...]
    xs = p + mu_ref[...] * (prev - p)
    w_ = RWKV_WIDTH
    r = xs[:, 0:w_]
    k = xs[:, w_:2 * w_]
    v = xs[:, 2 * w_:3 * w_]
    z = xs[:, 3 * w_:3 * w_ + LANES]
    gd = xs[:, 3 * w_ + LANES:3 * w_ + 2 * LANES]
    z = jnp.where(_iota((tm, LANES), 1) < 64, jnp.tanh(z), z)
    zh, zl = _split2(z)
    wa = _mm_w(zh, zl, wch[...], wcl[...])
    zw = -(w0_ref[...] + wa[:, 0:w_])
    softplus = jnp.maximum(zw, 0.0) + jnp.log1p(jnp.exp(-jnp.abs(zw)))
    wlog = -softplus - 0.5
    a_sig = _sigmoid(a0_ref[...] + wa[:, w_:2 * w_])
    sh_, sl_ = _split2(_sigmoid(gd))
    g = _mm_w(sh_, sl_, guh[...], gul[...])
    bd = _head_blockdiag()
    kkv = k * kk_ref[...]
    nrm = jnp.sqrt(_segsum(kkv * kkv, bd))
    kkn = kkv / jnp.maximum(nrm, 1e-12)
    r_o[...] = r
    ld_o[...] = -jnp.exp(wlog)
    k_o[...] = k * (1.0 + (a_sig - 1.0) * ka_ref[...])
    v_o[...] = v
    a_o[...] = -kkn
    b_o[...] = kkn * a_sig
    g_o[...] = g


def _rwkv_prep(p_r, prev, lw, tm, tpb, seq_mode):
    m = p_r.shape[0]
    w_ = RWKV_WIDTH
    row = lambda n: pl.BlockSpec((1, n), lambda i: (0, 0))
    full = lambda a: pl.BlockSpec(a.shape, lambda i: (0, 0))
    if seq_mode:
        prev_spec = pl.BlockSpec((1, 1, SHIFT_COLS), lambda i: (i // tpb, 0, 0))
        scratch = [pltpu.VMEM((1, SHIFT_COLS), F32)]
    else:
        prev_spec = pl.BlockSpec((tm, SHIFT_COLS), lambda i: (i, 0))
        scratch = []
    out = jax.ShapeDtypeStruct((m, w_), F32)
    ospec = pl.BlockSpec((tm, w_), lambda i: (i, 0))
    return pl.pallas_call(
        functools.partial(_rwkv_prep_kernel, seq_mode=seq_mode, tpb=tpb),
        out_shape=[out] * 7,
        grid=(m // tm,),
        in_specs=[pl.BlockSpec((tm, SHIFT_COLS), lambda i: (i, 0)), prev_spec, row(SHIFT_COLS),
                  full(lw['wc_h']), full(lw['wc_l']), full(lw['gu_h']), full(lw['gu_l']),
                  row(w_), row(w_), row(w_), row(w_)],
        out_specs=[ospec] * 7,
        scratch_shapes=scratch,
        compiler_params=_cparams(("arbitrary",)),
        name="rwkv_prep",
    )(p_r, prev, lw['mu'], lw['wc_h'], lw['wc_l'], lw['gu_h'], lw['gu_l'], lw['w0'], lw['a0'], lw['k_k'], lw['k_a'])


def _rwkv_chunk_kernel(r_ref, ld_ref, k_ref, v_ref, a_ref, b_ref, q1_o, y0_o, g_o, ha_o):
    c = RWKV_CHUNK
    c2 = 2 * c
    ld = ld_ref[0]
    lc = _mm_xl((_iota((c, c), 0) >= _iota((c, c), 1)).astype(BF16), ld)
    lx = lc - ld
    lend = lc[c - 1:c, :]
    p_in = jnp.exp(lc)
    inv_p = jnp.exp(-lc)
    p_to_end = jnp.exp(lend - lc)
    p_end = jnp.exp(lend)
    r = r_ref[0]
    k = k_ref[0]
    v = v_ref[0]
    b = b_ref[0]
    rt = r * p_in
    at = a_ref[0] * jnp.exp(lx)
    bt = b * inv_p
    kt = k * inv_p
    bh = b * p_to_end
    kh = k * p_to_end
    row = _iota((c2, c2), 0)
    col = _iota((c2, c2), 1)
    strict = row % c > col % c
    incl = row % c >= col % c
    eye = row == col
    lo = _iota((c, LANES), 1) < HEAD_DIM

    def pair_rows(x):
        return jnp.concatenate([jnp.where(lo, x, 0.0), jnp.where(lo, 0.0, x)], axis=0)

    pairs = range(RWKV_WIDTH // LANES)
    sls = [slice(LANES * j, LANES * (j + 1)) for j in pairs]
    at2 = [pair_rows(at[:, sl]) for sl in sls]
    rt2 = [pair_rows(rt[:, sl]) for sl in sls]
    v2s = [_split2(pair_rows(v[:, sl])) for sl in sls]
    lhs = [_split2(jnp.concatenate([at2[j], rt2[j]], axis=0)) for j in pairs]
    xb = [_mmp(lhs[j], _split2(pair_rows(bt[:, sls[j]])), 1, 1) for j in pairs]
    xk = [_mmp(lhs[j], _split2(pair_rows(kt[:, sls[j]])), 1, 1) for j in pairs]
    a_ab = [jnp.where(strict, xb[j][0:c2], 0.0) for j in pairs]
    a_rb = [_split2(jnp.where(incl, xb[j][c2:2 * c2], 0.0)) for j in pairs]
    a_ak = [_split2(jnp.where(strict, xk[j][0:c2], 0.0)) for j in pairs]
    a_rk = [_split2(jnp.where(incl, xk[j][c2:2 * c2], 0.0)) for j in pairs]
    akv = [_mmp(a_ak[j], v2s[j]) for j in pairs]
    x = [jnp.where(eye, 1.0, 0.0) + a_ab[j] for j in pairs]
    mp = a_ab
    for _ in range(int(np.log2(c)) - 1):
        mps = [_split2(mp[j]) for j in pairs]
        mp = [_mmp(mps[j], mps[j]) for j in pairs]
        x = [x[j] + _mmp(_split2(mp[j]), _split2(x[j])) for j in pairs]
    wu = [_split2(_mmp(_split2(x[j]), _split2(jnp.concatenate([at2[j], akv[j]], axis=1)))) for j in pairs]
    qy = [_mmp(a_rb[j], wu[j]) for j in pairs]
    ark_v = [_mmp(a_rk[j], v2s[j]) for j in pairs]
    gh = [_mmp(_split2(pair_rows(bh[:, sls[j]])), wu[j], 0, 0) for j in pairs]
    khv = [_mmp(_split2(pair_rows(kh[:, sls[j]])), v2s[j], 0, 0) for j in pairs]
    for j in pairs:
        sl = sls[j]
        q1 = rt2[j] + qy[j][:, 0:LANES]
        y0 = qy[j][:, LANES:2 * LANES] + ark_v[j]
        q1_o[0, :, sl] = q1[0:c] + q1[c:c2]
        y0_o[0, :, sl] = y0[0:c] + y0[c:c2]
        g_o[0, 0, :, sl] = jnp.where(eye, p_end[:, sl], 0.0) + gh[j][:, 0:LANES]
        ha_o[0, 0, :, sl] = gh[j][:, LANES:2 * LANES] + khv[j]


def _rwkv_chunk(r, ld, k, v, a, b):
    bsz, t, w_ = r.shape
    c = RWKV_CHUNK
    nch = t // c
    ispec = pl.BlockSpec((1, c, w_), lambda i, j: (i, j, 0))
    sspec = pl.BlockSpec((1, 1, 2 * HEAD_DIM, w_), lambda i, j: (i, j, 0, 0))
    return pl.pallas_call(
        _rwkv_chunk_kernel,
        out_shape=[jax.ShapeDtypeStruct((bsz, t, w_), F32)] * 2
                  + [jax.ShapeDtypeStruct((bsz, nch, 2 * HEAD_DIM, w_), F32)] * 2,
        grid=(bsz, nch),
        in_specs=[ispec] * 6,
        out_specs=[ispec, ispec, sspec, sspec],
        compiler_params=_cparams(("arbitrary", "arbitrary")),
        name="rwkv_chunk",
    )(r, ld, k, v, a, b)


def _rwkv_scan_kernel(q1_ref, y0_ref, g_ref, ha_ref, y_o, hf_o, h_s):
    j = pl.program_id(0)

    @pl.when(j == 0)
    def _():
        h_s[...] = jnp.zeros_like(h_s)

    src = j % 2
    dst = (j + 1) % 2
    for bi in range(q1_ref.shape[0]):
        for p in range(RWKV_WIDTH // LANES):
            sl = slice(LANES * p, LANES * (p + 1))
            h0 = _split2(h_s[src, bi, :, sl])
            y_o[bi, :, sl] = _mmp(_split2(q1_ref[bi, :, sl]), h0) + y0_ref[bi, :, sl]
            h_s[dst, bi, :, sl] = _mmp(_split2(g_ref[bi, 0, :, sl]), h0) + ha_ref[bi, 0, :, sl]

    @pl.when(j == pl.num_programs(0) - 1)
    def _():
        hf_o[...] = h_s[dst, :, 0:HEAD_DIM, :] + h_s[dst, :, HEAD_DIM:2 * HEAD_DIM, :]


def _rwkv_scan(q1, y0, g, ha):
    bsz, t, w_ = q1.shape
    c = RWKV_CHUNK
    ispec = pl.BlockSpec((bsz, c, w_), lambda j: (0, j, 0))
    sspec = pl.BlockSpec((bsz, 1, 2 * HEAD_DIM, w_), lambda j: (0, j, 0, 0))
    return pl.pallas_call(
        _rwkv_scan_kernel,
        out_shape=[jax.ShapeDtypeStruct((bsz, t, w_), F32), jax.ShapeDtypeStruct((bsz, HEAD_DIM, w_), F32)],
        grid=(t // c,),
        in_specs=[ispec, ispec, sspec, sspec],
        out_specs=[ispec, pl.BlockSpec((bsz, HEAD_DIM, w_), lambda j: (0, 0, 0))],
        scratch_shapes=[pltpu.VMEM((2, bsz, 2 * HEAD_DIM, w_), F32)],
        compiler_params=_cparams(("arbitrary",)),
        name="rwkv_scan",
    )(q1, y0, g, ha)


def _rwkv_step_kernel(s_ref, r_ref, ld_ref, k_ref, a_ref, b_ref, v_ref, so_ref, y_ref):
    s = s_ref[...]
    sa = jnp.sum(s * a_ref[...], axis=-1, keepdims=True)
    s_new = s * jnp.exp(ld_ref[...]) + sa * b_ref[...] + v_ref[...] * k_ref[...]
    so_ref[...] = s_new
    y_ref[...] = jnp.sum(s_new * r_ref[...], axis=-1, keepdims=True)


def _rwkv_step(state, layer, r, ld, k, a, b, v):
    bs = r.shape[0]
    nh, hd = RWKV_HEADS, HEAD_DIM
    bt = min(8, bs)
    rowv = lambda x: x.reshape(bs, nh, 1, hd)
    vspec = pl.BlockSpec((bt, nh, 1, hd), lambda i: (i, 0, 0, 0))
    cspec = pl.BlockSpec((bt, nh, hd, 1), lambda i: (i, 0, 0, 0))
    off = layer * (bs // bt)
    return pl.pallas_call(
        _rwkv_step_kernel,
        out_shape=[jax.ShapeDtypeStruct((bs, nh, hd, hd), F32), jax.ShapeDtypeStruct((bs, nh, hd, 1), F32)],
        grid=(bs // bt,),
        in_specs=[pl.BlockSpec((bt, nh, hd, hd), lambda i: (i + off, 0, 0, 0)),
                  vspec, vspec, vspec, vspec, vspec, cspec],
        out_specs=[pl.BlockSpec((bt, nh, hd, hd), lambda i: (i, 0, 0, 0)), cspec],
        compiler_params=_cparams(("arbitrary",)),
        name="rwkv_step",
    )(state, rowv(r), rowv(ld), rowv(k), rowv(a), rowv(b), v.reshape(bs, nh, hd, 1))


def _rwkv_post_kernel(y_ref, r_ref, k_ref, v_ref, g_ref, rk_ref, lw_ref, lb_ref, o_ref):
    bd = _head_blockdiag()
    y = y_ref[...]
    inv = 1.0 / HEAD_DIM
    mu = _segsum(y, bd) * inv
    d = y - mu
    var = _segsum(d * d, bd) * inv
    yn = d * lax.rsqrt(var + GN_EPS) * lw_ref[...] + lb_ref[...]
    v = v_ref[...]
    bonus = _segsum(r_ref[...] * k_ref[...] * rk_ref[...], bd) * v
    o_ref[...] = (yn + bonus) * g_ref[...]


def _rwkv_post(y, r, k, v, g, lw, tm):
    m, w_ = y.shape
    tile = pl.BlockSpec((tm, w_), lambda i: (i, 0))
    row = pl.BlockSpec((1, w_), lambda i: (0, 0))
    return pl.pallas_call(
        _rwkv_post_kernel,
        out_shape=jax.ShapeDtypeStruct((m, w_), F32),
        grid=(m // tm,),
        in_specs=[tile] * 5 + [row] * 3,
        out_specs=tile,
        compiler_params=_cparams(("arbitrary",)),
        name="rwkv_post",
    )(y, r, k, v, g, lw['r_k'], lw['ln_w'], lw['ln_b'])


def _nsa_proj_kernel(p_ref, cos_ref, sin_ref, nw_ref, q_o, qr_o, cmp_o, sel_o, win_o, gt_o, selk_o, selv_o, winb_o):
    tm = p_ref.shape[0]
    cos = cos_ref[...]
    sin = sin_ref[...]
    bd = _head_blockdiag()
    first = (_iota((tm, LANES), 1) % HEAD_DIM) < HEAD_DIM // 2
    nw = nw_ref[...]

    def norm(x, w):
        ms = _segsum(x * x, bd) * (1.0 / HEAD_DIM)
        return x * lax.rsqrt(ms + NORM_EPS) * w

    def rope(x):
        rot = jnp.where(first, pltpu.roll(x, LANES - HEAD_DIM // 2, 1), pltpu.roll(x, HEAD_DIM // 2, 1))
        return x * cos + rot * sin

    for j in range(NSA_WIDTH // LANES):
        sl = slice(LANES * j, LANES * (j + 1))
        xn = norm(p_ref[:, sl], nw[0:1])
        q_o[:, sl] = xn
        qr_o[:, sl] = rope(xn)
    o = NSA_WIDTH
    cmp_o[...] = p_ref[:, o:o + 2 * KV_WIDTH]
    ks = rope(norm(p_ref[:, o + 256:o + 384], nw[2:3]))
    vs = p_ref[:, o + 384:o + 512]
    kw = rope(norm(p_ref[:, o + 512:o + 640], nw[3:4]))
    vw = p_ref[:, o + 640:o + 768]
    sel_o[:, 0:LANES] = ks
    sel_o[:, LANES:2 * LANES] = vs
    win_o[:, 0:LANES] = kw
    win_o[:, LANES:2 * LANES] = vw
    gt_o[...] = _sigmoid(p_ref[:, o + 768:o + 896])
    lo = _iota((tm, LANES), 1) < HEAD_DIM
    selk_o[...] = ks.astype(BF16)
    selv_o[:, 0:LANES] = jnp.where(lo, vs, 1.0).astype(BF16)
    selv_o[:, LANES:2 * LANES] = jnp.where(lo, 1.0, vs).astype(BF16)
    winb_o[:, 0:LANES] = kw.astype(BF16)
    winb_o[:, LANES:2 * LANES] = vw.astype(BF16)


def _nsa_proj(p_n, cos_t, sin_t, nw, tm, tpb):
    m = p_n.shape[0]
    tt = cos_t.shape[0] // tm
    tab = pl.BlockSpec((tm, LANES), lambda i: (i % tt, 0))
    shapes = [(m, NSA_WIDTH), (m, NSA_WIDTH), (m, 256), (m, 256), (m, 256), (m, LANES),
              (m, LANES), (m, 256), (m, 256)]
    dtypes = [F32] * 6 + [BF16] * 3
    return pl.pallas_call(
        _nsa_proj_kernel,
        out_shape=[jax.ShapeDtypeStruct(s, dt) for s, dt in zip(shapes, dtypes)],
        grid=(m // tm,),
        in_specs=[pl.BlockSpec((tm, NSA_COLS_PAD), lambda i: (i, 0)), tab, tab,
                  pl.BlockSpec((4, LANES), lambda i: (0, 0))],
        out_specs=[pl.BlockSpec((tm, s[1]), lambda i: (i, 0)) for s in shapes],
        compiler_params=_cparams(("arbitrary",)),
        name="nsa_proj",
    )(p_n, cos_t, sin_t, nw)


def _cmp1_kernel(*refs, n_parts, paged):
    if paged:
        refs = refs[1:]
    parts = refs[:n_parts]
    wh_ref, wl_ref, o_ref, xs = refs[n_parts:]
    gpp = parts[0].shape[1] // CMP_STRIDE
    for j, pr in enumerate(parts):
        for l in range(CMP_STRIDE):
            xs[gpp * j:gpp * (j + 1), LANES * l:LANES * (l + 1)] = pr[0, pl.ds(l, gpp, stride=CMP_STRIDE), :]
    xh, xl = _split2(xs[...])
    res = _mm_w(xh, xl, wh_ref[0], wl_ref[0])
    ng = o_ref.shape[2]
    for s in range(o_ref.shape[0]):
        o_ref[s, 0] = res[ng * s:ng * (s + 1)]


def _cmp1_prompt(cmp_rows, wh, wl):
    bsz, t, _ = cmp_rows.shape
    rows = min(2048, t)
    ng = rows // CMP_STRIDE
    kdim = CMP_STRIDE * LANES
    return pl.pallas_call(
        functools.partial(_cmp1_kernel, n_parts=1, paged=False),
        out_shape=jax.ShapeDtypeStruct((bsz, 2, t // CMP_STRIDE, 512), F32),
        grid=(bsz, t // rows, 2),
        in_specs=[pl.BlockSpec((1, rows, LANES), lambda b, i, kv: (b, i, kv)),
                  pl.BlockSpec((1, kdim, 512), lambda b, i, kv: (kv, 0, 0)),
                  pl.BlockSpec((1, kdim, 512), lambda b, i, kv: (kv, 0, 0))],
        out_specs=pl.BlockSpec((1, 1, ng, 512), lambda b, i, kv: (b, kv, i, 0)),
        scratch_shapes=[pltpu.VMEM((ng, kdim), F32)],
        compiler_params=_cparams(("arbitrary", "arbitrary", "arbitrary")),
        name="cmp1_prompt",
    )(cmp_rows, wh, wl)


def _cmp1_sample(page_table, cache, page_off, wh, wl):
    bs, n_pages = page_table.shape
    page = cache.shape[1]
    ng = n_pages * page // CMP_STRIDE
    kdim = CMP_STRIDE * LANES
    sps = 4 if bs % 4 == 0 else 1

    def page_spec(s, j):
        return pl.BlockSpec((1, page, LANES), lambda b, kv, pt: (pt[b * sps + s, j] + page_off, 0, kv))

    return pl.pallas_call(
        functools.partial(_cmp1_kernel, n_parts=sps * n_pages, paged=True),
        out_shape=jax.ShapeDtypeStruct((bs, 2, ng, 512), F32),
        grid_spec=pltpu.PrefetchScalarGridSpec(
            num_scalar_prefetch=1,
            grid=(bs // sps, 2),
            in_specs=[page_spec(s, j) for s in range(sps) for j in range(n_pages)] + [
                pl.BlockSpec((1, kdim, 512), lambda b, kv, pt: (kv, 0, 0)),
                pl.BlockSpec((1, kdim, 512), lambda b, kv, pt: (kv, 0, 0))],
            out_specs=pl.BlockSpec((sps, 1, ng, 512), lambda b, kv, pt: (b, kv, 0, 0)),
            scratch_shapes=[pltpu.VMEM((sps * ng, kdim), F32)]),
        compiler_params=_cparams(("arbitrary", "arbitrary")),
        name="cmp1_sample",
    )(page_table, *([cache] * (sps * n_pages)), wh, wl)


def _cmp_bias_kernel(pos_ref, w1_ref, b1_ref, o_ref):
    posb = jnp.broadcast_to(pos_ref[0], (8, pos_ref.shape[2]))
    o_ref[0] = _mm(posb, w1_ref[0])[0:1] + b1_ref[0]


def _cmp_bias(lw):
    kdim = CMP_BLOCK * HEAD_DIM
    return pl.pallas_call(
        _cmp_bias_kernel,
        out_shape=jax.ShapeDtypeStruct((2, 1, CMP_HIDDEN), F32),
        grid=(2,),
        in_specs=[pl.BlockSpec((1, 1, kdim), lambda kv: (kv, 0, 0)),
                  pl.BlockSpec((1, kdim, CMP_HIDDEN), lambda kv: (kv, 0, 0)),
                  pl.BlockSpec((1, 1, CMP_HIDDEN), lambda kv: (kv, 0, 0))],
        out_specs=pl.BlockSpec((1, 1, CMP_HIDDEN), lambda kv: (kv, 0, 0)),
        compiler_params=_cparams(("arbitrary",)),
        name="cmp_bias",
    )(lw['cmp_pos'], lw['cmp_w1'], lw['cmp_b1'])


def _cmp2_kernel(ab_ref, c1_ref, w2_ref, b2_ref, nw_ref, o_ref):
    kv = pl.program_id(1)
    ab = ab_ref[0, 0]
    ng = ab.shape[0]
    a_part = ab[:, 0:256]
    b_next = pltpu.roll(ab[:, 256:512], ng - 1, 0)
    c1 = c1_ref[0]
    h = a_part + b_next + jnp.concatenate([c1, c1], axis=1)
    h = 0.5 * h * (1.0 + jnp.tanh(np.sqrt(2.0 / np.pi) * (h + 0.044715 * (h * h * h))))
    out = _mm(h, w2_ref[0]) + b2_ref[0]
    ms = _segsum(out * out, _head_blockdiag()) * (1.0 / HEAD_DIM)
    normed = out * lax.rsqrt(ms + NORM_EPS) * nw_ref[...]
    o_ref[0, 0] = jnp.where(kv == 0, normed, out)


def _cmp2(ab, c1, lw):
    bsz, _, ng, _ = ab.shape
    return pl.pallas_call(
        _cmp2_kernel,
        out_shape=jax.ShapeDtypeStruct((bsz, 2, ng, LANES), F32),
        grid=(bsz, 2),
        in_specs=[pl.BlockSpec((1, 1, ng, 512), lambda b, kv: (b, kv, 0, 0)),
                  pl.BlockSpec((1, 1, CMP_HIDDEN), lambda b, kv: (kv, 0, 0)),
                  pl.BlockSpec((1, 2 * CMP_HIDDEN, LANES), lambda b, kv: (kv, 0, 0)),
                  pl.BlockSpec((1, 1, LANES), lambda b, kv: (kv, 0, 0)),
                  pl.BlockSpec((1, LANES), lambda b, kv: (0, 0))],
        out_specs=pl.BlockSpec((1, 1, ng, LANES), lambda b, kv: (b, kv, 0, 0)),
        compiler_params=_cparams(("arbitrary", "arbitrary")),
        name="cmp2",
    )(ab, c1, lw['cmp_w2bd'], lw['cmp_b2t'], lw['nw_kc'])


def _masked_softmax(s, mask):
    s = jnp.where(mask, s, NEG_INF)
    mx = jnp.max(s, axis=-1, keepdims=True)
    e = jnp.where(mask, jnp.exp(s - mx), 0.0)
    den = jnp.sum(e, axis=-1, keepdims=True)
    return e / jnp.maximum(den, 1e-30)


def _selection_mask(imp, qpos, axis):
    blk = _iota(imp.shape, axis)
    cur = qpos // SEL_BLOCK
    valid = blk * SEL_BLOCK <= qpos
    forced = (blk == 0) | (blk == cur) | (blk == cur - 1)
    score = jnp.where(valid, jnp.where(forced, FORCE, imp), NEG_INF)
    return _topk_mask(score, SEL_TOP_N, axis)


def _attn_prompt_kernel(q_ref, qr_ref, gt_ref, kc_ref, vc_ref, ks_ref, vs0_ref, vs1_ref, *rest, ck):
    nwb = WINDOW // Q_TILE + 1
    kw_refs = rest[0:nwb]
    vw_refs = rest[nwb:2 * nwb]
    ovt_ref, en_ref, o_ref, qg_s, qrg_s, m_s, acc_s = rest[2 * nwb:]
    qi = pl.program_id(1)
    tq = Q_TILE
    hpg = NSA_HEADS // 2
    scale = HEAD_DIM ** -0.5
    ng = kc_ref.shape[2]
    lo_lanes = _iota((tq, LANES), 1) < HEAD_DIM
    qpos = qi * tq + _iota((tq, 1), 0)
    qpos_t = qi * tq + _iota((1, tq), 1)
    qpos4 = jnp.concatenate([qpos] * hpg, axis=0)
    gt = gt_ref[0]
    kc = kc_ref[0, 0]
    vc = vc_ref[0, 0]
    kw = jnp.concatenate([r[0] for r in kw_refs], axis=0)
    vw = jnp.concatenate([r[0] for r in vw_refs], axis=0)
    kwpos = jnp.concatenate([(qi - (nwb - 1) + j) * tq + _iota((1, tq), 1) for j in range(nwb)], axis=1)
    nch = (qi * tq + tq + ck - 1) // ck
    for g in range(2):
        keep = lo_lanes if g == 0 else jnp.logical_not(lo_lanes)
        for hh in range(hpg):
            h = hpg * g + hh
            cb = h // 2
            rs = slice(tq * hh, tq * (hh + 1))
            blk = q_ref[0, :, LANES * cb:LANES * (cb + 1)]
            blkr = qr_ref[0, :, LANES * cb:LANES * (cb + 1)]
            if (h % 2 == 0) != (g == 0):
                blk = pltpu.roll(blk, HEAD_DIM, 1)
                blkr = pltpu.roll(blkr, HEAD_DIM, 1)
            qg_s[rs, :] = jnp.where(keep, blk, 0.0) * scale
            qrg_s[rs, :] = (jnp.where(keep, blkr, 0.0) * scale).astype(BF16)
        qg = qg_s[...]
        qrg = qrg_s[...]
        s = _mm(qg, kc, 1, 1)
        mc = (_iota((hpg * tq, ng), 1) * CMP_STRIDE + (CMP_BLOCK - 1)) <= qpos4
        p = _masked_softmax(s, mc)
        oc = _mm(p, vc)
        psum = p[0:tq] + p[tq:2 * tq] + p[2 * tq:3 * tq] + p[3 * tq:4 * tq]
        imp_t = _mm_xl(ovt_ref[...], psum.T)
        sel_t = _selection_mask(imp_t, qpos_t, 0)
        nselb = (1.0 - sel_t).T.astype(BF16)
        m_s[...] = jnp.full(m_s.shape, 0.1 * NEG_INF, F32)
        acc_s[...] = jnp.zeros(acc_s.shape, F32)
        vs_ref = vs0_ref if g == 0 else vs1_ref

        def chunk(c, diagonal):
            st = pl.multiple_of(c * ck, ck)
            kch = ks_ref[0, pl.ds(st, ck), :]
            vch = vs_ref[0, pl.ds(st, ck), :]
            sc = _dg(qrg, kch, 1, 1)
            bias = _dg(nselb, en_ref[c])
            if diagonal:
                bias = bias + jnp.where(st + _iota((tq, ck), 1) <= qpos, 0.0, NEG_INF)
            for hh in range(hpg):
                rs = slice(tq * hh, tq * (hh + 1))
                sh = sc[rs] + bias
                m_old = m_s[rs, :]
                m_new = jnp.maximum(m_old, jnp.max(sh, axis=-1, keepdims=True))
                alpha = jnp.exp(m_old - m_new)
                pe = jnp.exp(sh - m_new)
                acc_s[rs, :] = alpha * acc_s[rs, :] + _dg(pe.astype(BF16), vch)
                m_s[rs, :] = m_new

        def body(c, carry):
            chunk(c, False)
            return carry

        lax.fori_loop(0, nch - 1, body, 0)
        chunk(nch - 1, True)
        acc = acc_s[...]
        osel = acc / pltpu.roll(acc, HEAD_DIM, 1)
        sw = _dg(qrg, kw, 1, 1)
        mw = (kwpos <= qpos4) & (kwpos > qpos4 - WINDOW) & (kwpos >= 0)
        ow = _dg(_masked_softmax(sw, mw).astype(BF16), vw)
        for hh in range(hpg):
            h = hpg * g + hh
            rs = slice(tq * hh, tq * (hh + 1))
            og = (gt[:, 3 * h:3 * h + 1] * oc[rs] + gt[:, 3 * h + 1:3 * h + 2] * osel[rs]
                  + gt[:, 3 * h + 2:3 * h + 3] * ow[rs])
            o_ref[0, :, HEAD_DIM * h:HEAD_DIM * (h + 1)] = og[:, HEAD_DIM * g:HEAD_DIM * (g + 1)]


def _overlap_matrix(ng):
    s = np.arange(ng)[:, None] * CMP_STRIDE
    j = np.arange(LANES)[None, :] * SEL_BLOCK
    return ((s < j + SEL_BLOCK) & (s + CMP_BLOCK > j)).astype(np.float32)


def _block_bias_matrix(t, ck):
    key_blk = (np.arange(t) // SEL_BLOCK).reshape(t // ck, 1, ck)
    blk = np.arange(LANES).reshape(1, LANES, 1)
    return jnp.asarray(np.where(key_blk == blk, NEG_INF, 0.0).astype(np.float32), dtype=BF16)


def _attn_prompt(q, qr, gt, ckv, selk, selv, win):
    bsz, t, _ = q.shape
    tq = Q_TILE
    ng = ckv.shape[2]
    ck = min(1024, t)
    nwb = WINDOW // tq + 1
    hpg = NSA_HEADS // 2
    assert t // SEL_BLOCK <= LANES and t % ck == 0
    ovt = jnp.asarray(_overlap_matrix(ng).T, dtype=BF16)

    def wspec(j, col):
        return pl.BlockSpec((1, tq, LANES), lambda b, i: (b, jnp.maximum(i - (nwb - 1) + j, 0), col))

    tile = lambda w: pl.BlockSpec((1, tq, w), lambda b, i: (b, i, 0))
    return pl.pallas_call(
        functools.partial(_attn_prompt_kernel, ck=ck),
        out_shape=jax.ShapeDtypeStruct((bsz, t, NSA_WIDTH), F32),
        grid=(bsz, t // tq),
        in_specs=[tile(NSA_WIDTH), tile(NSA_WIDTH), tile(LANES),
                  pl.BlockSpec((1, 1, ng, LANES), lambda b, i: (b, 0, 0, 0)),
                  pl.BlockSpec((1, 1, ng, LANES), lambda b, i: (b, 1, 0, 0)),
                  pl.BlockSpec((1, t, LANES), lambda b, i: (b, 0, 0)),
                  pl.BlockSpec((1, t, LANES), lambda b, i: (b, 0, 0)),
                  pl.BlockSpec((1, t, LANES), lambda b, i: (b, 0, 1))]
                 + [wspec(j, 0) for j in range(nwb)] + [wspec(j, 1) for j in range(nwb)]
                 + [pl.BlockSpec((LANES, ng), lambda b, i: (0, 0)),
                    pl.BlockSpec((t // ck, LANES, ck), lambda b, i: (0, 0, 0))],
        out_specs=tile(NSA_WIDTH),
        scratch_shapes=[pltpu.VMEM((hpg * tq, LANES), F32), pltpu.VMEM((hpg * tq, LANES), BF16),
                        pltpu.VMEM((hpg * tq, 1), F32), pltpu.VMEM((hpg * tq, LANES), F32)],
        compiler_params=_cparams(("arbitrary", "arbitrary")),
        name="attn_prompt",
    )(q, qr, gt, ckv, ckv, selk, selv, selv, *([win] * (2 * nwb)), ovt, _block_bias_matrix(t, ck))


def _attn_sample_kernel(pt_ref, q_ref, qr_ref, gt_ref, ckv_ref, *rest, past_len, n_pages):
    pages = rest[:n_pages]
    nsel_ref, win_ref, nwin_ref, ov_ref, o_ref, wout_ref = rest[n_pages:]
    nh = NSA_HEADS
    scale = HEAD_DIM ** -0.5
    qpos = past_len
    q = q_ref[0]
    qr = qr_ref[0]
    gt = gt_ref[0]
    kc = ckv_ref[0, 0]
    vc = ckv_ref[0, 1]
    ng = kc.shape[0]
    s = _mm(q, kc, 1, 1) * scale
    mc = (_iota((nh, ng), 1) * CMP_STRIDE + (CMP_BLOCK - 1)) <= qpos
    p = _masked_softmax(s, mc)
    oc = _mm(p, vc)
    same_group = (_iota((nh, nh), 0) // (nh // 2) == _iota((nh, nh), 1) // (nh // 2)).astype(BF16)
    imp = _mm_xr(_mm_xl(same_group, p), ov_ref[...])
    selb = _selection_mask(imp, qpos, 1).astype(BF16)
    page = pages[0].shape[1]
    nsel = nsel_ref[0]
    qrb = qr.astype(BF16)
    s_new = jnp.sum(qr * nsel[:, 0:LANES], axis=-1, keepdims=True) * scale
    scores = []
    oks = []
    m = s_new
    for j, pg in enumerate(pages):
        sj = _dg(qrb, pg[0, :, 0:LANES].astype(BF16), 1, 1) * scale
        expand = (_iota((LANES, page), 0) == (j * page + _iota((LANES, page), 1)) // SEL_BLOCK).astype(BF16)
        ok = _dg(selb, expand) > 0.5
        sj = jnp.where(ok, sj, NEG_INF)
        m = jnp.maximum(m, jnp.max(sj, axis=-1, keepdims=True))
        scores.append(sj)
        oks.append(ok)
    e_new = jnp.exp(s_new - m)
    den = e_new
    acc = e_new * nsel[:, LANES:2 * LANES]
    for j, pg in enumerate(pages):
        e = jnp.where(oks[j], jnp.exp(scores[j] - m), 0.0)
        den = den + jnp.sum(e, axis=-1, keepdims=True)
        acc = acc + _dg(e.astype(BF16), pg[0, :, LANES:2 * LANES].astype(BF16))
    osel = acc / den
    wb = win_ref.shape[1]
    wnd = win_ref[0]
    nwin = nwin_ref[0]
    sw = _dg(qrb, wnd[:, 0:LANES].astype(BF16), 1, 1) * scale
    kwpos = past_len - wb + _iota((1, wb), 1)
    mw = (kwpos <= qpos) & (kwpos > qpos - WINDOW) & (kwpos >= 0)
    sw = jnp.where(mw, sw, NEG_INF)
    sw_new = jnp.sum(qr * nwin[:, 0:LANES], axis=-1, keepdims=True) * scale
    mx = jnp.maximum(jnp.max(sw, axis=-1, keepdims=True), sw_new)
    ew = jnp.where(mw, jnp.exp(sw - mx), 0.0)
    ew_new = jnp.exp(sw_new - mx)
    ow = (_dg(ew.astype(BF16), wnd[:, LANES:2 * LANES].astype(BF16)) + ew_new * nwin[:, LANES:2 * LANES]) / (
        jnp.sum(ew, axis=-1, keepdims=True) + ew_new)
    o_ref[0] = gt[:, 0:1] * oc + gt[:, 1:2] * osel + gt[:, 2:3] * ow
    shifted = pltpu.roll(wnd, wb - 1, 0)
    wout_ref[0] = jnp.where(_iota((wb, 1), 0) == wb - 1, nwin, shifted)


def _attn_sample(page_table, q, qr, gt, ckv, cache_sel, page_off, nsel, cache_win, win_off, nwin, past_len):
    bs, n_pages = page_table.shape
    page = cache_sel.shape[1]
    wb = cache_win.shape[1]
    ng = ckv.shape[2]
    nh = NSA_HEADS
    assert wb == WINDOW and past_len // SEL_BLOCK + 1 <= LANES

    def page_spec(j):
        return pl.BlockSpec((1, page, 256), lambda b, pt: (pt[b, j] + page_off, 0, 0))

    vec = lambda w: pl.BlockSpec((1, nh, w), lambda b, pt: (b, 0, 0))
    rowspec = pl.BlockSpec((1, 1, 256), lambda b, pt: (b, 0, 0))
    return pl.pallas_call(
        functools.partial(_attn_sample_kernel, past_len=past_len, n_pages=n_pages),
        out_shape=[jax.ShapeDtypeStruct((bs, nh, LANES), F32), jax.ShapeDtypeStruct((bs, wb, 256), F32)],
        grid_spec=pltpu.PrefetchScalarGridSpec(
            num_scalar_prefetch=1,
            grid=(bs,),
            in_specs=[vec(LANES), vec(LANES), vec(LANES),
                      pl.BlockSpec((1, 2, ng, LANES), lambda b, pt: (b, 0, 0, 0))]
                     + [page_spec(j) for j in range(n_pages)]
                     + [rowspec,
                        pl.BlockSpec((1, wb, 256), lambda b, pt: (b + win_off, 0, 0)),
                        rowspec,
                        pl.BlockSpec((ng, LANES), lambda b, pt: (0, 0))],
            out_specs=[vec(LANES), pl.BlockSpec((1, wb, 256), lambda b, pt: (b, 0, 0))]),
        compiler_params=_cparams(("arbitrary",)),
        name="attn_sample",
    )(page_table, q, qr, gt, ckv, *([cache_sel] * n_pages), nsel, cache_win, nwin,
      jnp.asarray(_overlap_matrix(ng), dtype=BF16))


def _hilo(w):
    hi = w.astype(BF16)
    return hi, (w - hi.astype(F32)).astype(BF16)


def _tile_heads(v, reps):
    return jnp.tile(v.reshape(1, -1), (1, reps))


def _layer_weights(l, norm_mix, norm_ffn, w_in, rwkv_mu, rwkv_w0, rwkv_w_up, rwkv_a0, rwkv_a_up, rwkv_g_up,
                   rwkv_k_k, rwkv_k_a, rwkv_r_k, rwkv_ln_w, rwkv_ln_b, qk_norm, cmp_pos, cmp_w1, cmp_b1, cmp_w2,
                   cmp_b2, w_out, ffn_w_up, ffn_w_down):
    lw = {}
    d = w_in.shape[1]
    lw['norm_mix'] = norm_mix[l].reshape(1, d)
    lw['norm_ffn'] = norm_ffn[l].reshape(1, d)
    lw['wr'] = (w_in[l][:, :SHIFT_COLS].astype(BF16),)
    w_n = jnp.pad(w_in[l][:, SHIFT_COLS:], ((0, 0), (0, NSA_COLS_PAD - NSA_COLS)))
    lw['wn'] = _hilo(w_n)
    lw['mu'] = rwkv_mu[l].reshape(1, -1)
    lw['w0'] = rwkv_w0[l].reshape(1, -1)
    lw['a0'] = rwkv_a0[l].reshape(1, -1)
    lw['k_k'] = rwkv_k_k[l].reshape(1, -1)
    lw['k_a'] = rwkv_k_a[l].reshape(1, -1)
    lw['r_k'] = rwkv_r_k[l].reshape(1, -1)
    lw['ln_w'] = rwkv_ln_w[l].reshape(1, -1)
    lw['ln_b'] = rwkv_ln_b[l].reshape(1, -1)
    zeros = jnp.zeros_like(rwkv_w_up[l])
    wc = jnp.concatenate([jnp.concatenate([rwkv_w_up[l], zeros], axis=1),
                          jnp.concatenate([zeros, rwkv_a_up[l]], axis=1)], axis=0)
    lw['wc_h'], lw['wc_l'] = _hilo(wc)
    lw['gu_h'], lw['gu_l'] = _hilo(rwkv_g_up[l])
    qn = qk_norm[l]
    lw['nw'] = jnp.tile(qn, (1, 2))
    lw['nw_kc'] = jnp.tile(qn[1:2], (1, 2))
    w1 = cmp_w1[l].reshape(2, 2, CMP_STRIDE, HEAD_DIM, CMP_HIDDEN)
    eye = jnp.eye(2, dtype=F32)
    w1g = jnp.einsum('khldj,gf->klgdhfj', w1, eye)
    w1g = w1g.reshape(2, CMP_STRIDE * 2 * HEAD_DIM, 2 * 2 * CMP_HIDDEN)
    lw['c1_h'], lw['c1_l'] = _hilo(w1g)
    lw['cmp_pos'] = cmp_pos[l].reshape(2, 1, CMP_BLOCK * HEAD_DIM)
    lw['cmp_w1'] = cmp_w1[l]
    lw['cmp_b1'] = cmp_b1[l].reshape(2, 1, CMP_HIDDEN)
    w2 = cmp_w2[l]
    lw['cmp_w2bd'] = jnp.einsum('kjd,gf->kgjfd', w2, eye).reshape(2, 2 * CMP_HIDDEN, 2 * HEAD_DIM)
    lw['cmp_b2t'] = jnp.tile(cmp_b2[l].reshape(2, 1, HEAD_DIM), (1, 1, 2))
    lw['wo1'] = w_out[l][:RWKV_WIDTH].astype(BF16)
    lw['wo2'] = w_out[l][RWKV_WIDTH:].astype(BF16)
    lw['wu'] = ffn_w_up[l].astype(BF16)
    lw['wd'] = ffn_w_down[l].astype(BF16)
    return lw


def _rope_tables(pos):
    half = HEAD_DIM // 2
    inv_freq = ROPE_THETA ** (-jnp.arange(half, dtype=F32) / half)
    ang = pos.astype(F32)[:, None] * inv_freq[None, :]
    cos = jnp.cos(ang)
    sin = jnp.sin(ang)
    return jnp.tile(cos, (1, 4)), jnp.tile(jnp.concatenate([-sin, sin], axis=1), (1, 2))


def _mods6(m):
    d = m.shape[-1] // 6
    return [m[..., i * d:(i + 1) * d] for i in range(6)]


def _prompt_layer(x, mods, lw, tables):
    bsz, t, d = x.shape
    m = bsz * t
    tm = min(512, t)
    tpb = t // tm
    sh1, sc1, g1, sh2, sc2, g2 = [a.reshape(bsz, 1, d) for a in _mods6(mods)]
    x2 = x.reshape(m, d)
    p_r = _in_proj(x2, lw['norm_mix'], sc1, sh1, lw['wr'], tm, tpb)
    p_n = _in_proj(x2, lw['norm_mix'], sc1, sh1, lw['wn'], tm, tpb)
    shift0 = jnp.zeros((bsz, 1, SHIFT_COLS), F32)
    r, ld, k, v, a, b, g = _rwkv_prep(p_r, shift0, lw, tm, tpb, True)
    seq = lambda z: z.reshape(bsz, t, RWKV_WIDTH)
    q1, y0, gm, ha = _rwkv_chunk(seq(r), seq(ld), seq(k), seq(v), seq(a), seq(b))
    y, h_fin = _rwkv_scan(q1, y0, gm, ha)
    y_r = _rwkv_post(y.reshape(m, RWKV_WIDTH), r, k, v, g, lw, tm)
    s_new = h_fin.reshape(bsz, HEAD_DIM, RWKV_HEADS, HEAD_DIM).transpose(0, 2, 3, 1)
    shift_new = p_r.reshape(bsz, t, SHIFT_COLS)[:, -1]
    q, qr, cmp_rows, sel_rows, win_rows, gt, sel_k, sel_v, win_b = _nsa_proj(
        p_n, tables[0], tables[1], lw['nw'], tm, tpb)
    seqw = lambda z: z.reshape(bsz, t, z.shape[-1])
    ab = _cmp1_prompt(seqw(cmp_rows), lw['c1_h'], lw['c1_l'])
    ckv = _cmp2(ab, lw['cmp_c1'], lw)
    y_n = _attn_prompt(seqw(q), seqw(qr), seqw(gt), ckv, seqw(sel_k), seqw(sel_v), seqw(win_b))
    x2 = _out_proj(x2, y_r, y_n.reshape(m, NSA_WIDTH), g1, lw['wo1'], lw['wo2'], tm, tpb)
    tmf = min(1024, t)
    x2 = _ffn(x2, lw['norm_ffn'], sc2, sh2, g2, lw['wu'], lw['wd'], tmf, t // tmf)
    rows6 = lambda z: z.reshape(bsz, t, 2, 2, HEAD_DIM)
    wlen = min(WINDOW, t)
    return (x2.reshape(bsz, t, d), s_new, shift_new, rows6(cmp_rows), rows6(sel_rows), rows6(win_rows)[:, -wlen:])


def _sample_layer(x, mods, lw, tables, l, state_rwkv, state_shift, page_table, cache_cmp, cache_sel, cache_win,
                  n_pool, past_len):
    bs, d = x.shape
    tm = bs
    sh1, sc1, g1, sh2, sc2, g2 = [a.reshape(1, bs, d) for a in _mods6(mods)]
    p_r = _in_proj(x, lw['norm_mix'], sc1, sh1, lw['wr'], tm, 1)
    p_n = _in_proj(x, lw['norm_mix'], sc1, sh1, lw['wn'], tm, 1)
    r, ld, k, v, a, b, g = _rwkv_prep(p_r, state_shift[l], lw, tm, 1, False)
    s_new, y = _rwkv_step(state_rwkv, l, r, ld, k, a, b, v)
    y_r = _rwkv_post(y.reshape(bs, RWKV_WIDTH), r, k, v, g, lw, tm)
    q, qr, cmp_rows, sel_rows, win_rows, gt, _, _, _ = _nsa_proj(p_n, tables[0], tables[1], lw['nw'], tm, 1)
    ab = _cmp1_sample(page_table, cache_cmp, l * n_pool, lw['c1_h'], lw['c1_l'])
    ckv = _cmp2(ab, lw['cmp_c1'], lw)
    grp = (jnp.arange(NSA_HEADS) // (NSA_HEADS // 2))[None, :, None]
    half = (jnp.arange(LANES) // HEAD_DIM)[None, None, :]

    def pad_heads(z):
        z = z.reshape(bs, NSA_HEADS, HEAD_DIM)
        return jnp.where(grp == half, jnp.tile(z, (1, 1, 2)), 0.0)

    gates = jnp.pad(gt[:, :3 * NSA_HEADS].reshape(bs, NSA_HEADS, 3), ((0, 0), (0, 0), (0, LANES - 3)))
    o, win_new = _attn_sample(page_table, pad_heads(q), pad_heads(qr), gates, ckv, cache_sel, l * n_pool,
                              sel_rows.reshape(bs, 1, 256), cache_win, l * bs, win_rows.reshape(bs, 1, 256), past_len)
    o = o.reshape(bs, NSA_HEADS, 2, HEAD_DIM)
    y_n = jnp.concatenate([o[:, :NSA_HEADS // 2, 0], o[:, NSA_HEADS // 2:, 1]], axis=1).reshape(bs, NSA_WIDTH)
    x = _out_proj(x, y_r, y_n, g1, lw['wo1'], lw['wo2'], tm, 1)
    x = _ffn(x, lw['norm_ffn'], sc2, sh2, g2, lw['wu'], lw['wd'], tm, 1)
    rows6 = lambda z: z.reshape(bs, 1, 2, 2, HEAD_DIM)
    return (x, s_new, p_r, rows6(cmp_rows), rows6(sel_rows), win_new.reshape(bs, -1, 2, 2, HEAD_DIM))


def kernel(x_prompt, x_sample, cache_cmp_kv, cache_sel_kv, cache_win_kv, state_rwkv, state_shift, page_table,
           c_prompt, c_sample, ada_w, ada_b, norm_mix, norm_ffn, w_in, rwkv_mu, rwkv_w0, rwkv_w_up, rwkv_a0,
           rwkv_a_up, rwkv_g_up, rwkv_k_k, rwkv_k_a, rwkv_r_k, rwkv_ln_w, rwkv_ln_b, qk_norm, cmp_pos, cmp_w1,
           cmp_b1, cmp_w2, cmp_b2, w_out, ffn_w_up, ffn_w_down):
    depth = ada_w.shape[0]
    bsz, t, d = x_prompt.shape
    bs = x_sample.shape[0]
    n_pool, page = cache_cmp_kv.shape[1], cache_cmp_kv.shape[2]
    n_pages = page_table.shape[1]
    past_len = n_pages * page
    wb = cache_win_kv.shape[2]
    bc = -(-(bsz + bs) // 8) * 8
    c_all = jnp.pad(jnp.concatenate([c_prompt, c_sample], axis=0), ((0, bc - bsz - bs), (0, 0)))
    mods = _ada_mods(c_all, ada_w, ada_b)
    tab_p = _rope_tables(jnp.arange(t, dtype=jnp.int32))
    tab_s = _rope_tables(jnp.full((bs,), past_len, jnp.int32))
    cache_cmp = cache_cmp_kv.reshape(depth * n_pool, page, 256)
    cache_sel = cache_sel_kv.reshape(depth * n_pool, page, 256)
    cache_win = cache_win_kv.reshape(depth * bs, wb, 256)
    st_rwkv = state_rwkv.reshape((depth * bs,) + state_rwkv.shape[2:])
    y_p = x_prompt
    y_s = x_sample.reshape(bs, d)
    outs_p = []
    outs_s = []
    for l in range(depth):
        lw = _layer_weights(l, norm_mix, norm_ffn, w_in, rwkv_mu, rwkv_w0, rwkv_w_up, rwkv_a0, rwkv_a_up,
                            rwkv_g_up, rwkv_k_k, rwkv_k_a, rwkv_r_k, rwkv_ln_w, rwkv_ln_b, qk_norm, cmp_pos,
                            cmp_w1, cmp_b1, cmp_w2, cmp_b2, w_out, ffn_w_up, ffn_w_down)
        lw['cmp_c1'] = _cmp_bias(lw)
        res_p = _prompt_layer(y_p, mods[l, :bsz], lw, tab_p)
        y_p = res_p[0]
        outs_p.append(res_p[1:])
        res_s = _sample_layer(y_s, mods[l, bsz:bsz + bs], lw, tab_s, l, st_rwkv, state_shift, page_table,
                              cache_cmp, cache_sel, cache_win, n_pool, past_len)
        y_s = res_s[0]
        outs_s.append(res_s[1:])
    stack = lambda outs, i: jnp.stack([o[i] for o in outs])
    return (y_p, y_s.reshape(bs, 1, d),
            stack(outs_p, 2), stack(outs_s, 2),
            stack(outs_p, 3), stack(outs_s, 3),
            stack(outs_p, 4), stack(outs_s, 4),
            stack(outs_p, 0), stack(outs_s, 0),
            stack(outs_p, 1), stack(outs_s, 1))
```

```python
import functools

import numpy as np
import jax
import jax.numpy as jnp
from jax import lax
from jax.experimental import pallas as pl
from jax.experimental.pallas import tpu as pltpu

F32 = jnp.float32
BF16 = jnp.bfloat16

HEAD_DIM = 64
RWKV_WIDTH = 512
NSA_WIDTH = 512
RWKV_HEADS = 8
NSA_HEADS = 8
KV_WIDTH = 128
SHIFT_COLS = 1792
NSA_COLS = 1304
NSA_COLS_PAD = 1408
CMP_BLOCK = 32
CMP_STRIDE = 16
CMP_HIDDEN = 128
SEL_BLOCK = 64
SEL_TOP_N = 16
WINDOW = 512
ROPE_THETA = 10000.0
NORM_EPS = 1e-6
GN_EPS = 64e-5
NEG_INF = -1e30
FORCE = 1e9
LANES = 128
RWKV_CHUNK = 64
Q_TILE = 128
VMEM_LIMIT = 56 * 1024 * 1024


def _cparams(sem):
    return pltpu.CompilerParams(dimension_semantics=sem, vmem_limit_bytes=VMEM_LIMIT)


def _split2(x):
    hi = x.astype(BF16)
    lo = (x - hi.astype(F32)).astype(BF16)
    return hi, lo


def _split3(x):
    hi = x.astype(BF16)
    r = x - hi.astype(F32)
    mid = r.astype(BF16)
    lo = (r - mid.astype(F32)).astype(BF16)
    return hi, mid, lo


def _dg(a, b, ca=1, cb=0):
    return lax.dot_general(a, b, (((ca,), (cb,)), ((), ())), preferred_element_type=F32)


def _mm(a, b, ca=1, cb=0):
    ah, al = _split2(a)
    bh, bl = _split2(b)
    return _dg(ah, bh, ca, cb) + _dg(ah, bl, ca, cb) + _dg(al, bh, ca, cb)


def _bf(x):
    return x.astype(BF16)


def _mm1(a, b, ca=1, cb=0):
    return _dg(_bf(a), _bf(b), ca, cb)


def _mm_w(ah, al, wh, wl):
    return _dg(ah, wh) + _dg(ah, wl) + _dg(al, wh)


def _mmp(a2, b2, ca=1, cb=0):
    return _dg(a2[0], b2[0], ca, cb) + _dg(a2[0], b2[1], ca, cb) + _dg(a2[1], b2[0], ca, cb)


def _mm_xr(a, e, ca=1, cb=0):
    a0, a1, a2 = _split3(a)
    return _dg(a0, e, ca, cb) + _dg(a1, e, ca, cb) + _dg(a2, e, ca, cb)


def _mm_xl(e, b, ca=1, cb=0):
    b0, b1, b2 = _split3(b)
    return _dg(e, b0, ca, cb) + _dg(e, b1, ca, cb) + _dg(e, b2, ca, cb)


def _iota(shape, dim):
    return lax.broadcasted_iota(jnp.int32, shape, dim)


def _head_blockdiag():
    return (_iota((LANES, LANES), 0) // HEAD_DIM == _iota((LANES, LANES), 1) // HEAD_DIM).astype(BF16)


def _segsum(x, bd):
    w = x.shape[1]
    outs = [_mm_xr(x[:, j * LANES:(j + 1) * LANES], bd) for j in range(w // LANES)]
    return outs[0] if len(outs) == 1 else jnp.concatenate(outs, axis=1)


def _sigmoid(x):
    return 1.0 / (1.0 + jnp.exp(-x))


def _norm_mod(x, g, sc, sh):
    ms = jnp.mean(x * x, axis=-1, keepdims=True)
    y = x * lax.rsqrt(ms + NORM_EPS) * g
    return y * (1.0 + sc) + sh


def _topk_mask(score, n, axis):
    size = score.shape[axis]
    pos = _iota(score.shape, axis).astype(F32)
    sel = jnp.zeros_like(score)
    sc = score
    for _ in range(n):
        m = jnp.max(sc, axis=axis, keepdims=True)
        idx = jnp.min(jnp.where(sc == m, pos, float(size)), axis=axis, keepdims=True)
        hit = pos == idx
        sel = jnp.where(hit & (m > 0.5 * NEG_INF), 1.0, sel)
        sc = jnp.where(hit, -3e38, sc)
    return sel


def _ada_kernel(c_ref, w_ref, b_ref, o_ref):
    c = c_ref[...]
    o_ref[0] = _mm(c * _sigmoid(c), w_ref[0]) + b_ref[0]


def _ada_mods(c_all, ada_w, ada_b):
    depth, d, n = ada_w.shape
    bc = c_all.shape[0]
    tn = 1024
    return pl.pallas_call(
        _ada_kernel,
        out_shape=jax.ShapeDtypeStruct((depth, bc, n), F32),
        grid=(depth, n // tn),
        in_specs=[pl.BlockSpec((bc, d), lambda l, j: (0, 0)),
                  pl.BlockSpec((1, d, tn), lambda l, j: (l, 0, j)),
                  pl.BlockSpec((1, 1, tn), lambda l, j: (l, 0, j))],
        out_specs=pl.BlockSpec((1, bc, tn), lambda l, j: (l, 0, j)),
        compiler_params=_cparams(("arbitrary", "arbitrary")),
        name="ada_mods",
    )(c_all, ada_w, ada_b.reshape(depth, 1, n))


def _in_proj_kernel(x_ref, g_ref, sc_ref, sh_ref, *refs):
    w_refs, o_ref = refs[:-1], refs[-1]
    h = _norm_mod(x_ref[...], g_ref[...], sc_ref[0], sh_ref[0])
    if len(w_refs) == 1:
        o_ref[...] = _dg(h.astype(BF16), w_refs[0][...])
    else:
        hh, hl = _split2(h)
        o_ref[...] = _mm_w(hh, hl, w_refs[0][...], w_refs[1][...])


def _in_proj(x, g, sc, sh, ws, tm, tpb):
    m, d = x.shape
    n = ws[0].shape[1]
    r = sc.shape[1]
    return pl.pallas_call(
        _in_proj_kernel,
        out_shape=jax.ShapeDtypeStruct((m, n), F32),
        grid=(m // tm,),
        in_specs=[pl.BlockSpec((tm, d), lambda i: (i, 0)),
                  pl.BlockSpec((1, d), lambda i: (0, 0)),
                  pl.BlockSpec((1, r, d), lambda i: (i // tpb, 0, 0)),
                  pl.BlockSpec((1, r, d), lambda i: (i // tpb, 0, 0))]
                 + [pl.BlockSpec((d, n), lambda i: (0, 0)) for _ in ws],
        out_specs=pl.BlockSpec((tm, n), lambda i: (i, 0)),
        compiler_params=_cparams(("arbitrary",)),
        name="in_proj",
    )(x, g, sc, sh, *ws)


def _out_proj_kernel(x_ref, a1_ref, a2_ref, gate_ref, w1, w2, o_ref):
    acc = _dg(a1_ref[...].astype(BF16), w1[...]) + _dg(a2_ref[...].astype(BF16), w2[...])
    o_ref[...] = x_ref[...] + gate_ref[0] * acc


def _out_proj(x, a1, a2, gate, w1, w2, tm, tpb):
    m, d = x.shape
    k1 = a1.shape[1]
    k2 = a2.shape[1]
    r = gate.shape[1]
    return pl.pallas_call(
        _out_proj_kernel,
        out_shape=jax.ShapeDtypeStruct((m, d), F32),
        grid=(m // tm,),
        in_specs=[pl.BlockSpec((tm, d), lambda i: (i, 0)),
                  pl.BlockSpec((tm, k1), lambda i: (i, 0)),
                  pl.BlockSpec((tm, k2), lambda i: (i, 0)),
                  pl.BlockSpec((1, r, d), lambda i: (i // tpb, 0, 0)),
                  pl.BlockSpec((k1, d), lambda i: (0, 0)),
                  pl.BlockSpec((k2, d), lambda i: (0, 0))],
        out_specs=pl.BlockSpec((tm, d), lambda i: (i, 0)),
        compiler_params=_cparams(("arbitrary",)),
        name="out_proj",
    )(x, a1, a2, gate, w1, w2)


def _ffn_kernel(x_ref, g_ref, sc_ref, sh_ref, gate_ref, wg, wu, wd, o_ref, h_s, acc_s):
    j = pl.program_id(1)

    @pl.when(j == 0)
    def _():
        h_s[...] = _norm_mod(x_ref[...], g_ref[...], sc_ref[0], sh_ref[0]).astype(BF16)
        acc_s[...] = jnp.zeros_like(acc_s)

    h = h_s[...]
    gate = _dg(h, wg[...])
    up = _dg(h, wu[...])
    act = gate * _sigmoid(gate) * up
    acc_s[...] += _dg(act.astype(BF16), wd[...])

    @pl.when(j == pl.num_programs(1) - 1)
    def _():
        o_ref[...] = x_ref[...] + gate_ref[0] * acc_s[...]


def _ffn(x, g, sc, sh, gate, w_up, w_down, tm, tpb):
    m, d = x.shape
    dff = w_down.shape[0]
    tf = 256
    nf = dff // tf
    r = sc.shape[1]
    mod = pl.BlockSpec((1, r, d), lambda i, j: (i // tpb, 0, 0))
    return pl.pallas_call(
        _ffn_kernel,
        out_shape=jax.ShapeDtypeStruct((m, d), F32),
        grid=(m // tm, nf),
        in_specs=[pl.BlockSpec((tm, d), lambda i, j: (i, 0)),
                  pl.BlockSpec((1, d), lambda i, j: (0, 0)),
                  mod, mod, mod,
                  pl.BlockSpec((d, tf), lambda i, j: (0, j)),
                  pl.BlockSpec((d, tf), lambda i, j: (0, j + nf)),
                  pl.BlockSpec((tf, d), lambda i, j: (j, 0))],
        out_specs=pl.BlockSpec((tm, d), lambda i, j: (i, 0)),
        scratch_shapes=[pltpu.VMEM((tm, d), BF16), pltpu.VMEM((tm, d), F32)],
        compiler_params=_cparams(("arbitrary", "arbitrary")),
        name="ffn",
    )(x, g, sc, sh, gate, w_up, w_up, w_down)


def _rwkv_prep_kernel(p_ref, prev_ref, mu_ref, wch, wcl, guh, gul, w0_ref, a0_ref, kk_ref, ka_ref,
                      r_o, ld_o, k_o, v_o, a_o, b_o, g_o, *scratch, seq_mode, tpb):
    p = p_ref[...]
    tm = p.shape[0]
    if seq_mode:
        carry = scratch[0]
        i = pl.program_id(0)

        @pl.when(i % tpb == 0)
        def _():
            carry[...] = prev_ref[0]

        rolled = pltpu.roll(p, 1, 0)
        prev = jnp.where(_iota((tm, 1), 0) == 0, carry[...], rolled)
        carry[...] = p[tm - 1:tm, :]
    else:
        prev = prev_ref[...]
    xs = p + mu_ref[...] * (prev - p)
    w_ = RWKV_WIDTH
    r = xs[:, 0:w_]
    k = xs[:, w_:2 * w_]
    v = xs[:, 2 * w_:3 * w_]
    z = xs[:, 3 * w_:3 * w_ + LANES]
    gd = xs[:, 3 * w_ + LANES:3 * w_ + 2 * LANES]
    z = jnp.where(_iota((tm, LANES), 1) < 64, jnp.tanh(z), z)
    zh, zl = _split2(z)
    wa = _mm_w(zh, zl, wch[...], wcl[...])
    zw = -(w0_ref[...] + wa[:, 0:w_])
    softplus = jnp.maximum(zw, 0.0) + jnp.log1p(jnp.exp(-jnp.abs(zw)))
    wlog = -softplus - 0.5
    a_sig = _sigmoid(a0_ref[...] + wa[:, w_:2 * w_])
    sh_, sl_ = _split2(_sigmoid(gd))
    g = _mm_w(sh_, sl_, guh[...], gul[...])
    bd = _head_blockdiag()
    kkv = k * kk_ref[...]
    nrm = jnp.sqrt(_segsum(kkv * kkv, bd))
    kkn = kkv / jnp.maximum(nrm, 1e-12)
    r_o[...] = r
    ld_o[...] = -jnp.exp(wlog)
    k_o[...] = k * (1.0 + (a_sig - 1.0) * ka_ref[...])
    v_o[...] = v
    a_o[...] = -kkn
    b_o[...] = kkn * a_sig
    g_o[...] = g


def _rwkv_prep(p_r, prev, lw, tm, tpb, seq_mode):
    m = p_r.shape[0]
    w_ = RWKV_WIDTH
    row = lambda n: pl.BlockSpec((1, n), lambda i: (0, 0))
    full = lambda a: pl.BlockSpec(a.shape, lambda i: (0, 0))
    if seq_mode:
        prev_spec = pl.BlockSpec((1, 1, SHIFT_COLS), lambda i: (i // tpb, 0, 0))
        scratch = [pltpu.VMEM((1, SHIFT_COLS), F32)]
    else:
        prev_spec = pl.BlockSpec((tm, SHIFT_COLS), lambda i: (i, 0))
        scratch = []
    out = jax.ShapeDtypeStruct((m, w_), F32)
    ospec = pl.BlockSpec((tm, w_), lambda i: (i, 0))
    return pl.pallas_call(
        functools.partial(_rwkv_prep_kernel, seq_mode=seq_mode, tpb=tpb),
        out_shape=[out] * 7,
        grid=(m // tm,),
        in_specs=[pl.BlockSpec((tm, SHIFT_COLS), lambda i: (i, 0)), prev_spec, row(SHIFT_COLS),
                  full(lw['wc_h']), full(lw['wc_l']), full(lw['gu_h']), full(lw['gu_l']),
                  row(w_), row(w_), row(w_), row(w_)],
        out_specs=[ospec] * 7,
        scratch_shapes=scratch,
        compiler_params=_cparams(("arbitrary",)),
        name="rwkv_prep",
    )(p_r, prev, lw['mu'], lw['wc_h'], lw['wc_l'], lw['gu_h'], lw['gu_l'], lw['w0'], lw['a0'], lw['k_k'], lw['k_a'])


def _rwkv_chunk_kernel(r_ref, ld_ref, k_ref, v_ref, a_ref, b_ref, q1_o, y0_o, g_o, ha_o):
    c = RWKV_CHUNK
    c2 = 2 * c
    ld = ld_ref[0]
    lc = _mm_xl((_iota((c, c), 0) >= _iota((c, c), 1)).astype(BF16), ld)
    lx = lc - ld
    lend = lc[c - 1:c, :]
    p_in = jnp.exp(lc)
    inv_p = jnp.exp(-lc)
    p_to_end = jnp.exp(lend - lc)
    p_end = jnp.exp(lend)
    r = r_ref[0]
    k = k_ref[0]
    v = v_ref[0]
    b = b_ref[0]
    rt = r * p_in
    at = a_ref[0] * jnp.exp(lx)
    bt = b * inv_p
    kt = k * inv_p
    bh = b * p_to_end
    kh = k * p_to_end
    row = _iota((c2, c2), 0)
    col = _iota((c2, c2), 1)
    strict = row % c > col % c
    incl = row % c >= col % c
    eye = row == col
    lo = _iota((c, LANES), 1) < HEAD_DIM

    def pair_rows(x):
        return jnp.concatenate([jnp.where(lo, x, 0.0), jnp.where(lo, 0.0, x)], axis=0)

    pairs = range(RWKV_WIDTH // LANES)
    sls = [slice(LANES * j, LANES * (j + 1)) for j in pairs]
    at2 = [pair_rows(at[:, sl]) for sl in sls]
    rt2 = [pair_rows(rt[:, sl]) for sl in sls]
    v2s = [_bf(pair_rows(v[:, sl])) for sl in sls]
    lhs = [_bf(jnp.concatenate([at2[j], rt2[j]], axis=0)) for j in pairs]
    xb = [_dg(lhs[j], _bf(pair_rows(bt[:, sls[j]])), 1, 1) for j in pairs]
    xk = [_dg(lhs[j], _bf(pair_rows(kt[:, sls[j]])), 1, 1) for j in pairs]
    a_ab = [jnp.where(strict, xb[j][0:c2], 0.0) for j in pairs]
    a_rb = [_bf(jnp.where(incl, xb[j][c2:2 * c2], 0.0)) for j in pairs]
    a_ak = [_bf(jnp.where(strict, xk[j][0:c2], 0.0)) for j in pairs]
    a_rk = [_bf(jnp.where(incl, xk[j][c2:2 * c2], 0.0)) for j in pairs]
    akv = [_dg(a_ak[j], v2s[j]) for j in pairs]
    x = [jnp.where(eye, 1.0, 0.0) + a_ab[j] for j in pairs]
    mp = a_ab
    for _ in range(int(np.log2(c)) - 1):
        mps = [_bf(mp[j]) for j in pairs]
        mp = [_dg(mps[j], mps[j]) for j in pairs]
        x = [x[j] + _dg(_bf(mp[j]), _bf(x[j])) for j in pairs]
    wu = [_bf(_dg(_bf(x[j]), _bf(jnp.concatenate([at2[j], akv[j]], axis=1)))) for j in pairs]
    qy = [_dg(a_rb[j], wu[j]) for j in pairs]
    ark_v = [_dg(a_rk[j], v2s[j]) for j in pairs]
    gh = [_dg(_bf(pair_rows(bh[:, sls[j]])), wu[j], 0, 0) for j in pairs]
    khv = [_dg(_bf(pair_rows(kh[:, sls[j]])), v2s[j], 0, 0) for j in pairs]
    for j in pairs:
        sl = sls[j]
        q1 = rt2[j] + qy[j][:, 0:LANES]
        y0 = qy[j][:, LANES:2 * LANES] + ark_v[j]
        q1_o[0, :, sl] = q1[0:c] + q1[c:c2]
        y0_o[0, :, sl] = y0[0:c] + y0[c:c2]
        g_o[0, 0, :, sl] = jnp.where(eye, p_end[:, sl], 0.0) + gh[j][:, 0:LANES]
        ha_o[0, 0, :, sl] = gh[j][:, LANES:2 * LANES] + khv[j]


def _rwkv_chunk(r, ld, k, v, a, b):
    bsz, t, w_ = r.shape
    c = RWKV_CHUNK
    nch = t // c
    ispec = pl.BlockSpec((1, c, w_), lambda i, j: (i, j, 0))
    sspec = pl.BlockSpec((1, 1, 2 * HEAD_DIM, w_), lambda i, j: (i, j, 0, 0))
    return pl.pallas_call(
        _rwkv_chunk_kernel,
        out_shape=[jax.ShapeDtypeStruct((bsz, t, w_), F32)] * 2
                  + [jax.ShapeDtypeStruct((bsz, nch, 2 * HEAD_DIM, w_), F32)] * 2,
        grid=(bsz, nch),
        in_specs=[ispec] * 6,
        out_specs=[ispec, ispec, sspec, sspec],
        compiler_params=_cparams(("arbitrary", "arbitrary")),
        name="rwkv_chunk",
    )(r, ld, k, v, a, b)


def _rwkv_scan_kernel(q1_ref, y0_ref, g_ref, ha_ref, y_o, hf_o, h_s):
    j = pl.program_id(0)

    @pl.when(j == 0)
    def _():
        h_s[...] = jnp.zeros_like(h_s)

    src = j % 2
    dst = (j + 1) % 2
    for bi in range(q1_ref.shape[0]):
        for p in range(RWKV_WIDTH // LANES):
            sl = slice(LANES * p, LANES * (p + 1))
            h0 = _split2(h_s[src, bi, :, sl])
            y_o[bi, :, sl] = _mmp(_split2(q1_ref[bi, :, sl]), h0) + y0_ref[bi, :, sl]
            h_s[dst, bi, :, sl] = _mmp(_split2(g_ref[bi, 0, :, sl]), h0) + ha_ref[bi, 0, :, sl]

    @pl.when(j == pl.num_programs(0) - 1)
    def _():
        hf_o[...] = h_s[dst, :, 0:HEAD_DIM, :] + h_s[dst, :, HEAD_DIM:2 * HEAD_DIM, :]


def _rwkv_scan(q1, y0, g, ha):
    bsz, t, w_ = q1.shape
    c = RWKV_CHUNK
    ispec = pl.BlockSpec((bsz, c, w_), lambda j: (0, j, 0))
    sspec = pl.BlockSpec((bsz, 1, 2 * HEAD_DIM, w_), lambda j: (0, j, 0, 0))
    return pl.pallas_call(
        _rwkv_scan_kernel,
        out_shape=[jax.ShapeDtypeStruct((bsz, t, w_), F32), jax.ShapeDtypeStruct((bsz, HEAD_DIM, w_), F32)],
        grid=(t // c,),
        in_specs=[ispec, ispec, sspec, sspec],
        out_specs=[ispec, pl.BlockSpec((bsz, HEAD_DIM, w_), lambda j: (0, 0, 0))],
        scratch_shapes=[pltpu.VMEM((2, bsz, 2 * HEAD_DIM, w_), F32)],
        compiler_params=_cparams(("arbitrary",)),
        name="rwkv_scan",
    )(q1, y0, g, ha)


def _rwkv_step_kernel(s_ref, r_ref, ld_ref, k_ref, a_ref, b_ref, v_ref, so_ref, y_ref):
    s = s_ref[...]
    sa = jnp.sum(s * a_ref[...], axis=-1, keepdims=True)
    s_new = s * jnp.exp(ld_ref[...]) + sa * b_ref[...] + v_ref[...] * k_ref[...]
    so_ref[...] = s_new
    y_ref[...] = jnp.sum(s_new * r_ref[...], axis=-1, keepdims=True)


def _rwkv_step(state, layer, r, ld, k, a, b, v):
    bs = r.shape[0]
    nh, hd = RWKV_HEADS, HEAD_DIM
    bt = min(8, bs)
    rowv = lambda x: x.reshape(bs, nh, 1, hd)
    vspec = pl.BlockSpec((bt, nh, 1, hd), lambda i: (i, 0, 0, 0))
    cspec = pl.BlockSpec((bt, nh, hd, 1), lambda i: (i, 0, 0, 0))
    off = layer * (bs // bt)
    return pl.pallas_call(
        _rwkv_step_kernel,
        out_shape=[jax.ShapeDtypeStruct((bs, nh, hd, hd), F32), jax.ShapeDtypeStruct((bs, nh, hd, 1), F32)],
        grid=(bs // bt,),
        in_specs=[pl.BlockSpec((bt, nh, hd, hd), lambda i: (i + off, 0, 0, 0)),
                  vspec, vspec, vspec, vspec, vspec, cspec],
        out_specs=[pl.BlockSpec((bt, nh, hd, hd), lambda i: (i, 0, 0, 0)), cspec],
        compiler_params=_cparams(("arbitrary",)),
        name="rwkv_step",
    )(state, rowv(r), rowv(ld), rowv(k), rowv(a), rowv(b), v.reshape(bs, nh, hd, 1))


def _rwkv_post_kernel(y_ref, r_ref, k_ref, v_ref, g_ref, rk_ref, lw_ref, lb_ref, o_ref):
    bd = _head_blockdiag()
    y = y_ref[...]
    inv = 1.0 / HEAD_DIM
    mu = _segsum(y, bd) * inv
    d = y - mu
    var = _segsum(d * d, bd) * inv
    yn = d * lax.rsqrt(var + GN_EPS) * lw_ref[...] + lb_ref[...]
    v = v_ref[...]
    bonus = _segsum(r_ref[...] * k_ref[...] * rk_ref[...], bd) * v
    o_ref[...] = (yn + bonus) * g_ref[...]


def _rwkv_post(y, r, k, v, g, lw, tm):
    m, w_ = y.shape
    tile = pl.BlockSpec((tm, w_), lambda i: (i, 0))
    row = pl.BlockSpec((1, w_), lambda i: (0, 0))
    return pl.pallas_call(
        _rwkv_post_kernel,
        out_shape=jax.ShapeDtypeStruct((m, w_), F32),
        grid=(m // tm,),
        in_specs=[tile] * 5 + [row] * 3,
        out_specs=tile,
        compiler_params=_cparams(("arbitrary",)),
        name="rwkv_post",
    )(y, r, k, v, g, lw['r_k'], lw['ln_w'], lw['ln_b'])


def _nsa_proj_kernel(p_ref, cos_ref, sin_ref, nw_ref, q_o, qr_o, cmp_o, sel_o, win_o, gt_o, selk_o, selv_o, winb_o):
    tm = p_ref.shape[0]
    cos = cos_ref[...]
    sin = sin_ref[...]
    bd = _head_blockdiag()
    first = (_iota((tm, LANES), 1) % HEAD_DIM) < HEAD_DIM // 2
    nw = nw_ref[...]

    def norm(x, w):
        ms = _segsum(x * x, bd) * (1.0 / HEAD_DIM)
        return x * lax.rsqrt(ms + NORM_EPS) * w

    def rope(x):
        rot = jnp.where(first, pltpu.roll(x, LANES - HEAD_DIM // 2, 1), pltpu.roll(x, HEAD_DIM // 2, 1))
        return x * cos + rot * sin

    for j in range(NSA_WIDTH // LANES):
        sl = slice(LANES * j, LANES * (j + 1))
        xn = norm(p_ref[:, sl], nw[0:1])
        q_o[:, sl] = xn
        qr_o[:, sl] = rope(xn)
    o = NSA_WIDTH
    cmp_o[...] = p_ref[:, o:o + 2 * KV_WIDTH]
    ks = rope(norm(p_ref[:, o + 256:o + 384], nw[2:3]))
    vs = p_ref[:, o + 384:o + 512]
    kw = rope(norm(p_ref[:, o + 512:o + 640], nw[3:4]))
    vw = p_ref[:, o + 640:o + 768]
    sel_o[:, 0:LANES] = ks
    sel_o[:, LANES:2 * LANES] = vs
    win_o[:, 0:LANES] = kw
    win_o[:, LANES:2 * LANES] = vw
    gt_o[...] = _sigmoid(p_ref[:, o + 768:o + 896])
    lo = _iota((tm, LANES), 1) < HEAD_DIM
    selk_o[...] = ks.astype(BF16)
    selv_o[:, 0:LANES] = jnp.where(lo, vs, 1.0).astype(BF16)
    selv_o[:, LANES:2 * LANES] = jnp.where(lo, 1.0, vs).astype(BF16)
    winb_o[:, 0:LANES] = kw.astype(BF16)
    winb_o[:, LANES:2 * LANES] = vw.astype(BF16)


def _nsa_proj(p_n, cos_t, sin_t, nw, tm, tpb):
    m = p_n.shape[0]
    tt = cos_t.shape[0] // tm
    tab = pl.BlockSpec((tm, LANES), lambda i: (i % tt, 0))
    shapes = [(m, NSA_WIDTH), (m, NSA_WIDTH), (m, 256), (m, 256), (m, 256), (m, LANES),
              (m, LANES), (m, 256), (m, 256)]
    dtypes = [F32] * 6 + [BF16] * 3
    return pl.pallas_call(
        _nsa_proj_kernel,
        out_shape=[jax.ShapeDtypeStruct(s, dt) for s, dt in zip(shapes, dtypes)],
        grid=(m // tm,),
        in_specs=[pl.BlockSpec((tm, NSA_COLS_PAD), lambda i: (i, 0)), tab, tab,
                  pl.BlockSpec((4, LANES), lambda i: (0, 0))],
        out_specs=[pl.BlockSpec((tm, s[1]), lambda i: (i, 0)) for s in shapes],
        compiler_params=_cparams(("arbitrary",)),
        name="nsa_proj",
    )(p_n, cos_t, sin_t, nw)


def _cmp1_kernel(*refs, n_parts, paged):
    if paged:
        refs = refs[1:]
    parts = refs[:n_parts]
    w_ref, o_ref, xs = refs[n_parts:]
    gpp = parts[0].shape[1] // CMP_STRIDE
    for j, pr in enumerate(parts):
        for l in range(CMP_STRIDE):
            xs[gpp * j:gpp * (j + 1), LANES * l:LANES * (l + 1)] = pr[0, pl.ds(l, gpp, stride=CMP_STRIDE), :]
    res = _dg(_bf(xs[...]), w_ref[0])
    ng = o_ref.shape[2]
    for s in range(o_ref.shape[0]):
        o_ref[s, 0] = res[ng * s:ng * (s + 1)]


def _cmp1_prompt(cmp_rows, w):
    bsz, t, _ = cmp_rows.shape
    rows = min(2048, t)
    ng = rows // CMP_STRIDE
    kdim = CMP_STRIDE * LANES
    return pl.pallas_call(
        functools.partial(_cmp1_kernel, n_parts=1, paged=False),
        out_shape=jax.ShapeDtypeStruct((bsz, 2, t // CMP_STRIDE, 512), F32),
        grid=(bsz, t // rows, 2),
        in_specs=[pl.BlockSpec((1, rows, LANES), lambda b, i, kv: (b, i, kv)),
                  pl.BlockSpec((1, kdim, 512), lambda b, i, kv: (kv, 0, 0))],
        out_specs=pl.BlockSpec((1, 1, ng, 512), lambda b, i, kv: (b, kv, i, 0)),
        scratch_shapes=[pltpu.VMEM((ng, kdim), F32)],
        compiler_params=_cparams(("arbitrary", "arbitrary", "arbitrary")),
        name="cmp1_prompt",
    )(cmp_rows, w)


def _cmp1_sample(page_table, cache, page_off, w):
    bs, n_pages = page_table.shape
    page = cache.shape[1]
    ng = n_pages * page // CMP_STRIDE
    kdim = CMP_STRIDE * LANES
    sps = 4 if bs % 4 == 0 else 1

    def page_spec(s, j):
        return pl.BlockSpec((1, page, LANES), lambda b, kv, pt: (pt[b * sps + s, j] + page_off, 0, kv))

    return pl.pallas_call(
        functools.partial(_cmp1_kernel, n_parts=sps * n_pages, paged=True),
        out_shape=jax.ShapeDtypeStruct((bs, 2, ng, 512), F32),
        grid_spec=pltpu.PrefetchScalarGridSpec(
            num_scalar_prefetch=1,
            grid=(bs // sps, 2),
            in_specs=[page_spec(s, j) for s in range(sps) for j in range(n_pages)] + [
                pl.BlockSpec((1, kdim, 512), lambda b, kv, pt: (kv, 0, 0))],
            out_specs=pl.BlockSpec((sps, 1, ng, 512), lambda b, kv, pt: (b, kv, 0, 0)),
            scratch_shapes=[pltpu.VMEM((sps * ng, kdim), F32)]),
        compiler_params=_cparams(("arbitrary", "arbitrary")),
        name="cmp1_sample",
    )(page_table, *([cache] * (sps * n_pages)), w)


def _cmp_bias_kernel(pos_ref, w1_ref, b1_ref, o_ref):
    posb = jnp.broadcast_to(pos_ref[0], (8, pos_ref.shape[2]))
    o_ref[0] = _mm(posb, w1_ref[0])[0:1] + b1_ref[0]


def _cmp_bias(lw):
    kdim = CMP_BLOCK * HEAD_DIM
    return pl.pallas_call(
        _cmp_bias_kernel,
        out_shape=jax.ShapeDtypeStruct((2, 1, CMP_HIDDEN), F32),
        grid=(2,),
        in_specs=[pl.BlockSpec((1, 1, kdim), lambda kv: (kv, 0, 0)),
                  pl.BlockSpec((1, kdim, CMP_HIDDEN), lambda kv: (kv, 0, 0)),
                  pl.BlockSpec((1, 1, CMP_HIDDEN), lambda kv: (kv, 0, 0))],
        out_specs=pl.BlockSpec((1, 1, CMP_HIDDEN), lambda kv: (kv, 0, 0)),
        compiler_params=_cparams(("arbitrary",)),
        name="cmp_bias",
    )(lw['cmp_pos'], lw['cmp_w1'], lw['cmp_b1'])


def _cmp2_kernel(ab_ref, c1_ref, w2_ref, b2_ref, nw_ref, o_ref):
    kv = pl.program_id(1)
    ab = ab_ref[0, 0]
    ng = ab.shape[0]
    a_part = ab[:, 0:256]
    b_next = pltpu.roll(ab[:, 256:512], ng - 1, 0)
    c1 = c1_ref[0]
    h = a_part + b_next + jnp.concatenate([c1, c1], axis=1)
    h = 0.5 * h * (1.0 + jnp.tanh(np.sqrt(2.0 / np.pi) * (h + 0.044715 * (h * h * h))))
    out = _mm1(h, w2_ref[0]) + b2_ref[0]
    ms = _segsum(out * out, _head_blockdiag()) * (1.0 / HEAD_DIM)
    normed = out * lax.rsqrt(ms + NORM_EPS) * nw_ref[...]
    o_ref[0, 0] = jnp.where(kv == 0, normed, out)


def _cmp2(ab, c1, lw):
    bsz, _, ng, _ = ab.shape
    return pl.pallas_call(
        _cmp2_kernel,
        out_shape=jax.ShapeDtypeStruct((bsz, 2, ng, LANES), F32),
        grid=(bsz, 2),
        in_specs=[pl.BlockSpec((1, 1, ng, 512), lambda b, kv: (b, kv, 0, 0)),
                  pl.BlockSpec((1, 1, CMP_HIDDEN), lambda b, kv: (kv, 0, 0)),
                  pl.BlockSpec((1, 2 * CMP_HIDDEN, LANES), lambda b, kv: (kv, 0, 0)),
                  pl.BlockSpec((1, 1, LANES), lambda b, kv: (kv, 0, 0)),
                  pl.BlockSpec((1, LANES), lambda b, kv: (0, 0))],
        out_specs=pl.BlockSpec((1, 1, ng, LANES), lambda b, kv: (b, kv, 0, 0)),
        compiler_params=_cparams(("arbitrary", "arbitrary")),
        name="cmp2",
    )(ab, c1, lw['cmp_w2bd'], lw['cmp_b2t'], lw['nw_kc'])


def _masked_softmax(s, mask):
    s = jnp.where(mask, s, NEG_INF)
    mx = jnp.max(s, axis=-1, keepdims=True)
    e = jnp.where(mask, jnp.exp(s - mx), 0.0)
    den = jnp.sum(e, axis=-1, keepdims=True)
    return e / jnp.maximum(den, 1e-30)


def _selection_mask(imp, qpos, axis):
    blk = _iota(imp.shape, axis)
    cur = qpos // SEL_BLOCK
    valid = blk * SEL_BLOCK <= qpos
    forced = (blk == 0) | (blk == cur) | (blk == cur - 1)
    score = jnp.where(valid, jnp.where(forced, FORCE, imp), NEG_INF)
    return _topk_mask(score, SEL_TOP_N, axis)


def _attn_prompt_kernel(q_ref, qr_ref, gt_ref, kc_ref, vc_ref, ks_ref, vs0_ref, vs1_ref, *rest, ck):
    nwb = WINDOW // Q_TILE + 1
    kw_refs = rest[0:nwb]
    vw_refs = rest[nwb:2 * nwb]
    ovt_ref, en_ref, o_ref, qg_s, qrg_s, m_s, acc_s = rest[2 * nwb:]
    qi = pl.program_id(1)
    tq = Q_TILE
    hpg = NSA_HEADS // 2
    scale = HEAD_DIM ** -0.5
    ng = kc_ref.shape[2]
    lo_lanes = _iota((tq, LANES), 1) < HEAD_DIM
    qpos = qi * tq + _iota((tq, 1), 0)
    qpos_t = qi * tq + _iota((1, tq), 1)
    qpos_all = jnp.concatenate([qpos] * NSA_HEADS, axis=0)
    gt = gt_ref[0]
    kc = kc_ref[0, 0]
    vc = vc_ref[0, 0]
    kw = jnp.concatenate([r[0] for r in kw_refs], axis=0)
    vw = jnp.concatenate([r[0] for r in vw_refs], axis=0)
    kwpos = jnp.concatenate([(qi - (nwb - 1) + j) * tq + _iota((1, tq), 1) for j in range(nwb)], axis=1)
    nch = (qi * tq + tq + ck - 1) // ck
    rows = hpg * tq
    for h in range(NSA_HEADS):
        g = h // hpg
        cb = h // 2
        rs = slice(tq * h, tq * (h + 1))
        keep = lo_lanes if g == 0 else jnp.logical_not(lo_lanes)
        blk = q_ref[0, :, LANES * cb:LANES * (cb + 1)]
        blkr = qr_ref[0, :, LANES * cb:LANES * (cb + 1)]
        if (h % 2 == 0) != (g == 0):
            blk = pltpu.roll(blk, HEAD_DIM, 1)
            blkr = pltpu.roll(blkr, HEAD_DIM, 1)
        qg_s[rs, :] = (jnp.where(keep, blk, 0.0) * scale).astype(BF16)
        qrg_s[rs, :] = (jnp.where(keep, blkr, 0.0) * scale).astype(BF16)
    qrg = qrg_s[...]
    kcb = _bf(kc)
    s = _dg(qg_s[...], kcb, 1, 1)
    mc = (_iota((2 * rows, ng), 1) * CMP_STRIDE + (CMP_BLOCK - 1)) <= qpos_all
    p = _masked_softmax(s, mc)
    oc = _mm1(p, vc)
    nselb = []
    for g in range(2):
        b0 = rows * g
        psum = p[b0:b0 + tq] + p[b0 + tq:b0 + 2 * tq] + p[b0 + 2 * tq:b0 + 3 * tq] + p[b0 + 3 * tq:b0 + 4 * tq]
        imp_t = _mm_xl(ovt_ref[...], psum.T)
        sel_t = _selection_mask(imp_t, qpos_t, 0)
        nselb.append((1.0 - sel_t).T.astype(BF16))
    m_s[...] = jnp.full(m_s.shape, 0.1 * NEG_INF, F32)
    acc_s[...] = jnp.zeros(acc_s.shape, F32)

    def chunk(c, diagonal):
        st = pl.multiple_of(c * ck, ck)
        sc = _dg(qrg, ks_ref[0, pl.ds(st, ck), :], 1, 1)
        causal = jnp.where(st + _iota((tq, ck), 1) <= qpos, 0.0, NEG_INF) if diagonal else None
        for g in range(2):
            vs_ref = vs0_ref if g == 0 else vs1_ref
            vch = vs_ref[0, pl.ds(st, ck), :]
            bias = _dg(nselb[g], en_ref[c])
            if diagonal:
                bias = bias + causal
            for hh in range(hpg):
                rs = slice(rows * g + tq * hh, rows * g + tq * (hh + 1))
                sh = sc[rs] + bias
                m_old = m_s[rs, :]
                m_new = jnp.maximum(m_old, jnp.max(sh, axis=-1, keepdims=True))
                alpha = jnp.exp(m_old - m_new)
                pe = jnp.exp(sh - m_new)
                acc_s[rs, :] = alpha * acc_s[rs, :] + _dg(pe.astype(BF16), vch)
                m_s[rs, :] = m_new

    def body(c, carry):
        chunk(c, False)
        return carry

    lax.fori_loop(0, nch - 1, body, 0)
    chunk(nch - 1, True)
    acc = acc_s[...]
    osel = acc / pltpu.roll(acc, HEAD_DIM, 1)
    sw = _dg(qrg, kw, 1, 1)
    mw = (kwpos <= qpos_all) & (kwpos > qpos_all - WINDOW) & (kwpos >= 0)
    ow = _dg(_masked_softmax(sw, mw).astype(BF16), vw)
    for h in range(NSA_HEADS):
        g = h // hpg
        rs = slice(tq * h, tq * (h + 1))
        og = (gt[:, 3 * h:3 * h + 1] * oc[rs] + gt[:, 3 * h + 1:3 * h + 2] * osel[rs]
              + gt[:, 3 * h + 2:3 * h + 3] * ow[rs])
        o_ref[0, :, HEAD_DIM * h:HEAD_DIM * (h + 1)] = og[:, HEAD_DIM * g:HEAD_DIM * (g + 1)]


def _overlap_matrix(ng):
    s = np.arange(ng)[:, None] * CMP_STRIDE
    j = np.arange(LANES)[None, :] * SEL_BLOCK
    return ((s < j + SEL_BLOCK) & (s + CMP_BLOCK > j)).astype(np.float32)


def _block_bias_matrix(t, ck):
    key_blk = (np.arange(t) // SEL_BLOCK).reshape(t // ck, 1, ck)
    blk = np.arange(LANES).reshape(1, LANES, 1)
    return jnp.asarray(np.where(key_blk == blk, NEG_INF, 0.0).astype(np.float32), dtype=BF16)


def _attn_prompt(q, qr, gt, ckv, selk, selv, win):
    bsz, t, _ = q.shape
    tq = Q_TILE
    ng = ckv.shape[2]
    ck = min(1024, t)
    nwb = WINDOW // tq + 1
    hpg = NSA_HEADS // 2
    assert t // SEL_BLOCK <= LANES and t % ck == 0
    ovt = jnp.asarray(_overlap_matrix(ng).T, dtype=BF16)

    def wspec(j, col):
        return pl.BlockSpec((1, tq, LANES), lambda b, i: (b, jnp.maximum(i - (nwb - 1) + j, 0), col))

    tile = lambda w: pl.BlockSpec((1, tq, w), lambda b, i: (b, i, 0))
    return pl.pallas_call(
        functools.partial(_attn_prompt_kernel, ck=ck),
        out_shape=jax.ShapeDtypeStruct((bsz, t, NSA_WIDTH), F32),
        grid=(bsz, t // tq),
        in_specs=[tile(NSA_WIDTH), tile(NSA_WIDTH), tile(LANES),
                  pl.BlockSpec((1, 1, ng, LANES), lambda b, i: (b, 0, 0, 0)),
                  pl.BlockSpec((1, 1, ng, LANES), lambda b, i: (b, 1, 0, 0)),
                  pl.BlockSpec((1, t, LANES), lambda b, i: (b, 0, 0)),
                  pl.BlockSpec((1, t, LANES), lambda b, i: (b, 0, 0)),
                  pl.BlockSpec((1, t, LANES), lambda b, i: (b, 0, 1))]
                 + [wspec(j, 0) for j in range(nwb)] + [wspec(j, 1) for j in range(nwb)]
                 + [pl.BlockSpec((LANES, ng), lambda b, i: (0, 0)),
                    pl.BlockSpec((t // ck, LANES, ck), lambda b, i: (0, 0, 0))],
        out_specs=tile(NSA_WIDTH),
        scratch_shapes=[pltpu.VMEM((NSA_HEADS * tq, LANES), BF16), pltpu.VMEM((NSA_HEADS * tq, LANES), BF16),
                        pltpu.VMEM((NSA_HEADS * tq, 1), F32), pltpu.VMEM((NSA_HEADS * tq, LANES), F32)],
        compiler_params=_cparams(("arbitrary", "arbitrary")),
        name="attn_prompt",
    )(q, qr, gt, ckv, ckv, selk, selv, selv, *([win] * (2 * nwb)), ovt, _block_bias_matrix(t, ck))


def _attn_sample_kernel(pt_ref, q_ref, qr_ref, gt_ref, ckv_ref, *rest, past_len, n_pages):
    pages = rest[:n_pages]
    nsel_ref, win_ref, nwin_ref, ov_ref, o_ref, wout_ref = rest[n_pages:]
    nh = NSA_HEADS
    scale = HEAD_DIM ** -0.5
    qpos = past_len
    q = q_ref[0]
    qr = qr_ref[0]
    gt = gt_ref[0]
    kc = ckv_ref[0, 0]
    vc = ckv_ref[0, 1]
    ng = kc.shape[0]
    s = _mm1(q, kc, 1, 1) * scale
    mc = (_iota((nh, ng), 1) * CMP_STRIDE + (CMP_BLOCK - 1)) <= qpos
    p = _masked_softmax(s, mc)
    oc = _mm1(p, vc)
    same_group = (_iota((nh, nh), 0) // (nh // 2) == _iota((nh, nh), 1) // (nh // 2)).astype(BF16)
    imp = _mm_xr(_mm_xl(same_group, p), ov_ref[...])
    selb = _selection_mask(imp, qpos, 1).astype(BF16)
    page = pages[0].shape[1]
    nsel = nsel_ref[0]
    qrb = qr.astype(BF16)
    s_new = jnp.sum(qr * nsel[:, 0:LANES], axis=-1, keepdims=True) * scale
    scores = []
    oks = []
    m = s_new
    for j, pg in enumerate(pages):
        sj = _dg(qrb, pg[0, :, 0:LANES].astype(BF16), 1, 1) * scale
        expand = (_iota((LANES, page), 0) == (j * page + _iota((LANES, page), 1)) // SEL_BLOCK).astype(BF16)
        ok = _dg(selb, expand) > 0.5
        sj = jnp.where(ok, sj, NEG_INF)
        m = jnp.maximum(m, jnp.max(sj, axis=-1, keepdims=True))
        scores.append(sj)
        oks.append(ok)
    e_new = jnp.exp(s_new - m)
    den = e_new
    acc = e_new * nsel[:, LANES:2 * LANES]
    for j, pg in enumerate(pages):
        e = jnp.where(oks[j], jnp.exp(scores[j] - m), 0.0)
        den = den + jnp.sum(e, axis=-1, keepdims=True)
        acc = acc + _dg(e.astype(BF16), pg[0, :, LANES:2 * LANES].astype(BF16))
    osel = acc / den
    wb = win_ref.shape[1]
    wnd = win_ref[0]
    nwin = nwin_ref[0]
    sw = _dg(qrb, wnd[:, 0:LANES].astype(BF16), 1, 1) * scale
    kwpos = past_len - wb + _iota((1, wb), 1)
    mw = (kwpos <= qpos) & (kwpos > qpos - WINDOW) & (kwpos >= 0)
    sw = jnp.where(mw, sw, NEG_INF)
    sw_new = jnp.sum(qr * nwin[:, 0:LANES], axis=-1, keepdims=True) * scale
    mx = jnp.maximum(jnp.max(sw, axis=-1, keepdims=True), sw_new)
    ew = jnp.where(mw, jnp.exp(sw - mx), 0.0)
    ew_new = jnp.exp(sw_new - mx)
    ow = (_dg(ew.astype(BF16), wnd[:, LANES:2 * LANES].astype(BF16)) + ew_new * nwin[:, LANES:2 * LANES]) / (
        jnp.sum(ew, axis=-1, keepdims=True) + ew_new)
    o_ref[0] = gt[:, 0:1] * oc + gt[:, 1:2] * osel + gt[:, 2:3] * ow
    shifted = pltpu.roll(wnd, wb - 1, 0)
    wout_ref[0] = jnp.where(_iota((wb, 1), 0) == wb - 1, nwin, shifted)


def _attn_sample(page_table, q, qr, gt, ckv, cache_sel, page_off, nsel, cache_win, win_off, nwin, past_len):
    bs, n_pages = page_table.shape
    page = cache_sel.shape[1]
    wb = cache_win.shape[1]
    ng = ckv.shape[2]
    nh = NSA_HEADS
    assert wb == WINDOW and past_len // SEL_BLOCK + 1 <= LANES

    def page_spec(j):
        return pl.BlockSpec((1, page, 256), lambda b, pt: (pt[b, j] + page_off, 0, 0))

    vec = lambda w: pl.BlockSpec((1, nh, w), lambda b, pt: (b, 0, 0))
    rowspec = pl.BlockSpec((1, 1, 256), lambda b, pt: (b, 0, 0))
    return pl.pallas_call(
        functools.partial(_attn_sample_kernel, past_len=past_len, n_pages=n_pages),
        out_shape=[jax.ShapeDtypeStruct((bs, nh, LANES), F32), jax.ShapeDtypeStruct((bs, wb, 256), F32)],
        grid_spec=pltpu.PrefetchScalarGridSpec(
            num_scalar_prefetch=1,
            grid=(bs,),
            in_specs=[vec(LANES), vec(LANES), vec(LANES),
                      pl.BlockSpec((1, 2, ng, LANES), lambda b, pt: (b, 0, 0, 0))]
                     + [page_spec(j) for j in range(n_pages)]
                     + [rowspec,
                        pl.BlockSpec((1, wb, 256), lambda b, pt: (b + win_off, 0, 0)),
                        rowspec,
                        pl.BlockSpec((ng, LANES), lambda b, pt: (0, 0))],
            out_specs=[vec(LANES), pl.BlockSpec((1, wb, 256), lambda b, pt: (b, 0, 0))]),
        compiler_params=_cparams(("arbitrary",)),
        name="attn_sample",
    )(page_table, q, qr, gt, ckv, *([cache_sel] * n_pages), nsel, cache_win, nwin,
      jnp.asarray(_overlap_matrix(ng), dtype=BF16))


def _hilo(w):
    hi = w.astype(BF16)
    return hi, (w - hi.astype(F32)).astype(BF16)


def _tile_heads(v, reps):
    return jnp.tile(v.reshape(1, -1), (1, reps))


def _layer_weights(l, norm_mix, norm_ffn, w_in, rwkv_mu, rwkv_w0, rwkv_w_up, rwkv_a0, rwkv_a_up, rwkv_g_up,
                   rwkv_k_k, rwkv_k_a, rwkv_r_k, rwkv_ln_w, rwkv_ln_b, qk_norm, cmp_pos, cmp_w1, cmp_b1, cmp_w2,
                   cmp_b2, w_out, ffn_w_up, ffn_w_down):
    lw = {}
    d = w_in.shape[1]
    lw['norm_mix'] = norm_mix[l].reshape(1, d)
    lw['norm_ffn'] = norm_ffn[l].reshape(1, d)
    lw['wr'] = (w_in[l][:, :SHIFT_COLS].astype(BF16),)
    w_n = jnp.pad(w_in[l][:, SHIFT_COLS:], ((0, 0), (0, NSA_COLS_PAD - NSA_COLS)))
    lw['wn'] = (w_n.astype(BF16),)
    lw['mu'] = rwkv_mu[l].reshape(1, -1)
    lw['w0'] = rwkv_w0[l].reshape(1, -1)
    lw['a0'] = rwkv_a0[l].reshape(1, -1)
    lw['k_k'] = rwkv_k_k[l].reshape(1, -1)
    lw['k_a'] = rwkv_k_a[l].reshape(1, -1)
    lw['r_k'] = rwkv_r_k[l].reshape(1, -1)
    lw['ln_w'] = rwkv_ln_w[l].reshape(1, -1)
    lw['ln_b'] = rwkv_ln_b[l].reshape(1, -1)
    zeros = jnp.zeros_like(rwkv_w_up[l])
    wc = jnp.concatenate([jnp.concatenate([rwkv_w_up[l], zeros], axis=1),
                          jnp.concatenate([zeros, rwkv_a_up[l]], axis=1)], axis=0)
    lw['wc_h'], lw['wc_l'] = _hilo(wc)
    lw['gu_h'], lw['gu_l'] = _hilo(rwkv_g_up[l])
    qn = qk_norm[l]
    lw['nw'] = jnp.tile(qn, (1, 2))
    lw['nw_kc'] = jnp.tile(qn[1:2], (1, 2))
    w1 = cmp_w1[l].reshape(2, 2, CMP_STRIDE, HEAD_DIM, CMP_HIDDEN)
    eye = jnp.eye(2, dtype=F32)
    w1g = jnp.einsum('khldj,gf->klgdhfj', w1, eye)
    w1g = w1g.reshape(2, CMP_STRIDE * 2 * HEAD_DIM, 2 * 2 * CMP_HIDDEN)
    lw['c1'] = w1g.astype(BF16)
    lw['cmp_pos'] = cmp_pos[l].reshape(2, 1, CMP_BLOCK * HEAD_DIM)
    lw['cmp_w1'] = cmp_w1[l]
    lw['cmp_b1'] = cmp_b1[l].reshape(2, 1, CMP_HIDDEN)
    w2 = cmp_w2[l]
    lw['cmp_w2bd'] = jnp.einsum('kjd,gf->kgjfd', w2, eye).reshape(2, 2 * CMP_HIDDEN, 2 * HEAD_DIM)
    lw['cmp_b2t'] = jnp.tile(cmp_b2[l].reshape(2, 1, HEAD_DIM), (1, 1, 2))
    lw['wo1'] = w_out[l][:RWKV_WIDTH].astype(BF16)
    lw['wo2'] = w_out[l][RWKV_WIDTH:].astype(BF16)
    lw['wu'] = ffn_w_up[l].astype(BF16)
    lw['wd'] = ffn_w_down[l].astype(BF16)
    return lw


def _rope_tables(pos):
    half = HEAD_DIM // 2
    inv_freq = ROPE_THETA ** (-jnp.arange(half, dtype=F32) / half)
    ang = pos.astype(F32)[:, None] * inv_freq[None, :]
    cos = jnp.cos(ang)
    sin = jnp.sin(ang)
    return jnp.tile(cos, (1, 4)), jnp.tile(jnp.concatenate([-sin, sin], axis=1), (1, 2))


def _mods6(m):
    d = m.shape[-1] // 6
    return [m[..., i * d:(i + 1) * d] for i in range(6)]


def _prompt_layer(x, mods, lw, tables):
    bsz, t, d = x.shape
    m = bsz * t
    tm = min(512, t)
    tpb = t // tm
    sh1, sc1, g1, sh2, sc2, g2 = [a.reshape(bsz, 1, d) for a in _mods6(mods)]
    x2 = x.reshape(m, d)
    p_r = _in_proj(x2, lw['norm_mix'], sc1, sh1, lw['wr'], tm, tpb)
    p_n = _in_proj(x2, lw['norm_mix'], sc1, sh1, lw['wn'], tm, tpb)
    shift0 = jnp.zeros((bsz, 1, SHIFT_COLS), F32)
    r, ld, k, v, a, b, g = _rwkv_prep(p_r, shift0, lw, tm, tpb, True)
    seq = lambda z: z.reshape(bsz, t, RWKV_WIDTH)
    q1, y0, gm, ha = _rwkv_chunk(seq(r), seq(ld), seq(k), seq(v), seq(a), seq(b))
    y, h_fin = _rwkv_scan(q1, y0, gm, ha)
    y_r = _rwkv_post(y.reshape(m, RWKV_WIDTH), r, k, v, g, lw, tm)
    s_new = h_fin.reshape(bsz, HEAD_DIM, RWKV_HEADS, HEAD_DIM).transpose(0, 2, 3, 1)
    shift_new = p_r.reshape(bsz, t, SHIFT_COLS)[:, -1]
    q, qr, cmp_rows, sel_rows, win_rows, gt, sel_k, sel_v, win_b = _nsa_proj(
        p_n, tables[0], tables[1], lw['nw'], tm, tpb)
    seqw = lambda z: z.reshape(bsz, t, z.shape[-1])
    ab = _cmp1_prompt(seqw(cmp_rows), lw['c1'])
    ckv = _cmp2(ab, lw['cmp_c1'], lw)
    y_n = _attn_prompt(seqw(q), seqw(qr), seqw(gt), ckv, seqw(sel_k), seqw(sel_v), seqw(win_b))
    x2 = _out_proj(x2, y_r, y_n.reshape(m, NSA_WIDTH), g1, lw['wo1'], lw['wo2'], tm, tpb)
    tmf = min(1024, t)
    x2 = _ffn(x2, lw['norm_ffn'], sc2, sh2, g2, lw['wu'], lw['wd'], tmf, t // tmf)
    rows6 = lambda z: z.reshape(bsz, t, 2, 2, HEAD_DIM)
    wlen = min(WINDOW, t)
    return (x2.reshape(bsz, t, d), s_new, shift_new, rows6(cmp_rows), rows6(sel_rows), rows6(win_rows)[:, -wlen:])


def _sample_layer(x, mods, lw, tables, l, state_rwkv, state_shift, page_table, cache_cmp, cache_sel, cache_win,
                  n_pool, past_len):
    bs, d = x.shape
    tm = bs
    sh1, sc1, g1, sh2, sc2, g2 = [a.reshape(1, bs, d) for a in _mods6(mods)]
    p_r = _in_proj(x, lw['norm_mix'], sc1, sh1, lw['wr'], tm, 1)
    p_n = _in_proj(x, lw['norm_mix'], sc1, sh1, lw['wn'], tm, 1)
    r, ld, k, v, a, b, g = _rwkv_prep(p_r, state_shift[l], lw, tm, 1, False)
    s_new, y = _rwkv_step(state_rwkv, l, r, ld, k, a, b, v)
    y_r = _rwkv_post(y.reshape(bs, RWKV_WIDTH), r, k, v, g, lw, tm)
    q, qr, cmp_rows, sel_rows, win_rows, gt, _, _, _ = _nsa_proj(p_n, tables[0], tables[1], lw['nw'], tm, 1)
    ab = _cmp1_sample(page_table, cache_cmp, l * n_pool, lw['c1'])
    ckv = _cmp2(ab, lw['cmp_c1'], lw)
    grp = (jnp.arange(NSA_HEADS) // (NSA_HEADS // 2))[None, :, None]
    half = (jnp.arange(LANES) // HEAD_DIM)[None, None, :]

    def pad_heads(z):
        z = z.reshape(bs, NSA_HEADS, HEAD_DIM)
        return jnp.where(grp == half, jnp.tile(z, (1, 1, 2)), 0.0)

    gates = jnp.pad(gt[:, :3 * NSA_HEADS].reshape(bs, NSA_HEADS, 3), ((0, 0), (0, 0), (0, LANES - 3)))
    o, win_new = _attn_sample(page_table, pad_heads(q), pad_heads(qr), gates, ckv, cache_sel, l * n_pool,
                              sel_rows.reshape(bs, 1, 256), cache_win, l * bs, win_rows.reshape(bs, 1, 256), past_len)
    o = o.reshape(bs, NSA_HEADS, 2, HEAD_DIM)
    y_n = jnp.concatenate([o[:, :NSA_HEADS // 2, 0], o[:, NSA_HEADS // 2:, 1]], axis=1).reshape(bs, NSA_WIDTH)
    x = _out_proj(x, y_r, y_n, g1, lw['wo1'], lw['wo2'], tm, 1)
    x = _ffn(x, lw['norm_ffn'], sc2, sh2, g2, lw['wu'], lw['wd'], tm, 1)
    rows6 = lambda z: z.reshape(bs, 1, 2, 2, HEAD_DIM)
    return (x, s_new, p_r, rows6(cmp_rows), rows6(sel_rows), win_new.reshape(bs, -1, 2, 2, HEAD_DIM))


def kernel(x_prompt, x_sample, cache_cmp_kv, cache_sel_kv, cache_win_kv, state_rwkv, state_shift, page_table,
           c_prompt, c_sample, ada_w, ada_b, norm_mix, norm_ffn, w_in, rwkv_mu, rwkv_w0, rwkv_w_up, rwkv_a0,
           rwkv_a_up, rwkv_g_up, rwkv_k_k, rwkv_k_a, rwkv_r_k, rwkv_ln_w, rwkv_ln_b, qk_norm, cmp_pos, cmp_w1,
           cmp_b1, cmp_w2, cmp_b2, w_out, ffn_w_up, ffn_w_down):
    depth = ada_w.shape[0]
    bsz, t, d = x_prompt.shape
    bs = x_sample.shape[0]
    n_pool, page = cache_cmp_kv.shape[1], cache_cmp_kv.shape[2]
    n_pages = page_table.shape[1]
    past_len = n_pages * page
    wb = cache_win_kv.shape[2]
    bc = -(-(bsz + bs) // 8) * 8
    c_all = jnp.pad(jnp.concatenate([c_prompt, c_sample], axis=0), ((0, bc - bsz - bs), (0, 0)))
    mods = _ada_mods(c_all, ada_w, ada_b)
    tab_p = _rope_tables(jnp.arange(t, dtype=jnp.int32))
    tab_s = _rope_tables(jnp.full((bs,), past_len, jnp.int32))
    cache_cmp = cache_cmp_kv.reshape(depth * n_pool, page, 256)
    cache_sel = cache_sel_kv.reshape(depth * n_pool, page, 256)
    cache_win = cache_win_kv.reshape(depth * bs, wb, 256)
    st_rwkv = state_rwkv.reshape((depth * bs,) + state_rwkv.shape[2:])
    y_p = x_prompt
    y_s = x_sample.reshape(bs, d)
    outs_p = []
    outs_s = []
    for l in range(depth):
        lw = _layer_weights(l, norm_mix, norm_ffn, w_in, rwkv_mu, rwkv_w0, rwkv_w_up, rwkv_a0, rwkv_a_up,
                            rwkv_g_up, rwkv_k_k, rwkv_k_a, rwkv_r_k, rwkv_ln_w, rwkv_ln_b, qk_norm, cmp_pos,
                            cmp_w1, cmp_b1, cmp_w2, cmp_b2, w_out, ffn_w_up, ffn_w_down)
        lw['cmp_c1'] = _cmp_bias(lw)
        res_p = _prompt_layer(y_p, mods[l, :bsz], lw, tab_p)
        y_p = res_p[0]
        outs_p.append(res_p[1:])
        res_s = _sample_layer(y_s, mods[l, bsz:bsz + bs], lw, tab_s, l, st_rwkv, state_shift, page_table,
                              cache_cmp, cache_sel, cache_win, n_pool, past_len)
        y_s = res_s[0]
        outs_s.append(res_s[1:])
    stack = lambda outs, i: jnp.stack([o[i] for o in outs])
    return (y_p, y_s.reshape(bs, 1, d),
            stack(outs_p, 2), stack(outs_s, 2),
            stack(outs_p, 3), stack(outs_s, 3),
            stack(outs_p, 4), stack(outs_s, 4),
            stack(outs_p, 0), stack(outs_s, 0),
            stack(outs_p, 1), stack(outs_s, 1))
```

```python
import functools

import numpy as np
import jax
import jax.numpy as jnp
from jax import lax
from jax.experimental import pallas as pl
from jax.experimental.pallas import tpu as pltpu

F32 = jnp.float32
BF16 = jnp.bfloat16

HEAD_DIM = 64
RWKV_WIDTH = 512
NSA_WIDTH = 512
RWKV_HEADS = 8
NSA_HEADS = 8
KV_WIDTH = 128
SHIFT_COLS = 1792
NSA_COLS = 1304
NSA_COLS_PAD = 1408
CMP_BLOCK = 32
CMP_STRIDE = 16
CMP_HIDDEN = 128
SEL_BLOCK = 64
SEL_TOP_N = 16
WINDOW = 512
ROPE_THETA = 10000.0
NORM_EPS = 1e-6
GN_EPS = 64e-5
NEG_INF = -1e30
FORCE = 1e9
LANES = 128
RWKV_CHUNK = 64
Q_TILE = 128
VMEM_LIMIT = 56 * 1024 * 1024


def _cparams(sem):
    return pltpu.CompilerParams(dimension_semantics=sem, vmem_limit_bytes=VMEM_LIMIT)


def _split2(x):
    hi = x.astype(BF16)
    lo = (x - hi.astype(F32)).astype(BF16)
    return hi, lo


def _split3(x):
    hi = x.astype(BF16)
    r = x - hi.astype(F32)
    mid = r.astype(BF16)
    lo = (r - mid.astype(F32)).astype(BF16)
    return hi, mid, lo


def _dg(a, b, ca=1, cb=0):
    return lax.dot_general(a, b, (((ca,), (cb,)), ((), ())), preferred_element_type=F32)


def _mm(a, b, ca=1, cb=0):
    ah, al = _split2(a)
    bh, bl = _split2(b)
    return _dg(ah, bh, ca, cb) + _dg(ah, bl, ca, cb) + _dg(al, bh, ca, cb)


def _bf(x):
    return x.astype(BF16)


def _mm1(a, b, ca=1, cb=0):
    return _dg(_bf(a), _bf(b), ca, cb)


def _mm_w(ah, al, wh, wl):
    return _dg(ah, wh) + _dg(ah, wl) + _dg(al, wh)


def _mmp(a2, b2, ca=1, cb=0):
    return _dg(a2[0], b2[0], ca, cb) + _dg(a2[0], b2[1], ca, cb) + _dg(a2[1], b2[0], ca, cb)


def _mm_xr(a, e, ca=1, cb=0):
    a0, a1, a2 = _split3(a)
    return _dg(a0, e, ca, cb) + _dg(a1, e, ca, cb) + _dg(a2, e, ca, cb)


def _mm_xl(e, b, ca=1, cb=0):
    b0, b1, b2 = _split3(b)
    return _dg(e, b0, ca, cb) + _dg(e, b1, ca, cb) + _dg(e, b2, ca, cb)


def _iota(shape, dim):
    return lax.broadcasted_iota(jnp.int32, shape, dim)


def _head_blockdiag():
    return (_iota((LANES, LANES), 0) // HEAD_DIM == _iota((LANES, LANES), 1) // HEAD_DIM).astype(BF16)


def _segsum(x, bd):
    w = x.shape[1]
    outs = [_mm_xr(x[:, j * LANES:(j + 1) * LANES], bd) for j in range(w // LANES)]
    return outs[0] if len(outs) == 1 else jnp.concatenate(outs, axis=1)


def _sigmoid(x):
    return 1.0 / (1.0 + jnp.exp(-x))


def _norm_mod(x, g, sc, sh):
    ms = jnp.mean(x * x, axis=-1, keepdims=True)
    y = x * lax.rsqrt(ms + NORM_EPS) * g
    return y * (1.0 + sc) + sh


def _topk_mask(score, n, axis):
    size = score.shape[axis]
    pos = _iota(score.shape, axis).astype(F32)
    sel = jnp.zeros_like(score)
    sc = score
    for _ in range(n):
        m = jnp.max(sc, axis=axis, keepdims=True)
        idx = jnp.min(jnp.where(sc == m, pos, float(size)), axis=axis, keepdims=True)
        hit = pos == idx
        sel = jnp.where(hit & (m > 0.5 * NEG_INF), 1.0, sel)
        sc = jnp.where(hit, -3e38, sc)
    return sel


def _ada_kernel(c_ref, w_ref, b_ref, o_ref):
    c = c_ref[...]
    o_ref[0] = _mm(c * _sigmoid(c), w_ref[0]) + b_ref[0]


def _ada_mods(c_all, ada_w, ada_b):
    depth, d, n = ada_w.shape
    bc = c_all.shape[0]
    tn = 1024
    return pl.pallas_call(
        _ada_kernel,
        out_shape=jax.ShapeDtypeStruct((depth, bc, n), F32),
        grid=(depth, n // tn),
        in_specs=[pl.BlockSpec((bc, d), lambda l, j: (0, 0)),
                  pl.BlockSpec((1, d, tn), lambda l, j: (l, 0, j)),
                  pl.BlockSpec((1, 1, tn), lambda l, j: (l, 0, j))],
        out_specs=pl.BlockSpec((1, bc, tn), lambda l, j: (l, 0, j)),
        compiler_params=_cparams(("arbitrary", "arbitrary")),
        name="ada_mods",
    )(c_all, ada_w, ada_b.reshape(depth, 1, n))


def _in_proj_kernel(x_ref, g_ref, sc_ref, sh_ref, *refs):
    w_refs, o_ref = refs[:-1], refs[-1]
    h = _norm_mod(x_ref[...], g_ref[...], sc_ref[0], sh_ref[0])
    if len(w_refs) == 1:
        o_ref[...] = _dg(h.astype(BF16), w_refs[0][...])
    else:
        hh, hl = _split2(h)
        o_ref[...] = _mm_w(hh, hl, w_refs[0][...], w_refs[1][...])


def _in_proj(x, g, sc, sh, ws, tm, tpb):
    m, d = x.shape
    n = ws[0].shape[1]
    r = sc.shape[1]
    return pl.pallas_call(
        _in_proj_kernel,
        out_shape=jax.ShapeDtypeStruct((m, n), F32),
        grid=(m // tm,),
        in_specs=[pl.BlockSpec((tm, d), lambda i: (i, 0)),
                  pl.BlockSpec((1, d), lambda i: (0, 0)),
                  pl.BlockSpec((1, r, d), lambda i: (i // tpb, 0, 0)),
                  pl.BlockSpec((1, r, d), lambda i: (i // tpb, 0, 0))]
                 + [pl.BlockSpec((d, n), lambda i: (0, 0)) for _ in ws],
        out_specs=pl.BlockSpec((tm, n), lambda i: (i, 0)),
        compiler_params=_cparams(("arbitrary",)),
        name="in_proj",
    )(x, g, sc, sh, *ws)


def _out_proj_kernel(x_ref, a1_ref, a2_ref, gate_ref, w1, w2, o_ref):
    acc = _dg(a1_ref[...].astype(BF16), w1[...]) + _dg(a2_ref[...].astype(BF16), w2[...])
    o_ref[...] = x_ref[...] + gate_ref[0] * acc


def _out_proj(x, a1, a2, gate, w1, w2, tm, tpb):
    m, d = x.shape
    k1 = a1.shape[1]
    k2 = a2.shape[1]
    r = gate.shape[1]
    return pl.pallas_call(
        _out_proj_kernel,
        out_shape=jax.ShapeDtypeStruct((m, d), F32),
        grid=(m // tm,),
        in_specs=[pl.BlockSpec((tm, d), lambda i: (i, 0)),
                  pl.BlockSpec((tm, k1), lambda i: (i, 0)),
                  pl.BlockSpec((tm, k2), lambda i: (i, 0)),
                  pl.BlockSpec((1, r, d), lambda i: (i // tpb, 0, 0)),
                  pl.BlockSpec((k1, d), lambda i: (0, 0)),
                  pl.BlockSpec((k2, d), lambda i: (0, 0))],
        out_specs=pl.BlockSpec((tm, d), lambda i: (i, 0)),
        compiler_params=_cparams(("arbitrary",)),
        name="out_proj",
    )(x, a1, a2, gate, w1, w2)


def _ffn_kernel(x_ref, g_ref, sc_ref, sh_ref, gate_ref, wg, wu, wd, o_ref, h_s, acc_s):
    j = pl.program_id(1)

    @pl.when(j == 0)
    def _():
        h_s[...] = _norm_mod(x_ref[...], g_ref[...], sc_ref[0], sh_ref[0]).astype(BF16)
        acc_s[...] = jnp.zeros_like(acc_s)

    h = h_s[...]
    gate = _dg(h, wg[...])
    up = _dg(h, wu[...])
    act = gate * _sigmoid(gate) * up
    acc_s[...] += _dg(act.astype(BF16), wd[...])

    @pl.when(j == pl.num_programs(1) - 1)
    def _():
        o_ref[...] = x_ref[...] + gate_ref[0] * acc_s[...]


def _ffn(x, g, sc, sh, gate, w_up, w_down, tm, tpb):
    m, d = x.shape
    dff = w_down.shape[0]
    tf = 256
    nf = dff // tf
    r = sc.shape[1]
    mod = pl.BlockSpec((1, r, d), lambda i, j: (i // tpb, 0, 0))
    return pl.pallas_call(
        _ffn_kernel,
        out_shape=jax.ShapeDtypeStruct((m, d), F32),
        grid=(m // tm, nf),
        in_specs=[pl.BlockSpec((tm, d), lambda i, j: (i, 0)),
                  pl.BlockSpec((1, d), lambda i, j: (0, 0)),
                  mod, mod, mod,
                  pl.BlockSpec((d, tf), lambda i, j: (0, j)),
                  pl.BlockSpec((d, tf), lambda i, j: (0, j + nf)),
                  pl.BlockSpec((tf, d), lambda i, j: (j, 0))],
        out_specs=pl.BlockSpec((tm, d), lambda i, j: (i, 0)),
        scratch_shapes=[pltpu.VMEM((tm, d), BF16), pltpu.VMEM((tm, d), F32)],
        compiler_params=_cparams(("arbitrary", "arbitrary")),
        name="ffn",
    )(x, g, sc, sh, gate, w_up, w_up, w_down)


def _rwkv_prep_kernel(p_ref, prev_ref, mu_ref, wch, wcl, guh, gul, w0_ref, a0_ref, kk_ref, ka_ref,
                      r_o, ld_o, k_o, v_o, a_o, b_o, g_o, *scratch, seq_mode, tpb):
    p = p_ref[...]
    tm = p.shape[0]
    if seq_mode:
        carry = scratch[0]
        i = pl.program_id(0)

        @pl.when(i % tpb == 0)
        def _():
            carry[...] = prev_ref[0]

        rolled = pltpu.roll(p, 1, 0)
        prev = jnp.where(_iota((tm, 1), 0) == 0, carry[...], rolled)
        carry[...] = p[tm - 1:tm, :]
    else:
        prev = prev_ref[...]
    xs = p + mu_ref[...] * (prev - p)
    w_ = RWKV_WIDTH
    r = xs[:, 0:w_]
    k = xs[:, w_:2 * w_]
    v = xs[:, 2 * w_:3 * w_]
    z = xs[:, 3 * w_:3 * w_ + LANES]
    gd = xs[:, 3 * w_ + LANES:3 * w_ + 2 * LANES]
    z = jnp.where(_iota((tm, LANES), 1) < 64, jnp.tanh(z), z)
    zh, zl = _split2(z)
    wa = _mm_w(zh, zl, wch[...], wcl[...])
    zw = -(w0_ref[...] + wa[:, 0:w_])
    softplus = jnp.maximum(zw, 0.0) + jnp.log1p(jnp.exp(-jnp.abs(zw)))
    wlog = -softplus - 0.5
    a_sig = _sigmoid(a0_ref[...] + wa[:, w_:2 * w_])
    sh_, sl_ = _split2(_sigmoid(gd))
    g = _mm_w(sh_, sl_, guh[...], gul[...])
    bd = _head_blockdiag()
    kkv = k * kk_ref[...]
    nrm = jnp.sqrt(_segsum(kkv * kkv, bd))
    kkn = kkv / jnp.maximum(nrm, 1e-12)
    r_o[...] = r
    ld_o[...] = -jnp.exp(wlog)
    k_o[...] = k * (1.0 + (a_sig - 1.0) * ka_ref[...])
    v_o[...] = v
    a_o[...] = -kkn
    b_o[...] = kkn * a_sig
    g_o[...] = g


def _rwkv_prep(p_r, prev, lw, tm, tpb, seq_mode):
    m = p_r.shape[0]
    w_ = RWKV_WIDTH
    row = lambda n: pl.BlockSpec((1, n), lambda i: (0, 0))
    full = lambda a: pl.BlockSpec(a.shape, lambda i: (0, 0))
    if seq_mode:
        prev_spec = pl.BlockSpec((1, 1, SHIFT_COLS), lambda i: (i // tpb, 0, 0))
        scratch = [pltpu.VMEM((1, SHIFT_COLS), F32)]
    else:
        prev_spec = pl.BlockSpec((tm, SHIFT_COLS), lambda i: (i, 0))
        scratch = []
    out = jax.ShapeDtypeStruct((m, w_), F32)
    ospec = pl.BlockSpec((tm, w_), lambda i: (i, 0))
    return pl.pallas_call(
        functools.partial(_rwkv_prep_kernel, seq_mode=seq_mode, tpb=tpb),
        out_shape=[out] * 7,
        grid=(m // tm,),
        in_specs=[pl.BlockSpec((tm, SHIFT_COLS), lambda i: (i, 0)), prev_spec, row(SHIFT_COLS),
                  full(lw['wc_h']), full(lw['wc_l']), full(lw['gu_h']), full(lw['gu_l']),
                  row(w_), row(w_), row(w_), row(w_)],
        out_specs=[ospec] * 7,
        scratch_shapes=scratch,
        compiler_params=_cparams(("arbitrary",)),
        name="rwkv_prep",
    )(p_r, prev, lw['mu'], lw['wc_h'], lw['wc_l'], lw['gu_h'], lw['gu_l'], lw['w0'], lw['a0'], lw['k_k'], lw['k_a'])


def _rwkv_chunk_kernel(r_ref, ld_ref, k_ref, v_ref, a_ref, b_ref, q1_o, y0_o, g_o, ha_o):
    c = RWKV_CHUNK
    c2 = 2 * c
    ld = ld_ref[0]
    lc = _mm_xl((_iota((c, c), 0) >= _iota((c, c), 1)).astype(BF16), ld)
    lx = lc - ld
    lend = lc[c - 1:c, :]
    p_in = jnp.exp(lc)
    inv_p = jnp.exp(-lc)
    p_to_end = jnp.exp(lend - lc)
    p_end = jnp.exp(lend)
    r = r_ref[0]
    k = k_ref[0]
    v = v_ref[0]
    b = b_ref[0]
    rt = r * p_in
    at = a_ref[0] * jnp.exp(lx)
    bt = b * inv_p
    kt = k * inv_p
    bh = b * p_to_end
    kh = k * p_to_end
    row = _iota((c2, c2), 0)
    col = _iota((c2, c2), 1)
    strict = row % c > col % c
    incl = row % c >= col % c
    eye = row == col
    lo = _iota((c, LANES), 1) < HEAD_DIM

    def pair_rows(x):
        return jnp.concatenate([jnp.where(lo, x, 0.0), jnp.where(lo, 0.0, x)], axis=0)

    pairs = range(RWKV_WIDTH // LANES)
    sls = [slice(LANES * j, LANES * (j + 1)) for j in pairs]
    at2 = [pair_rows(at[:, sl]) for sl in sls]
    rt2 = [pair_rows(rt[:, sl]) for sl in sls]
    v2s = [_bf(pair_rows(v[:, sl])) for sl in sls]
    lhs = [_bf(jnp.concatenate([at2[j], rt2[j]], axis=0)) for j in pairs]
    xb = [_dg(lhs[j], _bf(pair_rows(bt[:, sls[j]])), 1, 1) for j in pairs]
    xk = [_dg(lhs[j], _bf(pair_rows(kt[:, sls[j]])), 1, 1) for j in pairs]
    a_ab = [jnp.where(strict, xb[j][0:c2], 0.0) for j in pairs]
    a_rb = [_bf(jnp.where(incl, xb[j][c2:2 * c2], 0.0)) for j in pairs]
    a_ak = [_bf(jnp.where(strict, xk[j][0:c2], 0.0)) for j in pairs]
    a_rk = [_bf(jnp.where(incl, xk[j][c2:2 * c2], 0.0)) for j in pairs]
    akv = [_dg(a_ak[j], v2s[j]) for j in pairs]
    x = [jnp.where(eye, 1.0, 0.0) + a_ab[j] for j in pairs]
    mp = a_ab
    for _ in range(int(np.log2(c)) - 1):
        mps = [_bf(mp[j]) for j in pairs]
        mp = [_dg(mps[j], mps[j]) for j in pairs]
        x = [x[j] + _dg(_bf(mp[j]), _bf(x[j])) for j in pairs]
    wu = [_bf(_dg(_bf(x[j]), _bf(jnp.concatenate([at2[j], akv[j]], axis=1)))) for j in pairs]
    qy = [_dg(a_rb[j], wu[j]) for j in pairs]
    ark_v = [_dg(a_rk[j], v2s[j]) for j in pairs]
    gh = [_dg(_bf(pair_rows(bh[:, sls[j]])), wu[j], 0, 0) for j in pairs]
    khv = [_dg(_bf(pair_rows(kh[:, sls[j]])), v2s[j], 0, 0) for j in pairs]
    for j in pairs:
        sl = sls[j]
        q1 = rt2[j] + qy[j][:, 0:LANES]
        y0 = qy[j][:, LANES:2 * LANES] + ark_v[j]
        q1_o[0, :, sl] = q1[0:c] + q1[c:c2]
        y0_o[0, :, sl] = y0[0:c] + y0[c:c2]
        g_o[0, 0, :, sl] = jnp.where(eye, p_end[:, sl], 0.0) + gh[j][:, 0:LANES]
        ha_o[0, 0, :, sl] = gh[j][:, LANES:2 * LANES] + khv[j]


def _rwkv_chunk(r, ld, k, v, a, b):
    bsz, t, w_ = r.shape
    c = RWKV_CHUNK
    nch = t // c
    ispec = pl.BlockSpec((1, c, w_), lambda i, j: (i, j, 0))
    sspec = pl.BlockSpec((1, 1, 2 * HEAD_DIM, w_), lambda i, j: (i, j, 0, 0))
    return pl.pallas_call(
        _rwkv_chunk_kernel,
        out_shape=[jax.ShapeDtypeStruct((bsz, t, w_), F32)] * 2
                  + [jax.ShapeDtypeStruct((bsz, nch, 2 * HEAD_DIM, w_), F32)] * 2,
        grid=(bsz, nch),
        in_specs=[ispec] * 6,
        out_specs=[ispec, ispec, sspec, sspec],
        compiler_params=_cparams(("arbitrary", "arbitrary")),
        name="rwkv_chunk",
    )(r, ld, k, v, a, b)


def _rwkv_scan_kernel(q1_ref, y0_ref, g_ref, ha_ref, y_o, hf_o, h_s):
    j = pl.program_id(0)

    @pl.when(j == 0)
    def _():
        h_s[...] = jnp.zeros_like(h_s)

    src = j % 2
    dst = (j + 1) % 2
    for bi in range(q1_ref.shape[0]):
        for p in range(RWKV_WIDTH // LANES):
            sl = slice(LANES * p, LANES * (p + 1))
            h0 = _split2(h_s[src, bi, :, sl])
            y_o[bi, :, sl] = _mmp(_split2(q1_ref[bi, :, sl]), h0) + y0_ref[bi, :, sl]
            h_s[dst, bi, :, sl] = _mmp(_split2(g_ref[bi, 0, :, sl]), h0) + ha_ref[bi, 0, :, sl]

    @pl.when(j == pl.num_programs(0) - 1)
    def _():
        hf_o[...] = h_s[dst, :, 0:HEAD_DIM, :] + h_s[dst, :, HEAD_DIM:2 * HEAD_DIM, :]


def _rwkv_scan(q1, y0, g, ha):
    bsz, t, w_ = q1.shape
    c = RWKV_CHUNK
    ispec = pl.BlockSpec((bsz, c, w_), lambda j: (0, j, 0))
    sspec = pl.BlockSpec((bsz, 1, 2 * HEAD_DIM, w_), lambda j: (0, j, 0, 0))
    return pl.pallas_call(
        _rwkv_scan_kernel,
        out_shape=[jax.ShapeDtypeStruct((bsz, t, w_), F32), jax.ShapeDtypeStruct((bsz, HEAD_DIM, w_), F32)],
        grid=(t // c,),
        in_specs=[ispec, ispec, sspec, sspec],
        out_specs=[ispec, pl.BlockSpec((bsz, HEAD_DIM, w_), lambda j: (0, 0, 0))],
        scratch_shapes=[pltpu.VMEM((2, bsz, 2 * HEAD_DIM, w_), F32)],
        compiler_params=_cparams(("arbitrary",)),
        name="rwkv_scan",
    )(q1, y0, g, ha)


def _rwkv_step_kernel(s_ref, r_ref, ld_ref, k_ref, a_ref, b_ref, v_ref, so_ref, y_ref):
    s = s_ref[...]
    sa = jnp.sum(s * a_ref[...], axis=-1, keepdims=True)
    s_new = s * jnp.exp(ld_ref[...]) + sa * b_ref[...] + v_ref[...] * k_ref[...]
    so_ref[...] = s_new
    y_ref[...] = jnp.sum(s_new * r_ref[...], axis=-1, keepdims=True)


def _rwkv_step(state, layer, r, ld, k, a, b, v):
    bs = r.shape[0]
    nh, hd = RWKV_HEADS, HEAD_DIM
    bt = min(8, bs)
    rowv = lambda x: x.reshape(bs, nh, 1, hd)
    vspec = pl.BlockSpec((bt, nh, 1, hd), lambda i: (i, 0, 0, 0))
    cspec = pl.BlockSpec((bt, nh, hd, 1), lambda i: (i, 0, 0, 0))
    off = layer * (bs // bt)
    return pl.pallas_call(
        _rwkv_step_kernel,
        out_shape=[jax.ShapeDtypeStruct((bs, nh, hd, hd), F32), jax.ShapeDtypeStruct((bs, nh, hd, 1), F32)],
        grid=(bs // bt,),
        in_specs=[pl.BlockSpec((bt, nh, hd, hd), lambda i: (i + off, 0, 0, 0)),
                  vspec, vspec, vspec, vspec, vspec, cspec],
        out_specs=[pl.BlockSpec((bt, nh, hd, hd), lambda i: (i, 0, 0, 0)), cspec],
        compiler_params=_cparams(("arbitrary",)),
        name="rwkv_step",
    )(state, rowv(r), rowv(ld), rowv(k), rowv(a), rowv(b), v.reshape(bs, nh, hd, 1))


def _rwkv_post_kernel(y_ref, r_ref, k_ref, v_ref, g_ref, rk_ref, lw_ref, lb_ref, o_ref):
    bd = _head_blockdiag()
    y = y_ref[...]
    inv = 1.0 / HEAD_DIM
    mu = _segsum(y, bd) * inv
    d = y - mu
    var = _segsum(d * d, bd) * inv
    yn = d * lax.rsqrt(var + GN_EPS) * lw_ref[...] + lb_ref[...]
    v = v_ref[...]
    bonus = _segsum(r_ref[...] * k_ref[...] * rk_ref[...], bd) * v
    o_ref[...] = (yn + bonus) * g_ref[...]


def _rwkv_post(y, r, k, v, g, lw, tm):
    m, w_ = y.shape
    tile = pl.BlockSpec((tm, w_), lambda i: (i, 0))
    row = pl.BlockSpec((1, w_), lambda i: (0, 0))
    return pl.pallas_call(
        _rwkv_post_kernel,
        out_shape=jax.ShapeDtypeStruct((m, w_), F32),
        grid=(m // tm,),
        in_specs=[tile] * 5 + [row] * 3,
        out_specs=tile,
        compiler_params=_cparams(("arbitrary",)),
        name="rwkv_post",
    )(y, r, k, v, g, lw['r_k'], lw['ln_w'], lw['ln_b'])


def _nsa_proj_kernel(p_ref, cos_ref, sin_ref, nw_ref, q_o, qr_o, cmp_o, sel_o, win_o, gt_o, selk_o, selv_o, winb_o):
    tm = p_ref.shape[0]
    cos = cos_ref[...]
    sin = sin_ref[...]
    bd = _head_blockdiag()
    first = (_iota((tm, LANES), 1) % HEAD_DIM) < HEAD_DIM // 2
    nw = nw_ref[...]

    def norm(x, w):
        ms = _segsum(x * x, bd) * (1.0 / HEAD_DIM)
        return x * lax.rsqrt(ms + NORM_EPS) * w

    def rope(x):
        rot = jnp.where(first, pltpu.roll(x, LANES - HEAD_DIM // 2, 1), pltpu.roll(x, HEAD_DIM // 2, 1))
        return x * cos + rot * sin

    for j in range(NSA_WIDTH // LANES):
        sl = slice(LANES * j, LANES * (j + 1))
        xn = norm(p_ref[:, sl], nw[0:1])
        q_o[:, sl] = xn
        qr_o[:, sl] = rope(xn)
    o = NSA_WIDTH
    cmp_o[...] = p_ref[:, o:o + 2 * KV_WIDTH]
    ks = rope(norm(p_ref[:, o + 256:o + 384], nw[2:3]))
    vs = p_ref[:, o + 384:o + 512]
    kw = rope(norm(p_ref[:, o + 512:o + 640], nw[3:4]))
    vw = p_ref[:, o + 640:o + 768]
    sel_o[:, 0:LANES] = ks
    sel_o[:, LANES:2 * LANES] = vs
    win_o[:, 0:LANES] = kw
    win_o[:, LANES:2 * LANES] = vw
    gt_o[...] = _sigmoid(p_ref[:, o + 768:o + 896])
    lo = _iota((tm, LANES), 1) < HEAD_DIM
    selk_o[...] = ks.astype(BF16)
    selv_o[:, 0:LANES] = jnp.where(lo, vs, 1.0).astype(BF16)
    selv_o[:, LANES:2 * LANES] = jnp.where(lo, 1.0, vs).astype(BF16)
    winb_o[:, 0:LANES] = kw.astype(BF16)
    winb_o[:, LANES:2 * LANES] = vw.astype(BF16)


def _nsa_proj(p_n, cos_t, sin_t, nw, tm, tpb):
    m = p_n.shape[0]
    tt = cos_t.shape[0] // tm
    tab = pl.BlockSpec((tm, LANES), lambda i: (i % tt, 0))
    shapes = [(m, NSA_WIDTH), (m, NSA_WIDTH), (m, 256), (m, 256), (m, 256), (m, LANES),
              (m, LANES), (m, 256), (m, 256)]
    dtypes = [F32] * 6 + [BF16] * 3
    return pl.pallas_call(
        _nsa_proj_kernel,
        out_shape=[jax.ShapeDtypeStruct(s, dt) for s, dt in zip(shapes, dtypes)],
        grid=(m // tm,),
        in_specs=[pl.BlockSpec((tm, NSA_COLS_PAD), lambda i: (i, 0)), tab, tab,
                  pl.BlockSpec((4, LANES), lambda i: (0, 0))],
        out_specs=[pl.BlockSpec((tm, s[1]), lambda i: (i, 0)) for s in shapes],
        compiler_params=_cparams(("arbitrary",)),
        name="nsa_proj",
    )(p_n, cos_t, sin_t, nw)


def _cmp1_kernel(*refs, n_parts, paged):
    if paged:
        refs = refs[1:]
    parts = refs[:n_parts]
    w_ref, o_ref, xs = refs[n_parts:]
    gpp = parts[0].shape[1] // CMP_STRIDE
    for j, pr in enumerate(parts):
        for l in range(CMP_STRIDE):
            xs[gpp * j:gpp * (j + 1), LANES * l:LANES * (l + 1)] = pr[0, pl.ds(l, gpp, stride=CMP_STRIDE), :]
    res = _dg(_bf(xs[...]), w_ref[0])
    ng = o_ref.shape[2]
    for s in range(o_ref.shape[0]):
        o_ref[s, 0] = res[ng * s:ng * (s + 1)]


def _cmp1_prompt(cmp_rows, w):
    bsz, t, _ = cmp_rows.shape
    rows = min(2048, t)
    ng = rows // CMP_STRIDE
    kdim = CMP_STRIDE * LANES
    return pl.pallas_call(
        functools.partial(_cmp1_kernel, n_parts=1, paged=False),
        out_shape=jax.ShapeDtypeStruct((bsz, 2, t // CMP_STRIDE, 512), F32),
        grid=(bsz, t // rows, 2),
        in_specs=[pl.BlockSpec((1, rows, LANES), lambda b, i, kv: (b, i, kv)),
                  pl.BlockSpec((1, kdim, 512), lambda b, i, kv: (kv, 0, 0))],
        out_specs=pl.BlockSpec((1, 1, ng, 512), lambda b, i, kv: (b, kv, i, 0)),
        scratch_shapes=[pltpu.VMEM((ng, kdim), F32)],
        compiler_params=_cparams(("arbitrary", "arbitrary", "arbitrary")),
        name="cmp1_prompt",
    )(cmp_rows, w)


def _cmp1_sample(page_table, cache, page_off, w):
    bs, n_pages = page_table.shape
    page = cache.shape[1]
    ng = n_pages * page // CMP_STRIDE
    kdim = CMP_STRIDE * LANES
    sps = 4 if bs % 4 == 0 else 1

    def page_spec(s, j):
        return pl.BlockSpec((1, page, LANES), lambda b, kv, pt: (pt[b * sps + s, j] + page_off, 0, kv))

    return pl.pallas_call(
        functools.partial(_cmp1_kernel, n_parts=sps * n_pages, paged=True),
        out_shape=jax.ShapeDtypeStruct((bs, 2, ng, 512), F32),
        grid_spec=pltpu.PrefetchScalarGridSpec(
            num_scalar_prefetch=1,
            grid=(bs // sps, 2),
            in_specs=[page_spec(s, j) for s in range(sps) for j in range(n_pages)] + [
                pl.BlockSpec((1, kdim, 512), lambda b, kv, pt: (kv, 0, 0))],
            out_specs=pl.BlockSpec((sps, 1, ng, 512), lambda b, kv, pt: (b, kv, 0, 0)),
            scratch_shapes=[pltpu.VMEM((sps * ng, kdim), F32)]),
        compiler_params=_cparams(("arbitrary", "arbitrary")),
        name="cmp1_sample",
    )(page_table, *([cache] * (sps * n_pages)), w)


def _cmp_bias_kernel(pos_ref, w1_ref, b1_ref, o_ref):
    posb = jnp.broadcast_to(pos_ref[0], (8, pos_ref.shape[2]))
    o_ref[0] = _mm(posb, w1_ref[0])[0:1] + b1_ref[0]


def _cmp_bias(lw):
    kdim = CMP_BLOCK * HEAD_DIM
    return pl.pallas_call(
        _cmp_bias_kernel,
        out_shape=jax.ShapeDtypeStruct((2, 1, CMP_HIDDEN), F32),
        grid=(2,),
        in_specs=[pl.BlockSpec((1, 1, kdim), lambda kv: (kv, 0, 0)),
                  pl.BlockSpec((1, kdim, CMP_HIDDEN), lambda kv: (kv, 0, 0)),
                  pl.BlockSpec((1, 1, CMP_HIDDEN), lambda kv: (kv, 0, 0))],
        out_specs=pl.BlockSpec((1, 1, CMP_HIDDEN), lambda kv: (kv, 0, 0)),
        compiler_params=_cparams(("arbitrary",)),
        name="cmp_bias",
    )(lw['cmp_pos'], lw['cmp_w1'], lw['cmp_b1'])


def _cmp2_kernel(ab_ref, c1_ref, w2_ref, b2_ref, nw_ref, o_ref):
    kv = pl.program_id(1)
    spb, _, ng, width = ab_ref.shape
    ab = ab_ref[:, 0].reshape(spb * ng, width)
    a_part = ab[:, 0:256]
    b_next = pltpu.roll(ab[:, 256:512], spb * ng - 1, 0)
    c1 = c1_ref[0]
    h = a_part + b_next + jnp.concatenate([c1, c1], axis=1)
    h = 0.5 * h * (1.0 + jnp.tanh(np.sqrt(2.0 / np.pi) * (h + 0.044715 * (h * h * h))))
    out = _mm1(h, w2_ref[0]) + b2_ref[0]
    ms = _segsum(out * out, _head_blockdiag()) * (1.0 / HEAD_DIM)
    normed = out * lax.rsqrt(ms + NORM_EPS) * nw_ref[...]
    o_ref[:, 0] = jnp.where(kv == 0, normed, out).reshape(spb, ng, LANES)


def _cmp2(ab, c1, lw):
    bsz, _, ng, _ = ab.shape
    spb = 8 if bsz % 8 == 0 else 1
    return pl.pallas_call(
        _cmp2_kernel,
        out_shape=jax.ShapeDtypeStruct((bsz, 2, ng, LANES), F32),
        grid=(bsz // spb, 2),
        in_specs=[pl.BlockSpec((spb, 1, ng, 512), lambda b, kv: (b, kv, 0, 0)),
                  pl.BlockSpec((1, 1, CMP_HIDDEN), lambda b, kv: (kv, 0, 0)),
                  pl.BlockSpec((1, 2 * CMP_HIDDEN, LANES), lambda b, kv: (kv, 0, 0)),
                  pl.BlockSpec((1, 1, LANES), lambda b, kv: (kv, 0, 0)),
                  pl.BlockSpec((1, LANES), lambda b, kv: (0, 0))],
        out_specs=pl.BlockSpec((spb, 1, ng, LANES), lambda b, kv: (b, kv, 0, 0)),
        compiler_params=_cparams(("arbitrary", "arbitrary")),
        name="cmp2",
    )(ab, c1, lw['cmp_w2bd'], lw['cmp_b2t'], lw['nw_kc'])


def _masked_exp(s, mask):
    s = jnp.where(mask, s, NEG_INF)
    mx = jnp.maximum(jnp.max(s, axis=-1, keepdims=True), 0.1 * NEG_INF)
    e = jnp.exp(s - mx)
    den = jnp.sum(e, axis=-1, keepdims=True)
    return e, 1.0 / jnp.maximum(den, 1e-30)


def _masked_softmax(s, mask):
    e, rden = _masked_exp(s, mask)
    return e * rden


def _selection_mask(imp, qpos, axis):
    blk = _iota(imp.shape, axis)
    cur = qpos // SEL_BLOCK
    valid = blk * SEL_BLOCK <= qpos
    forced = (blk == 0) | (blk == cur) | (blk == cur - 1)
    score = jnp.where(valid, jnp.where(forced, FORCE, imp), NEG_INF)
    return _topk_mask(score, SEL_TOP_N, axis)


def _attn_prompt_kernel(q_ref, qr_ref, gt_ref, kc_ref, vc_ref, ks_ref, vs0_ref, vs1_ref, *rest, ck):
    nwb = WINDOW // Q_TILE + 1
    kw_refs = rest[0:nwb]
    vw_refs = rest[nwb:2 * nwb]
    ovt_ref, en_ref, o_ref, qg_s, qrg_s, m_s, acc_s = rest[2 * nwb:]
    qi = pl.program_id(1)
    tq = Q_TILE
    hpg = NSA_HEADS // 2
    scale = HEAD_DIM ** -0.5
    ng = kc_ref.shape[2]
    lo_lanes = _iota((tq, LANES), 1) < HEAD_DIM
    qpos = qi * tq + _iota((tq, 1), 0)
    qpos_t = qi * tq + _iota((1, tq), 1)
    qpos_all = jnp.concatenate([qpos] * NSA_HEADS, axis=0)
    gt = gt_ref[0]
    kc = kc_ref[0, 0]
    vc = vc_ref[0, 0]
    kw = jnp.concatenate([r[0] for r in kw_refs], axis=0)
    vw = jnp.concatenate([r[0] for r in vw_refs], axis=0)
    kwpos = jnp.concatenate([(qi - (nwb - 1) + j) * tq + _iota((1, tq), 1) for j in range(nwb)], axis=1)
    nch = (qi * tq + tq + ck - 1) // ck
    rows = hpg * tq
    for h in range(NSA_HEADS):
        g = h // hpg
        cb = h // 2
        rs = slice(tq * h, tq * (h + 1))
        keep = lo_lanes if g == 0 else jnp.logical_not(lo_lanes)
        blk = q_ref[0, :, LANES * cb:LANES * (cb + 1)]
        blkr = qr_ref[0, :, LANES * cb:LANES * (cb + 1)]
        if (h % 2 == 0) != (g == 0):
            blk = pltpu.roll(blk, HEAD_DIM, 1)
            blkr = pltpu.roll(blkr, HEAD_DIM, 1)
        qg_s[rs, :] = (jnp.where(keep, blk, 0.0) * scale).astype(BF16)
        qrg_s[rs, :] = (jnp.where(keep, blkr, 0.0) * scale).astype(BF16)
    qrg = qrg_s[...]
    kcb = _bf(kc)
    s = _dg(qg_s[...], kcb, 1, 1)
    mc = (_iota((2 * rows, ng), 1) * CMP_STRIDE + (CMP_BLOCK - 1)) <= qpos_all
    p = _masked_softmax(s, mc)
    oc = _mm1(p, vc)
    nselb = []
    for g in range(2):
        b0 = rows * g
        psum = p[b0:b0 + tq] + p[b0 + tq:b0 + 2 * tq] + p[b0 + 2 * tq:b0 + 3 * tq] + p[b0 + 3 * tq:b0 + 4 * tq]
        imp_t = _mm_xl(ovt_ref[...], psum.T)
        sel_t = _selection_mask(imp_t, qpos_t, 0)
        nselb.append((1.0 - sel_t).T.astype(BF16))
    m_s[...] = jnp.full(m_s.shape, 0.1 * NEG_INF, F32)
    acc_s[...] = jnp.zeros(acc_s.shape, F32)

    def chunk(c, diagonal):
        st = pl.multiple_of(c * ck, ck)
        sc = _dg(qrg, ks_ref[0, pl.ds(st, ck), :], 1, 1)
        causal = jnp.where(st + _iota((tq, ck), 1) <= qpos, 0.0, NEG_INF) if diagonal else None
        for g in range(2):
            vs_ref = vs0_ref if g == 0 else vs1_ref
            vch = vs_ref[0, pl.ds(st, ck), :]
            bias = _dg(nselb[g], en_ref[c])
            if diagonal:
                bias = bias + causal
            for hh in range(hpg):
                rs = slice(rows * g + tq * hh, rows * g + tq * (hh + 1))
                sh = sc[rs] + bias
                m_old = m_s[rs, :]
                m_new = jnp.maximum(m_old, jnp.max(sh, axis=-1, keepdims=True))
                alpha = jnp.exp(m_old - m_new)
                pe = jnp.exp(sh - m_new)
                acc_s[rs, :] = alpha * acc_s[rs, :] + _dg(pe.astype(BF16), vch)
                m_s[rs, :] = m_new

    def body(c, carry):
        chunk(c, False)
        return carry

    lax.fori_loop(0, nch - 1, body, 0)
    chunk(nch - 1, True)
    acc = acc_s[...]
    osel = acc / pltpu.roll(acc, HEAD_DIM, 1)
    sw = _dg(qrg, kw, 1, 1)
    mw = (kwpos <= qpos_all) & (kwpos > qpos_all - WINDOW) & (kwpos >= 0)
    ew, rw = _masked_exp(sw, mw)
    ow = _dg(ew.astype(BF16), vw) * rw
    for h in range(NSA_HEADS):
        g = h // hpg
        rs = slice(tq * h, tq * (h + 1))
        og = (gt[:, 3 * h:3 * h + 1] * oc[rs] + gt[:, 3 * h + 1:3 * h + 2] * osel[rs]
              + gt[:, 3 * h + 2:3 * h + 3] * ow[rs])
        o_ref[0, :, HEAD_DIM * h:HEAD_DIM * (h + 1)] = og[:, HEAD_DIM * g:HEAD_DIM * (g + 1)]


def _overlap_matrix(ng):
    s = np.arange(ng)[:, None] * CMP_STRIDE
    j = np.arange(LANES)[None, :] * SEL_BLOCK
    return ((s < j + SEL_BLOCK) & (s + CMP_BLOCK > j)).astype(np.float32)


def _block_bias_matrix(t, ck):
    key_blk = (np.arange(t) // SEL_BLOCK).reshape(t // ck, 1, ck)
    blk = np.arange(LANES).reshape(1, LANES, 1)
    return jnp.asarray(np.where(key_blk == blk, NEG_INF, 0.0).astype(np.float32), dtype=BF16)


def _attn_prompt(q, qr, gt, ckv, selk, selv, win):
    bsz, t, _ = q.shape
    tq = Q_TILE
    ng = ckv.shape[2]
    ck = min(1024, t)
    nwb = WINDOW // tq + 1
    hpg = NSA_HEADS // 2
    assert t // SEL_BLOCK <= LANES and t % ck == 0
    ovt = jnp.asarray(_overlap_matrix(ng).T, dtype=BF16)

    def wspec(j, col):
        return pl.BlockSpec((1, tq, LANES), lambda b, i: (b, jnp.maximum(i - (nwb - 1) + j, 0), col))

    tile = lambda w: pl.BlockSpec((1, tq, w), lambda b, i: (b, i, 0))
    return pl.pallas_call(
        functools.partial(_attn_prompt_kernel, ck=ck),
        out_shape=jax.ShapeDtypeStruct((bsz, t, NSA_WIDTH), F32),
        grid=(bsz, t // tq),
        in_specs=[tile(NSA_WIDTH), tile(NSA_WIDTH), tile(LANES),
                  pl.BlockSpec((1, 1, ng, LANES), lambda b, i: (b, 0, 0, 0)),
                  pl.BlockSpec((1, 1, ng, LANES), lambda b, i: (b, 1, 0, 0)),
                  pl.BlockSpec((1, t, LANES), lambda b, i: (b, 0, 0)),
                  pl.BlockSpec((1, t, LANES), lambda b, i: (b, 0, 0)),
                  pl.BlockSpec((1, t, LANES), lambda b, i: (b, 0, 1))]
                 + [wspec(j, 0) for j in range(nwb)] + [wspec(j, 1) for j in range(nwb)]
                 + [pl.BlockSpec((LANES, ng), lambda b, i: (0, 0)),
                    pl.BlockSpec((t // ck, LANES, ck), lambda b, i: (0, 0, 0))],
        out_specs=tile(NSA_WIDTH),
        scratch_shapes=[pltpu.VMEM((NSA_HEADS * tq, LANES), BF16), pltpu.VMEM((NSA_HEADS * tq, LANES), BF16),
                        pltpu.VMEM((NSA_HEADS * tq, 1), F32), pltpu.VMEM((NSA_HEADS * tq, LANES), F32)],
        compiler_params=_cparams(("arbitrary", "arbitrary")),
        name="attn_prompt",
    )(q, qr, gt, ckv, ckv, selk, selv, selv, *([win] * (2 * nwb)), ovt, _block_bias_matrix(t, ck))


def _attn_sample_kernel(pt_ref, q_ref, qr_ref, gt_ref, ckv_ref, *rest, past_len, n_pages):
    pages = rest[:n_pages]
    nsel_ref, win_ref, nwin_ref, ov_ref, o_ref, wout_ref = rest[n_pages:]
    nh = NSA_HEADS
    scale = HEAD_DIM ** -0.5
    qpos = past_len
    q = q_ref[0]
    qr = qr_ref[0]
    gt = gt_ref[0]
    kc = ckv_ref[0, 0]
    vc = ckv_ref[0, 1]
    ng = kc.shape[0]
    s = _mm1(q, kc, 1, 1) * scale
    mc = (_iota((nh, ng), 1) * CMP_STRIDE + (CMP_BLOCK - 1)) <= qpos
    p = _masked_softmax(s, mc)
    oc = _mm1(p, vc)
    same_group = (_iota((nh, nh), 0) // (nh // 2) == _iota((nh, nh), 1) // (nh // 2)).astype(BF16)
    imp = _mm_xr(_mm_xl(same_group, p), ov_ref[...])
    selb = _selection_mask(imp, qpos, 1).astype(BF16)
    page = pages[0].shape[1]
    nsel = nsel_ref[0]
    qrb = qr.astype(BF16)
    s_new = jnp.sum(qr * nsel[:, 0:LANES], axis=-1, keepdims=True) * scale
    k_all = jnp.concatenate([pg[0, :, 0:LANES].astype(BF16) for pg in pages], axis=0)
    v_all = jnp.concatenate([pg[0, :, LANES:2 * LANES].astype(BF16) for pg in pages], axis=0)
    nkeys = n_pages * page
    s_all = _dg(qrb, k_all, 1, 1) * scale
    expand = (_iota((LANES, nkeys), 0) == _iota((LANES, nkeys), 1) // SEL_BLOCK).astype(BF16)
    ok = _dg(selb, expand) > 0.5
    s_all = jnp.where(ok, s_all, NEG_INF)
    m = jnp.maximum(s_new, jnp.max(s_all, axis=-1, keepdims=True))
    e_new = jnp.exp(s_new - m)
    e = jnp.where(ok, jnp.exp(s_all - m), 0.0)
    den = e_new + jnp.sum(e, axis=-1, keepdims=True)
    osel = (e_new * nsel[:, LANES:2 * LANES] + _dg(e.astype(BF16), v_all)) / den
    wb = win_ref.shape[1]
    wnd = win_ref[0]
    nwin = nwin_ref[0]
    sw = _dg(qrb, wnd[:, 0:LANES].astype(BF16), 1, 1) * scale
    kwpos = past_len - wb + _iota((1, wb), 1)
    mw = (kwpos <= qpos) & (kwpos > qpos - WINDOW) & (kwpos >= 0)
    sw = jnp.where(mw, sw, NEG_INF)
    sw_new = jnp.sum(qr * nwin[:, 0:LANES], axis=-1, keepdims=True) * scale
    mx = jnp.maximum(jnp.max(sw, axis=-1, keepdims=True), sw_new)
    ew = jnp.where(mw, jnp.exp(sw - mx), 0.0)
    ew_new = jnp.exp(sw_new - mx)
    ow = (_dg(ew.astype(BF16), wnd[:, LANES:2 * LANES].astype(BF16)) + ew_new * nwin[:, LANES:2 * LANES]) / (
        jnp.sum(ew, axis=-1, keepdims=True) + ew_new)
    o_ref[0] = gt[:, 0:1] * oc + gt[:, 1:2] * osel + gt[:, 2:3] * ow
    shifted = pltpu.roll(wnd, wb - 1, 0)
    wout_ref[0] = jnp.where(_iota((wb, 1), 0) == wb - 1, nwin, shifted)


def _attn_sample(page_table, q, qr, gt, ckv, cache_sel, page_off, nsel, cache_win, win_off, nwin, past_len):
    bs, n_pages = page_table.shape
    page = cache_sel.shape[1]
    wb = cache_win.shape[1]
    ng = ckv.shape[2]
    nh = NSA_HEADS
    assert wb == WINDOW and past_len // SEL_BLOCK + 1 <= LANES

    def page_spec(j):
        return pl.BlockSpec((1, page, 256), lambda b, pt: (pt[b, j] + page_off, 0, 0))

    vec = lambda w: pl.BlockSpec((1, nh, w), lambda b, pt: (b, 0, 0))
    rowspec = pl.BlockSpec((1, 1, 256), lambda b, pt: (b, 0, 0))
    return pl.pallas_call(
        functools.partial(_attn_sample_kernel, past_len=past_len, n_pages=n_pages),
        out_shape=[jax.ShapeDtypeStruct((bs, nh, LANES), F32), jax.ShapeDtypeStruct((bs, wb, 256), F32)],
        grid_spec=pltpu.PrefetchScalarGridSpec(
            num_scalar_prefetch=1,
            grid=(bs,),
            in_specs=[vec(LANES), vec(LANES), vec(LANES),
                      pl.BlockSpec((1, 2, ng, LANES), lambda b, pt: (b, 0, 0, 0))]
                     + [page_spec(j) for j in range(n_pages)]
                     + [rowspec,
                        pl.BlockSpec((1, wb, 256), lambda b, pt: (b + win_off, 0, 0)),
                        rowspec,
                        pl.BlockSpec((ng, LANES), lambda b, pt: (0, 0))],
            out_specs=[vec(LANES), pl.BlockSpec((1, wb, 256), lambda b, pt: (b, 0, 0))]),
        compiler_params=_cparams(("arbitrary",)),
        name="attn_sample",
    )(page_table, q, qr, gt, ckv, *([cache_sel] * n_pages), nsel, cache_win, nwin,
      jnp.asarray(_overlap_matrix(ng), dtype=BF16))


def _hilo(w):
    hi = w.astype(BF16)
    return hi, (w - hi.astype(F32)).astype(BF16)


def _tile_heads(v, reps):
    return jnp.tile(v.reshape(1, -1), (1, reps))


def _layer_weights(l, norm_mix, norm_ffn, w_in, rwkv_mu, rwkv_w0, rwkv_w_up, rwkv_a0, rwkv_a_up, rwkv_g_up,
                   rwkv_k_k, rwkv_k_a, rwkv_r_k, rwkv_ln_w, rwkv_ln_b, qk_norm, cmp_pos, cmp_w1, cmp_b1, cmp_w2,
                   cmp_b2, w_out, ffn_w_up, ffn_w_down):
    lw = {}
    d = w_in.shape[1]
    lw['norm_mix'] = norm_mix[l].reshape(1, d)
    lw['norm_ffn'] = norm_ffn[l].reshape(1, d)
    lw['wr'] = (w_in[l][:, :SHIFT_COLS].astype(BF16),)
    w_n = jnp.pad(w_in[l][:, SHIFT_COLS:], ((0, 0), (0, NSA_COLS_PAD - NSA_COLS)))
    lw['wn'] = (w_n.astype(BF16),)
    lw['mu'] = rwkv_mu[l].reshape(1, -1)
    lw['w0'] = rwkv_w0[l].reshape(1, -1)
    lw['a0'] = rwkv_a0[l].reshape(1, -1)
    lw['k_k'] = rwkv_k_k[l].reshape(1, -1)
    lw['k_a'] = rwkv_k_a[l].reshape(1, -1)
    lw['r_k'] = rwkv_r_k[l].reshape(1, -1)
    lw['ln_w'] = rwkv_ln_w[l].reshape(1, -1)
    lw['ln_b'] = rwkv_ln_b[l].reshape(1, -1)
    zeros = jnp.zeros_like(rwkv_w_up[l])
    wc = jnp.concatenate([jnp.concatenate([rwkv_w_up[l], zeros], axis=1),
                          jnp.concatenate([zeros, rwkv_a_up[l]], axis=1)], axis=0)
    lw['wc_h'], lw['wc_l'] = _hilo(wc)
    lw['gu_h'], lw['gu_l'] = _hilo(rwkv_g_up[l])
    qn = qk_norm[l]
    lw['nw'] = jnp.tile(qn, (1, 2))
    lw['nw_kc'] = jnp.tile(qn[1:2], (1, 2))
    w1 = cmp_w1[l].reshape(2, 2, CMP_STRIDE, HEAD_DIM, CMP_HIDDEN)
    eye = jnp.eye(2, dtype=F32)
    w1g = jnp.einsum('khldj,gf->klgdhfj', w1, eye)
    w1g = w1g.reshape(2, CMP_STRIDE * 2 * HEAD_DIM, 2 * 2 * CMP_HIDDEN)
    lw['c1'] = w1g.astype(BF16)
    lw['cmp_pos'] = cmp_pos[l].reshape(2, 1, CMP_BLOCK * HEAD_DIM)
    lw['cmp_w1'] = cmp_w1[l]
    lw['cmp_b1'] = cmp_b1[l].reshape(2, 1, CMP_HIDDEN)
    w2 = cmp_w2[l]
    lw['cmp_w2bd'] = jnp.einsum('kjd,gf->kgjfd', w2, eye).reshape(2, 2 * CMP_HIDDEN, 2 * HEAD_DIM)
    lw['cmp_b2t'] = jnp.tile(cmp_b2[l].reshape(2, 1, HEAD_DIM), (1, 1, 2))
    lw['wo1'] = w_out[l][:RWKV_WIDTH].astype(BF16)
    lw['wo2'] = w_out[l][RWKV_WIDTH:].astype(BF16)
    lw['wu'] = ffn_w_up[l].astype(BF16)
    lw['wd'] = ffn_w_down[l].astype(BF16)
    return lw


def _rope_tables(pos):
    half = HEAD_DIM // 2
    inv_freq = ROPE_THETA ** (-jnp.arange(half, dtype=F32) / half)
    ang = pos.astype(F32)[:, None] * inv_freq[None, :]
    cos = jnp.cos(ang)
    sin = jnp.sin(ang)
    return jnp.tile(cos, (1, 4)), jnp.tile(jnp.concatenate([-sin, sin], axis=1), (1, 2))


def _mods6(m):
    d = m.shape[-1] // 6
    return [m[..., i * d:(i + 1) * d] for i in range(6)]


def _prompt_layer(x, mods, lw, tables):
    bsz, t, d = x.shape
    m = bsz * t
    tm = min(512, t)
    tpb = t // tm
    sh1, sc1, g1, sh2, sc2, g2 = [a.reshape(bsz, 1, d) for a in _mods6(mods)]
    x2 = x.reshape(m, d)
    p_r = _in_proj(x2, lw['norm_mix'], sc1, sh1, lw['wr'], tm, tpb)
    p_n = _in_proj(x2, lw['norm_mix'], sc1, sh1, lw['wn'], tm, tpb)
    shift0 = jnp.zeros((bsz, 1, SHIFT_COLS), F32)
    r, ld, k, v, a, b, g = _rwkv_prep(p_r, shift0, lw, tm, tpb, True)
    seq = lambda z: z.reshape(bsz, t, RWKV_WIDTH)
    q1, y0, gm, ha = _rwkv_chunk(seq(r), seq(ld), seq(k), seq(v), seq(a), seq(b))
    y, h_fin = _rwkv_scan(q1, y0, gm, ha)
    y_r = _rwkv_post(y.reshape(m, RWKV_WIDTH), r, k, v, g, lw, tm)
    s_new = h_fin.reshape(bsz, HEAD_DIM, RWKV_HEADS, HEAD_DIM).transpose(0, 2, 3, 1)
    shift_new = p_r.reshape(bsz, t, SHIFT_COLS)[:, -1]
    q, qr, cmp_rows, sel_rows, win_rows, gt, sel_k, sel_v, win_b = _nsa_proj(
        p_n, tables[0], tables[1], lw['nw'], tm, tpb)
    seqw = lambda z: z.reshape(bsz, t, z.shape[-1])
    ab = _cmp1_prompt(seqw(cmp_rows), lw['c1'])
    ckv = _cmp2(ab, lw['cmp_c1'], lw)
    y_n = _attn_prompt(seqw(q), seqw(qr), seqw(gt), ckv, seqw(sel_k), seqw(sel_v), seqw(win_b))
    x2 = _out_proj(x2, y_r, y_n.reshape(m, NSA_WIDTH), g1, lw['wo1'], lw['wo2'], tm, tpb)
    tmf = min(1024, t)
    x2 = _ffn(x2, lw['norm_ffn'], sc2, sh2, g2, lw['wu'], lw['wd'], tmf, t // tmf)
    rows6 = lambda z: z.reshape(bsz, t, 2, 2, HEAD_DIM)
    wlen = min(WINDOW, t)
    return (x2.reshape(bsz, t, d), s_new, shift_new, rows6(cmp_rows), rows6(sel_rows), rows6(win_rows)[:, -wlen:])


def _sample_layer(x, mods, lw, tables, l, state_rwkv, state_shift, page_table, cache_cmp, cache_sel, cache_win,
                  n_pool, past_len):
    bs, d = x.shape
    tm = bs
    sh1, sc1, g1, sh2, sc2, g2 = [a.reshape(1, bs, d) for a in _mods6(mods)]
    p_r = _in_proj(x, lw['norm_mix'], sc1, sh1, lw['wr'], tm, 1)
    p_n = _in_proj(x, lw['norm_mix'], sc1, sh1, lw['wn'], tm, 1)
    r, ld, k, v, a, b, g = _rwkv_prep(p_r, state_shift[l], lw, tm, 1, False)
    s_new, y = _rwkv_step(state_rwkv, l, r, ld, k, a, b, v)
    y_r = _rwkv_post(y.reshape(bs, RWKV_WIDTH), r, k, v, g, lw, tm)
    q, qr, cmp_rows, sel_rows, win_rows, gt, _, _, _ = _nsa_proj(p_n, tables[0], tables[1], lw['nw'], tm, 1)
    ab = _cmp1_sample(page_table, cache_cmp, l * n_pool, lw['c1'])
    ckv = _cmp2(ab, lw['cmp_c1'], lw)
    grp = (jnp.arange(NSA_HEADS) // (NSA_HEADS // 2))[None, :, None]
    half = (jnp.arange(LANES) // HEAD_DIM)[None, None, :]

    def pad_heads(z):
        z = z.reshape(bs, NSA_HEADS, HEAD_DIM)
        return jnp.where(grp == half, jnp.tile(z, (1, 1, 2)), 0.0)

    gates = jnp.pad(gt[:, :3 * NSA_HEADS].reshape(bs, NSA_HEADS, 3), ((0, 0), (0, 0), (0, LANES - 3)))
    o, win_new = _attn_sample(page_table, pad_heads(q), pad_heads(qr), gates, ckv, cache_sel, l * n_pool,
                              sel_rows.reshape(bs, 1, 256), cache_win, l * bs, win_rows.reshape(bs, 1, 256), past_len)
    o = o.reshape(bs, NSA_HEADS, 2, HEAD_DIM)
    y_n = jnp.concatenate([o[:, :NSA_HEADS // 2, 0], o[:, NSA_HEADS // 2:, 1]], axis=1).reshape(bs, NSA_WIDTH)
    x = _out_proj(x, y_r, y_n, g1, lw['wo1'], lw['wo2'], tm, 1)
    x = _ffn(x, lw['norm_ffn'], sc2, sh2, g2, lw['wu'], lw['wd'], tm, 1)
    rows6 = lambda z: z.reshape(bs, 1, 2, 2, HEAD_DIM)
    return (x, s_new, p_r, rows6(cmp_rows), rows6(sel_rows), win_new.reshape(bs, -1, 2, 2, HEAD_DIM))


def kernel(x_prompt, x_sample, cache_cmp_kv, cache_sel_kv, cache_win_kv, state_rwkv, state_shift, page_table,
           c_prompt, c_sample, ada_w, ada_b, norm_mix, norm_ffn, w_in, rwkv_mu, rwkv_w0, rwkv_w_up, rwkv_a0,
           rwkv_a_up, rwkv_g_up, rwkv_k_k, rwkv_k_a, rwkv_r_k, rwkv_ln_w, rwkv_ln_b, qk_norm, cmp_pos, cmp_w1,
           cmp_b1, cmp_w2, cmp_b2, w_out, ffn_w_up, ffn_w_down):
    depth = ada_w.shape[0]
    bsz, t, d = x_prompt.shape
    bs = x_sample.shape[0]
    n_pool, page = cache_cmp_kv.shape[1], cache_cmp_kv.shape[2]
    n_pages = page_table.shape[1]
    past_len = n_pages * page
    wb = cache_win_kv.shape[2]
    bc = -(-(bsz + bs) // 8) * 8
    c_all = jnp.pad(jnp.concatenate([c_prompt, c_sample], axis=0), ((0, bc - bsz - bs), (0, 0)))
    mods = _ada_mods(c_all, ada_w, ada_b)
    tab_p = _rope_tables(jnp.arange(t, dtype=jnp.int32))
    tab_s = _rope_tables(jnp.full((bs,), past_len, jnp.int32))
    cache_cmp = cache_cmp_kv.reshape(depth * n_pool, page, 256)
    cache_sel = cache_sel_kv.reshape(depth * n_pool, page, 256)
    cache_win = cache_win_kv.reshape(depth * bs, wb, 256)
    st_rwkv = state_rwkv.reshape((depth * bs,) + state_rwkv.shape[2:])
    y_p = x_prompt
    y_s = x_sample.reshape(bs, d)
    outs_p = []
    outs_s = []
    for l in range(depth):
        lw = _layer_weights(l, norm_mix, norm_ffn, w_in, rwkv_mu, rwkv_w0, rwkv_w_up, rwkv_a0, rwkv_a_up,
                            rwkv_g_up, rwkv_k_k, rwkv_k_a, rwkv_r_k, rwkv_ln_w, rwkv_ln_b, qk_norm, cmp_pos,
                            cmp_w1, cmp_b1, cmp_w2, cmp_b2, w_out, ffn_w_up, ffn_w_down)
        lw['cmp_c1'] = _cmp_bias(lw)
        res_p = _prompt_layer(y_p, mods[l, :bsz], lw, tab_p)
        y_p = res_p[0]
        outs_p.append(res_p[1:])
        res_s = _sample_layer(y_s, mods[l, bsz:bsz + bs], lw, tab_s, l, st_rwkv, state_shift, page_table,
                              cache_cmp, cache_sel, cache_win, n_pool, past_len)
        y_s = res_s[0]
        outs_s.append(res_s[1:])
    stack = lambda outs, i: jnp.stack([o[i] for o in outs])
    return (y_p, y_s.reshape(bs, 1, d),
            stack(outs_p, 2), stack(outs_s, 2),
            stack(outs_p, 3), stack(outs_s, 3),
            stack(outs_p, 4), stack(outs_s, 4),
            stack(outs_p, 0), stack(outs_s, 0),
            stack(outs_p, 1), stack(outs_s, 1))
```

```python
import functools

import numpy as np
import jax
import jax.numpy as jnp
from jax import lax
from jax.experimental import pallas as pl
from jax.experimental.pallas import tpu as pltpu

F32 = jnp.float32
BF16 = jnp.bfloat16

HEAD_DIM = 64
RWKV_WIDTH = 512
NSA_WIDTH = 512
RWKV_HEADS = 8
NSA_HEADS = 8
KV_WIDTH = 128
SHIFT_COLS = 1792
NSA_COLS = 1304
NSA_COLS_PAD = 1408
CMP_BLOCK = 32
CMP_STRIDE = 16
CMP_HIDDEN = 128
SEL_BLOCK = 64
SEL_TOP_N = 16
WINDOW = 512
ROPE_THETA = 10000.0
NORM_EPS = 1e-6
GN_EPS = 64e-5
NEG_INF = -1e30
FORCE = 1e9
LANES = 128
RWKV_CHUNK = 64
Q_TILE = 128
VMEM_LIMIT = 56 * 1024 * 1024


def _cparams(sem):
    return pltpu.CompilerParams(dimension_semantics=sem, vmem_limit_bytes=VMEM_LIMIT)


def _split2(x):
    hi = x.astype(BF16)
    lo = (x - hi.astype(F32)).astype(BF16)
    return hi, lo


def _split3(x):
    hi = x.astype(BF16)
    r = x - hi.astype(F32)
    mid = r.astype(BF16)
    lo = (r - mid.astype(F32)).astype(BF16)
    return hi, mid, lo


def _dg(a, b, ca=1, cb=0):
    return lax.dot_general(a, b, (((ca,), (cb,)), ((), ())), preferred_element_type=F32)


def _mm(a, b, ca=1, cb=0):
    ah, al = _split2(a)
    bh, bl = _split2(b)
    return _dg(ah, bh, ca, cb) + _dg(ah, bl, ca, cb) + _dg(al, bh, ca, cb)


def _bf(x):
    return x.astype(BF16)


def _mm1(a, b, ca=1, cb=0):
    return _dg(_bf(a), _bf(b), ca, cb)


def _mm_w(ah, al, wh, wl):
    return _dg(ah, wh) + _dg(ah, wl) + _dg(al, wh)


def _mmp(a2, b2, ca=1, cb=0):
    return _dg(a2[0], b2[0], ca, cb) + _dg(a2[0], b2[1], ca, cb) + _dg(a2[1], b2[0], ca, cb)


def _mm_xr(a, e, ca=1, cb=0):
    a0, a1, a2 = _split3(a)
    return _dg(a0, e, ca, cb) + _dg(a1, e, ca, cb) + _dg(a2, e, ca, cb)


def _mm_xl(e, b, ca=1, cb=0):
    b0, b1, b2 = _split3(b)
    return _dg(e, b0, ca, cb) + _dg(e, b1, ca, cb) + _dg(e, b2, ca, cb)


def _iota(shape, dim):
    return lax.broadcasted_iota(jnp.int32, shape, dim)


def _head_blockdiag():
    return (_iota((LANES, LANES), 0) // HEAD_DIM == _iota((LANES, LANES), 1) // HEAD_DIM).astype(BF16)


def _segsum(x, bd):
    w = x.shape[1]
    outs = [_mm_xr(x[:, j * LANES:(j + 1) * LANES], bd) for j in range(w // LANES)]
    return outs[0] if len(outs) == 1 else jnp.concatenate(outs, axis=1)


def _sigmoid(x):
    return 1.0 / (1.0 + jnp.exp(-x))


def _norm_mod(x, g, sc, sh):
    ms = jnp.mean(x * x, axis=-1, keepdims=True)
    y = x * lax.rsqrt(ms + NORM_EPS) * g
    return y * (1.0 + sc) + sh


def _topk_mask(score, n, axis):
    size = score.shape[axis]
    pos = _iota(score.shape, axis).astype(F32)
    sel = jnp.zeros_like(score)
    sc = score
    for _ in range(n):
        m = jnp.max(sc, axis=axis, keepdims=True)
        idx = jnp.min(jnp.where(sc == m, pos, float(size)), axis=axis, keepdims=True)
        hit = pos == idx
        sel = jnp.where(hit & (m > 0.5 * NEG_INF), 1.0, sel)
        sc = jnp.where(hit, -3e38, sc)
    return sel


def _ada_kernel(c_ref, w_ref, b_ref, o_ref):
    c = c_ref[...]
    o_ref[0] = _mm(c * _sigmoid(c), w_ref[0]) + b_ref[0]


def _ada_mods(c_all, ada_w, ada_b):
    depth, d, n = ada_w.shape
    bc = c_all.shape[0]
    tn = 1024
    return pl.pallas_call(
        _ada_kernel,
        out_shape=jax.ShapeDtypeStruct((depth, bc, n), F32),
        grid=(depth, n // tn),
        in_specs=[pl.BlockSpec((bc, d), lambda l, j: (0, 0)),
                  pl.BlockSpec((1, d, tn), lambda l, j: (l, 0, j)),
                  pl.BlockSpec((1, 1, tn), lambda l, j: (l, 0, j))],
        out_specs=pl.BlockSpec((1, bc, tn), lambda l, j: (l, 0, j)),
        compiler_params=_cparams(("arbitrary", "arbitrary")),
        name="ada_mods",
    )(c_all, ada_w, ada_b.reshape(depth, 1, n))


def _in_proj_kernel(x_ref, g_ref, sc_ref, sh_ref, *refs):
    w_refs, o_ref = refs[:-1], refs[-1]
    h = _norm_mod(x_ref[...], g_ref[...], sc_ref[0], sh_ref[0])
    if len(w_refs) == 1:
        o_ref[...] = _dg(h.astype(BF16), w_refs[0][...])
    else:
        hh, hl = _split2(h)
        o_ref[...] = _mm_w(hh, hl, w_refs[0][...], w_refs[1][...])


def _in_proj(x, g, sc, sh, ws, tm, tpb):
    m, d = x.shape
    n = ws[0].shape[1]
    r = sc.shape[1]
    return pl.pallas_call(
        _in_proj_kernel,
        out_shape=jax.ShapeDtypeStruct((m, n), F32),
        grid=(m // tm,),
        in_specs=[pl.BlockSpec((tm, d), lambda i: (i, 0)),
                  pl.BlockSpec((1, d), lambda i: (0, 0)),
                  pl.BlockSpec((1, r, d), lambda i: (i // tpb, 0, 0)),
                  pl.BlockSpec((1, r, d), lambda i: (i // tpb, 0, 0))]
                 + [pl.BlockSpec((d, n), lambda i: (0, 0)) for _ in ws],
        out_specs=pl.BlockSpec((tm, n), lambda i: (i, 0)),
        compiler_params=_cparams(("arbitrary",)),
        name="in_proj",
    )(x, g, sc, sh, *ws)


def _out_proj_kernel(x_ref, a1_ref, a2_ref, gate_ref, w1, w2, o_ref):
    acc = _dg(a1_ref[...].astype(BF16), w1[...]) + _dg(a2_ref[...].astype(BF16), w2[...])
    o_ref[...] = x_ref[...] + gate_ref[0] * acc


def _out_proj(x, a1, a2, gate, w1, w2, tm, tpb):
    m, d = x.shape
    k1 = a1.shape[1]
    k2 = a2.shape[1]
    r = gate.shape[1]
    return pl.pallas_call(
        _out_proj_kernel,
        out_shape=jax.ShapeDtypeStruct((m, d), F32),
        grid=(m // tm,),
        in_specs=[pl.BlockSpec((tm, d), lambda i: (i, 0)),
                  pl.BlockSpec((tm, k1), lambda i: (i, 0)),
                  pl.BlockSpec((tm, k2), lambda i: (i, 0)),
                  pl.BlockSpec((1, r, d), lambda i: (i // tpb, 0, 0)),
                  pl.BlockSpec((k1, d), lambda i: (0, 0)),
                  pl.BlockSpec((k2, d), lambda i: (0, 0))],
        out_specs=pl.BlockSpec((tm, d), lambda i: (i, 0)),
        compiler_params=_cparams(("arbitrary",)),
        name="out_proj",
    )(x, a1, a2, gate, w1, w2)


def _ffn_kernel(x_ref, g_ref, sc_ref, sh_ref, gate_ref, wg, wu, wd, o_ref, h_s, acc_s):
    j = pl.program_id(1)

    @pl.when(j == 0)
    def _():
        h_s[...] = _norm_mod(x_ref[...], g_ref[...], sc_ref[0], sh_ref[0]).astype(BF16)
        acc_s[...] = jnp.zeros_like(acc_s)

    h = h_s[...]
    gate = _dg(h, wg[...])
    up = _dg(h, wu[...])
    act = gate * _sigmoid(gate) * up
    acc_s[...] += _dg(act.astype(BF16), wd[...])

    @pl.when(j == pl.num_programs(1) - 1)
    def _():
        o_ref[...] = x_ref[...] + gate_ref[0] * acc_s[...]


def _ffn(x, g, sc, sh, gate, w_up, w_down, tm, tpb):
    m, d = x.shape
    dff = w_down.shape[0]
    tf = 256
    nf = dff // tf
    r = sc.shape[1]
    mod = pl.BlockSpec((1, r, d), lambda i, j: (i // tpb, 0, 0))
    return pl.pallas_call(
        _ffn_kernel,
        out_shape=jax.ShapeDtypeStruct((m, d), F32),
        grid=(m // tm, nf),
        in_specs=[pl.BlockSpec((tm, d), lambda i, j: (i, 0)),
                  pl.BlockSpec((1, d), lambda i, j: (0, 0)),
                  mod, mod, mod,
                  pl.BlockSpec((d, tf), lambda i, j: (0, j)),
                  pl.BlockSpec((d, tf), lambda i, j: (0, j + nf)),
                  pl.BlockSpec((tf, d), lambda i, j: (j, 0))],
        out_specs=pl.BlockSpec((tm, d), lambda i, j: (i, 0)),
        scratch_shapes=[pltpu.VMEM((tm, d), BF16), pltpu.VMEM((tm, d), F32)],
        compiler_params=_cparams(("arbitrary", "arbitrary")),
        name="ffn",
    )(x, g, sc, sh, gate, w_up, w_up, w_down)


def _rwkv_prep_kernel(p_ref, prev_ref, mu_ref, wch, wcl, guh, gul, w0_ref, a0_ref, kk_ref, ka_ref,
                      r_o, ld_o, k_o, v_o, a_o, b_o, g_o, *scratch, seq_mode, tpb):
    p = p_ref[...]
    tm = p.shape[0]
    if seq_mode:
        carry = scratch[0]
        i = pl.program_id(0)

        @pl.when(i % tpb == 0)
        def _():
            carry[...] = prev_ref[0]

        rolled = pltpu.roll(p, 1, 0)
        prev = jnp.where(_iota((tm, 1), 0) == 0, carry[...], rolled)
        carry[...] = p[tm - 1:tm, :]
    else:
        prev = prev_ref[...]
    xs = p + mu_ref[...] * (prev - p)
    w_ = RWKV_WIDTH
    r = xs[:, 0:w_]
    k = xs[:, w_:2 * w_]
    v = xs[:, 2 * w_:3 * w_]
    z = xs[:, 3 * w_:3 * w_ + LANES]
    gd = xs[:, 3 * w_ + LANES:3 * w_ + 2 * LANES]
    z = jnp.where(_iota((tm, LANES), 1) < 64, jnp.tanh(z), z)
    zh, zl = _split2(z)
    wa = _mm_w(zh, zl, wch[...], wcl[...])
    zw = -(w0_ref[...] + wa[:, 0:w_])
    softplus = jnp.maximum(zw, 0.0) + jnp.log1p(jnp.exp(-jnp.abs(zw)))
    wlog = -softplus - 0.5
    a_sig = _sigmoid(a0_ref[...] + wa[:, w_:2 * w_])
    sh_, sl_ = _split2(_sigmoid(gd))
    g = _mm_w(sh_, sl_, guh[...], gul[...])
    bd = _head_blockdiag()
    kkv = k * kk_ref[...]
    nrm = jnp.sqrt(_segsum(kkv * kkv, bd))
    kkn = kkv / jnp.maximum(nrm, 1e-12)
    r_o[...] = r
    ld_o[...] = -jnp.exp(wlog)
    k_o[...] = k * (1.0 + (a_sig - 1.0) * ka_ref[...])
    v_o[...] = v
    a_o[...] = -kkn
    b_o[...] = kkn * a_sig
    g_o[...] = g


def _rwkv_prep(p_r, prev, lw, tm, tpb, seq_mode):
    m = p_r.shape[0]
    w_ = RWKV_WIDTH
    row = lambda n: pl.BlockSpec((1, n), lambda i: (0, 0))
    full = lambda a: pl.BlockSpec(a.shape, lambda i: (0, 0))
    if seq_mode:
        prev_spec = pl.BlockSpec((1, 1, SHIFT_COLS), lambda i: (i // tpb, 0, 0))
        scratch = [pltpu.VMEM((1, SHIFT_COLS), F32)]
    else:
        prev_spec = pl.BlockSpec((tm, SHIFT_COLS), lambda i: (i, 0))
        scratch = []
    out = jax.ShapeDtypeStruct((m, w_), F32)
    ospec = pl.BlockSpec((tm, w_), lambda i: (i, 0))
    return pl.pallas_call(
        functools.partial(_rwkv_prep_kernel, seq_mode=seq_mode, tpb=tpb),
        out_shape=[out] * 7,
        grid=(m // tm,),
        in_specs=[pl.BlockSpec((tm, SHIFT_COLS), lambda i: (i, 0)), prev_spec, row(SHIFT_COLS),
                  full(lw['wc_h']), full(lw['wc_l']), full(lw['gu_h']), full(lw['gu_l']),
                  row(w_), row(w_), row(w_), row(w_)],
        out_specs=[ospec] * 7,
        scratch_shapes=scratch,
        compiler_params=_cparams(("arbitrary",)),
        name="rwkv_prep",
    )(p_r, prev, lw['mu'], lw['wc_h'], lw['wc_l'], lw['gu_h'], lw['gu_l'], lw['w0'], lw['a0'], lw['k_k'], lw['k_a'])


def _rwkv_chunk_kernel(r_ref, ld_ref, k_ref, v_ref, a_ref, b_ref, q1_o, y0_o, g_o, ha_o):
    c = RWKV_CHUNK
    c2 = 2 * c
    ld = ld_ref[0]
    lc = _mm_xl((_iota((c, c), 0) >= _iota((c, c), 1)).astype(BF16), ld)
    lx = lc - ld
    lend = lc[c - 1:c, :]
    p_in = jnp.exp(lc)
    inv_p = jnp.exp(-lc)
    p_to_end = jnp.exp(lend - lc)
    p_end = jnp.exp(lend)
    r = r_ref[0]
    k = k_ref[0]
    v = v_ref[0]
    b = b_ref[0]
    rt = r * p_in
    at = a_ref[0] * jnp.exp(lx)
    bt = b * inv_p
    kt = k * inv_p
    bh = b * p_to_end
    kh = k * p_to_end
    row = _iota((c2, c2), 0)
    col = _iota((c2, c2), 1)
    strict = row % c > col % c
    incl = row % c >= col % c
    eye = row == col
    lo = _iota((c, LANES), 1) < HEAD_DIM

    def pair_rows(x):
        return jnp.concatenate([jnp.where(lo, x, 0.0), jnp.where(lo, 0.0, x)], axis=0)

    pairs = range(RWKV_WIDTH // LANES)
    sls = [slice(LANES * j, LANES * (j + 1)) for j in pairs]
    at2 = [pair_rows(at[:, sl]) for sl in sls]
    rt2 = [pair_rows(rt[:, sl]) for sl in sls]
    v2s = [_bf(pair_rows(v[:, sl])) for sl in sls]
    lhs = [_bf(jnp.concatenate([at2[j], rt2[j]], axis=0)) for j in pairs]
    xb = [_dg(lhs[j], _bf(pair_rows(bt[:, sls[j]])), 1, 1) for j in pairs]
    xk = [_dg(lhs[j], _bf(pair_rows(kt[:, sls[j]])), 1, 1) for j in pairs]
    a_ab = [jnp.where(strict, xb[j][0:c2], 0.0) for j in pairs]
    a_rb = [_bf(jnp.where(incl, xb[j][c2:2 * c2], 0.0)) for j in pairs]
    a_ak = [_bf(jnp.where(strict, xk[j][0:c2], 0.0)) for j in pairs]
    a_rk = [_bf(jnp.where(incl, xk[j][c2:2 * c2], 0.0)) for j in pairs]
    akv = [_dg(a_ak[j], v2s[j]) for j in pairs]
    x = [jnp.where(eye, 1.0, 0.0) + a_ab[j] for j in pairs]
    mp = a_ab
    for _ in range(int(np.log2(c)) - 1):
        mps = [_bf(mp[j]) for j in pairs]
        mp = [_dg(mps[j], mps[j]) for j in pairs]
        x = [x[j] + _dg(_bf(mp[j]), _bf(x[j])) for j in pairs]
    wu = [_bf(_dg(_bf(x[j]), _bf(jnp.concatenate([at2[j], akv[j]], axis=1)))) for j in pairs]
    qy = [_dg(a_rb[j], wu[j]) for j in pairs]
    ark_v = [_dg(a_rk[j], v2s[j]) for j in pairs]
    gh = [_dg(_bf(pair_rows(bh[:, sls[j]])), wu[j], 0, 0) for j in pairs]
    khv = [_dg(_bf(pair_rows(kh[:, sls[j]])), v2s[j], 0, 0) for j in pairs]
    for j in pairs:
        sl = sls[j]
        q1 = rt2[j] + qy[j][:, 0:LANES]
        y0 = qy[j][:, LANES:2 * LANES] + ark_v[j]
        q1_o[0, :, sl] = q1[0:c] + q1[c:c2]
        y0_o[0, :, sl] = y0[0:c] + y0[c:c2]
        g_o[0, 0, :, sl] = jnp.where(eye, p_end[:, sl], 0.0) + gh[j][:, 0:LANES]
        ha_o[0, 0, :, sl] = gh[j][:, LANES:2 * LANES] + khv[j]


def _rwkv_chunk(r, ld, k, v, a, b):
    bsz, t, w_ = r.shape
    c = RWKV_CHUNK
    nch = t // c
    ispec = pl.BlockSpec((1, c, w_), lambda i, j: (i, j, 0))
    sspec = pl.BlockSpec((1, 1, 2 * HEAD_DIM, w_), lambda i, j: (i, j, 0, 0))
    return pl.pallas_call(
        _rwkv_chunk_kernel,
        out_shape=[jax.ShapeDtypeStruct((bsz, t, w_), F32)] * 2
                  + [jax.ShapeDtypeStruct((bsz, nch, 2 * HEAD_DIM, w_), F32)] * 2,
        grid=(bsz, nch),
        in_specs=[ispec] * 6,
        out_specs=[ispec, ispec, sspec, sspec],
        compiler_params=_cparams(("arbitrary", "arbitrary")),
        name="rwkv_chunk",
    )(r, ld, k, v, a, b)


def _rwkv_scan_kernel(q1_ref, y0_ref, g_ref, ha_ref, y_o, hf_o, h_s):
    j = pl.program_id(0)

    @pl.when(j == 0)
    def _():
        h_s[...] = jnp.zeros_like(h_s)

    src = j % 2
    dst = (j + 1) % 2
    for bi in range(q1_ref.shape[0]):
        for p in range(RWKV_WIDTH // LANES):
            sl = slice(LANES * p, LANES * (p + 1))
            h0 = _split2(h_s[src, bi, :, sl])
            y_o[bi, :, sl] = _mmp(_split2(q1_ref[bi, :, sl]), h0) + y0_ref[bi, :, sl]
            h_s[dst, bi, :, sl] = _mmp(_split2(g_ref[bi, 0, :, sl]), h0) + ha_ref[bi, 0, :, sl]

    @pl.when(j == pl.num_programs(0) - 1)
    def _():
        hf_o[...] = h_s[dst, :, 0:HEAD_DIM, :] + h_s[dst, :, HEAD_DIM:2 * HEAD_DIM, :]


def _rwkv_scan(q1, y0, g, ha):
    bsz, t, w_ = q1.shape
    c = RWKV_CHUNK
    ispec = pl.BlockSpec((bsz, c, w_), lambda j: (0, j, 0))
    sspec = pl.BlockSpec((bsz, 1, 2 * HEAD_DIM, w_), lambda j: (0, j, 0, 0))
    return pl.pallas_call(
        _rwkv_scan_kernel,
        out_shape=[jax.ShapeDtypeStruct((bsz, t, w_), F32), jax.ShapeDtypeStruct((bsz, HEAD_DIM, w_), F32)],
        grid=(t // c,),
        in_specs=[ispec, ispec, sspec, sspec],
        out_specs=[ispec, pl.BlockSpec((bsz, HEAD_DIM, w_), lambda j: (0, 0, 0))],
        scratch_shapes=[pltpu.VMEM((2, bsz, 2 * HEAD_DIM, w_), F32)],
        compiler_params=_cparams(("arbitrary",)),
        name="rwkv_scan",
    )(q1, y0, g, ha)


def _rwkv_step_kernel(s_ref, r_ref, ld_ref, k_ref, a_ref, b_ref, v_ref, so_ref, y_ref):
    s = s_ref[...]
    sa = jnp.sum(s * a_ref[...], axis=-1, keepdims=True)
    s_new = s * jnp.exp(ld_ref[...]) + sa * b_ref[...] + v_ref[...] * k_ref[...]
    so_ref[...] = s_new
    y_ref[...] = jnp.sum(s_new * r_ref[...], axis=-1, keepdims=True)


def _rwkv_step(state, layer, r, ld, k, a, b, v):
    bs = r.shape[0]
    nh, hd = RWKV_HEADS, HEAD_DIM
    bt = min(8, bs)
    rowv = lambda x: x.reshape(bs, nh, 1, hd)
    vspec = pl.BlockSpec((bt, nh, 1, hd), lambda i: (i, 0, 0, 0))
    cspec = pl.BlockSpec((bt, nh, hd, 1), lambda i: (i, 0, 0, 0))
    off = layer * (bs // bt)
    return pl.pallas_call(
        _rwkv_step_kernel,
        out_shape=[jax.ShapeDtypeStruct((bs, nh, hd, hd), F32), jax.ShapeDtypeStruct((bs, nh, hd, 1), F32)],
        grid=(bs // bt,),
        in_specs=[pl.BlockSpec((bt, nh, hd, hd), lambda i: (i + off, 0, 0, 0)),
                  vspec, vspec, vspec, vspec, vspec, cspec],
        out_specs=[pl.BlockSpec((bt, nh, hd, hd), lambda i: (i, 0, 0, 0)), cspec],
        compiler_params=_cparams(("arbitrary",)),
        name="rwkv_step",
    )(state, rowv(r), rowv(ld), rowv(k), rowv(a), rowv(b), v.reshape(bs, nh, hd, 1))


def _rwkv_post_kernel(y_ref, r_ref, k_ref, v_ref, g_ref, rk_ref, lw_ref, lb_ref, o_ref):
    bd = _head_blockdiag()
    y = y_ref[...]
    inv = 1.0 / HEAD_DIM
    mu = _segsum(y, bd) * inv
    d = y - mu
    var = _segsum(d * d, bd) * inv
    yn = d * lax.rsqrt(var + GN_EPS) * lw_ref[...] + lb_ref[...]
    v = v_ref[...]
    bonus = _segsum(r_ref[...] * k_ref[...] * rk_ref[...], bd) * v
    o_ref[...] = (yn + bonus) * g_ref[...]


def _rwkv_post(y, r, k, v, g, lw, tm):
    m, w_ = y.shape
    tile = pl.BlockSpec((tm, w_), lambda i: (i, 0))
    row = pl.BlockSpec((1, w_), lambda i: (0, 0))
    return pl.pallas_call(
        _rwkv_post_kernel,
        out_shape=jax.ShapeDtypeStruct((m, w_), F32),
        grid=(m // tm,),
        in_specs=[tile] * 5 + [row] * 3,
        out_specs=tile,
        compiler_params=_cparams(("arbitrary",)),
        name="rwkv_post",
    )(y, r, k, v, g, lw['r_k'], lw['ln_w'], lw['ln_b'])


def _nsa_proj_kernel(p_ref, cos_ref, sin_ref, nw_ref, q_o, qr_o, cmp_o, sel_o, win_o, gt_o, selk_o, selv_o, winb_o):
    tm = p_ref.shape[0]
    cos = cos_ref[...]
    sin = sin_ref[...]
    bd = _head_blockdiag()
    first = (_iota((tm, LANES), 1) % HEAD_DIM) < HEAD_DIM // 2
    nw = nw_ref[...]

    def norm(x, w):
        ms = _segsum(x * x, bd) * (1.0 / HEAD_DIM)
        return x * lax.rsqrt(ms + NORM_EPS) * w

    def rope(x):
        rot = jnp.where(first, pltpu.roll(x, LANES - HEAD_DIM // 2, 1), pltpu.roll(x, HEAD_DIM // 2, 1))
        return x * cos + rot * sin

    for j in range(NSA_WIDTH // LANES):
        sl = slice(LANES * j, LANES * (j + 1))
        xn = norm(p_ref[:, sl], nw[0:1])
        q_o[:, sl] = xn
        qr_o[:, sl] = rope(xn)
    o = NSA_WIDTH
    cmp_o[...] = p_ref[:, o:o + 2 * KV_WIDTH]
    ks = rope(norm(p_ref[:, o + 256:o + 384], nw[2:3]))
    vs = p_ref[:, o + 384:o + 512]
    kw = rope(norm(p_ref[:, o + 512:o + 640], nw[3:4]))
    vw = p_ref[:, o + 640:o + 768]
    sel_o[:, 0:LANES] = ks
    sel_o[:, LANES:2 * LANES] = vs
    win_o[:, 0:LANES] = kw
    win_o[:, LANES:2 * LANES] = vw
    gt_o[...] = _sigmoid(p_ref[:, o + 768:o + 896])
    lo = _iota((tm, LANES), 1) < HEAD_DIM
    selk_o[...] = ks.astype(BF16)
    selv_o[:, 0:LANES] = jnp.where(lo, vs, 1.0).astype(BF16)
    selv_o[:, LANES:2 * LANES] = jnp.where(lo, 1.0, vs).astype(BF16)
    winb_o[:, 0:LANES] = kw.astype(BF16)
    winb_o[:, LANES:2 * LANES] = vw.astype(BF16)


def _nsa_proj(p_n, cos_t, sin_t, nw, tm, tpb):
    m = p_n.shape[0]
    tt = cos_t.shape[0] // tm
    tab = pl.BlockSpec((tm, LANES), lambda i: (i % tt, 0))
    shapes = [(m, NSA_WIDTH), (m, NSA_WIDTH), (m, 256), (m, 256), (m, 256), (m, LANES),
              (m, LANES), (m, 256), (m, 256)]
    dtypes = [F32] * 6 + [BF16] * 3
    return pl.pallas_call(
        _nsa_proj_kernel,
        out_shape=[jax.ShapeDtypeStruct(s, dt) for s, dt in zip(shapes, dtypes)],
        grid=(m // tm,),
        in_specs=[pl.BlockSpec((tm, NSA_COLS_PAD), lambda i: (i, 0)), tab, tab,
                  pl.BlockSpec((4, LANES), lambda i: (0, 0))],
        out_specs=[pl.BlockSpec((tm, s[1]), lambda i: (i, 0)) for s in shapes],
        compiler_params=_cparams(("arbitrary",)),
        name="nsa_proj",
    )(p_n, cos_t, sin_t, nw)


def _cmp1_kernel(*refs, n_parts, paged):
    if paged:
        refs = refs[1:]
    parts = refs[:n_parts]
    w_ref, o_ref, xs = refs[n_parts:]
    gpp = parts[0].shape[1] // CMP_STRIDE
    for j, pr in enumerate(parts):
        for l in range(CMP_STRIDE):
            xs[gpp * j:gpp * (j + 1), LANES * l:LANES * (l + 1)] = pr[0, pl.ds(l, gpp, stride=CMP_STRIDE), :]
    res = _dg(_bf(xs[...]), w_ref[0])
    ng = o_ref.shape[2]
    for s in range(o_ref.shape[0]):
        o_ref[s, 0] = res[ng * s:ng * (s + 1)]


def _cmp1_prompt(cmp_rows, w):
    bsz, t, _ = cmp_rows.shape
    rows = min(2048, t)
    ng = rows // CMP_STRIDE
    kdim = CMP_STRIDE * LANES
    return pl.pallas_call(
        functools.partial(_cmp1_kernel, n_parts=1, paged=False),
        out_shape=jax.ShapeDtypeStruct((bsz, 2, t // CMP_STRIDE, 512), F32),
        grid=(bsz, t // rows, 2),
        in_specs=[pl.BlockSpec((1, rows, LANES), lambda b, i, kv: (b, i, kv)),
                  pl.BlockSpec((1, kdim, 512), lambda b, i, kv: (kv, 0, 0))],
        out_specs=pl.BlockSpec((1, 1, ng, 512), lambda b, i, kv: (b, kv, i, 0)),
        scratch_shapes=[pltpu.VMEM((ng, kdim), F32)],
        compiler_params=_cparams(("arbitrary", "arbitrary", "arbitrary")),
        name="cmp1_prompt",
    )(cmp_rows, w)


def _cmp1_sample(page_table, cache, page_off, w):
    bs, n_pages = page_table.shape
    page = cache.shape[1]
    ng = n_pages * page // CMP_STRIDE
    kdim = CMP_STRIDE * LANES
    sps = 4 if bs % 4 == 0 else 1

    def page_spec(s, j):
        return pl.BlockSpec((1, page, LANES), lambda b, kv, pt: (pt[b * sps + s, j] + page_off, 0, kv))

    return pl.pallas_call(
        functools.partial(_cmp1_kernel, n_parts=sps * n_pages, paged=True),
        out_shape=jax.ShapeDtypeStruct((bs, 2, ng, 512), F32),
        grid_spec=pltpu.PrefetchScalarGridSpec(
            num_scalar_prefetch=1,
            grid=(bs // sps, 2),
            in_specs=[page_spec(s, j) for s in range(sps) for j in range(n_pages)] + [
                pl.BlockSpec((1, kdim, 512), lambda b, kv, pt: (kv, 0, 0))],
            out_specs=pl.BlockSpec((sps, 1, ng, 512), lambda b, kv, pt: (b, kv, 0, 0)),
            scratch_shapes=[pltpu.VMEM((sps * ng, kdim), F32)]),
        compiler_params=_cparams(("arbitrary", "arbitrary")),
        name="cmp1_sample",
    )(page_table, *([cache] * (sps * n_pages)), w)


def _cmp_bias_kernel(pos_ref, w1_ref, b1_ref, o_ref):
    posb = jnp.broadcast_to(pos_ref[0], (8, pos_ref.shape[2]))
    o_ref[0] = _mm(posb, w1_ref[0])[0:1] + b1_ref[0]


def _cmp_bias(lw):
    kdim = CMP_BLOCK * HEAD_DIM
    return pl.pallas_call(
        _cmp_bias_kernel,
        out_shape=jax.ShapeDtypeStruct((2, 1, CMP_HIDDEN), F32),
        grid=(2,),
        in_specs=[pl.BlockSpec((1, 1, kdim), lambda kv: (kv, 0, 0)),
                  pl.BlockSpec((1, kdim, CMP_HIDDEN), lambda kv: (kv, 0, 0)),
                  pl.BlockSpec((1, 1, CMP_HIDDEN), lambda kv: (kv, 0, 0))],
        out_specs=pl.BlockSpec((1, 1, CMP_HIDDEN), lambda kv: (kv, 0, 0)),
        compiler_params=_cparams(("arbitrary",)),
        name="cmp_bias",
    )(lw['cmp_pos'], lw['cmp_w1'], lw['cmp_b1'])


def _cmp2_kernel(ab_ref, c1_ref, w2_ref, b2_ref, nw_ref, o_ref):
    kv = pl.program_id(1)
    spb, _, ng, width = ab_ref.shape
    ab = ab_ref[:, 0].reshape(spb * ng, width)
    a_part = ab[:, 0:256]
    b_next = pltpu.roll(ab[:, 256:512], spb * ng - 1, 0)
    c1 = c1_ref[0]
    h = a_part + b_next + jnp.concatenate([c1, c1], axis=1)
    h = 0.5 * h * (1.0 + jnp.tanh(np.sqrt(2.0 / np.pi) * (h + 0.044715 * (h * h * h))))
    out = _mm1(h, w2_ref[0]) + b2_ref[0]
    ms = _segsum(out * out, _head_blockdiag()) * (1.0 / HEAD_DIM)
    normed = out * lax.rsqrt(ms + NORM_EPS) * nw_ref[...]
    o_ref[:, 0] = jnp.where(kv == 0, normed, out).reshape(spb, ng, LANES)


def _cmp2(ab, c1, lw):
    bsz, _, ng, _ = ab.shape
    spb = 8 if bsz % 8 == 0 else 1
    return pl.pallas_call(
        _cmp2_kernel,
        out_shape=jax.ShapeDtypeStruct((bsz, 2, ng, LANES), F32),
        grid=(bsz // spb, 2),
        in_specs=[pl.BlockSpec((spb, 1, ng, 512), lambda b, kv: (b, kv, 0, 0)),
                  pl.BlockSpec((1, 1, CMP_HIDDEN), lambda b, kv: (kv, 0, 0)),
                  pl.BlockSpec((1, 2 * CMP_HIDDEN, LANES), lambda b, kv: (kv, 0, 0)),
                  pl.BlockSpec((1, 1, LANES), lambda b, kv: (kv, 0, 0)),
                  pl.BlockSpec((1, LANES), lambda b, kv: (0, 0))],
        out_specs=pl.BlockSpec((spb, 1, ng, LANES), lambda b, kv: (b, kv, 0, 0)),
        compiler_params=_cparams(("arbitrary", "arbitrary")),
        name="cmp2",
    )(ab, c1, lw['cmp_w2bd'], lw['cmp_b2t'], lw['nw_kc'])


def _masked_exp(s, mask):
    s = jnp.where(mask, s, NEG_INF)
    mx = jnp.maximum(jnp.max(s, axis=-1, keepdims=True), 0.1 * NEG_INF)
    e = jnp.exp(s - mx)
    den = jnp.sum(e, axis=-1, keepdims=True)
    return e, 1.0 / jnp.maximum(den, 1e-30)


def _masked_softmax(s, mask):
    e, rden = _masked_exp(s, mask)
    return e * rden


def _selection_mask(imp, qpos, axis):
    blk = _iota(imp.shape, axis)
    cur = qpos // SEL_BLOCK
    valid = blk * SEL_BLOCK <= qpos
    forced = (blk == 0) | (blk == cur) | (blk == cur - 1)
    score = jnp.where(valid, jnp.where(forced, FORCE, imp), NEG_INF)
    return _topk_mask(score, SEL_TOP_N, axis)


def _attn_prompt_kernel(q_ref, qr_ref, gt_ref, kc_ref, vc_ref, ks_ref, vs0_ref, vs1_ref, *rest, ck):
    nwb = WINDOW // Q_TILE + 1
    kw_refs = rest[0:nwb]
    vw_refs = rest[nwb:2 * nwb]
    ovt_ref, en_ref, o_ref, qg_s, qrg_s, m_s, acc_s, sc_s = rest[2 * nwb:]
    qi = pl.program_id(1)
    tq = Q_TILE
    hpg = NSA_HEADS // 2
    scale = HEAD_DIM ** -0.5
    ng = kc_ref.shape[2]
    lo_lanes = _iota((tq, LANES), 1) < HEAD_DIM
    qpos = qi * tq + _iota((tq, 1), 0)
    qpos_t = qi * tq + _iota((1, tq), 1)
    qpos_all = jnp.concatenate([qpos] * NSA_HEADS, axis=0)
    gt = gt_ref[0]
    kc = kc_ref[0, 0]
    vc = vc_ref[0, 0]
    kw = jnp.concatenate([r[0] for r in kw_refs], axis=0)
    vw = jnp.concatenate([r[0] for r in vw_refs], axis=0)
    kwpos = jnp.concatenate([(qi - (nwb - 1) + j) * tq + _iota((1, tq), 1) for j in range(nwb)], axis=1)
    nch = (qi * tq + tq + ck - 1) // ck
    rows = hpg * tq
    for h in range(NSA_HEADS):
        g = h // hpg
        cb = h // 2
        rs = slice(tq * h, tq * (h + 1))
        keep = lo_lanes if g == 0 else jnp.logical_not(lo_lanes)
        blk = q_ref[0, :, LANES * cb:LANES * (cb + 1)]
        blkr = qr_ref[0, :, LANES * cb:LANES * (cb + 1)]
        if (h % 2 == 0) != (g == 0):
            blk = pltpu.roll(blk, HEAD_DIM, 1)
            blkr = pltpu.roll(blkr, HEAD_DIM, 1)
        qg_s[rs, :] = (jnp.where(keep, blk, 0.0) * scale).astype(BF16)
        qrg_s[rs, :] = (jnp.where(keep, blkr, 0.0) * scale).astype(BF16)
    qrg = qrg_s[...]
    kcb = _bf(kc)
    s = _dg(qg_s[...], kcb, 1, 1)
    mc = (_iota((2 * rows, ng), 1) * CMP_STRIDE + (CMP_BLOCK - 1)) <= qpos_all
    p = _masked_softmax(s, mc)
    oc = _mm1(p, vc)
    nselb = []
    for g in range(2):
        b0 = rows * g
        psum = p[b0:b0 + tq] + p[b0 + tq:b0 + 2 * tq] + p[b0 + 2 * tq:b0 + 3 * tq] + p[b0 + 3 * tq:b0 + 4 * tq]
        imp_t = _mm_xl(ovt_ref[...], psum.T)
        sel_t = _selection_mask(imp_t, qpos_t, 0)
        nselb.append((1.0 - sel_t).T.astype(BF16))
    m_s[...] = jnp.full(m_s.shape, 0.1 * NEG_INF, F32)
    acc_s[...] = jnp.zeros(acc_s.shape, F32)

    def scores(c, slot):
        st = pl.multiple_of(c * ck, ck)
        sc_s[slot] = _dg(qrg, ks_ref[0, pl.ds(st, ck), :], 1, 1)

    def chunk(c, slot, diagonal):
        st = pl.multiple_of(c * ck, ck)
        causal = jnp.where(st + _iota((tq, ck), 1) <= qpos, 0.0, NEG_INF) if diagonal else None
        for g in range(2):
            vs_ref = vs0_ref if g == 0 else vs1_ref
            vch = vs_ref[0, pl.ds(st, ck), :]
            bias = _dg(nselb[g], en_ref[c])
            if diagonal:
                bias = bias + causal
            for hh in range(hpg):
                rs = slice(rows * g + tq * hh, rows * g + tq * (hh + 1))
                sh = sc_s[slot, rs, :] + bias
                m_old = m_s[rs, :]
                m_new = jnp.maximum(m_old, jnp.max(sh, axis=-1, keepdims=True))
                alpha = jnp.exp(m_old - m_new)
                pe = jnp.exp(sh - m_new)
                acc_s[rs, :] = alpha * acc_s[rs, :] + _dg(pe.astype(BF16), vch)
                m_s[rs, :] = m_new

    def body(c, carry):
        scores(c + 1, (c + 1) % 2)
        chunk(c, c % 2, False)
        return carry

    scores(0, 0)
    lax.fori_loop(0, nch - 1, body, 0)
    chunk(nch - 1, (nch - 1) % 2, True)
    acc = acc_s[...]
    osel = acc / pltpu.roll(acc, HEAD_DIM, 1)
    sw = _dg(qrg, kw, 1, 1)
    mw = (kwpos <= qpos_all) & (kwpos > qpos_all - WINDOW) & (kwpos >= 0)
    ew, rw = _masked_exp(sw, mw)
    ow = _dg(ew.astype(BF16), vw) * rw
    for h in range(NSA_HEADS):
        g = h // hpg
        rs = slice(tq * h, tq * (h + 1))
        og = (gt[:, 3 * h:3 * h + 1] * oc[rs] + gt[:, 3 * h + 1:3 * h + 2] * osel[rs]
              + gt[:, 3 * h + 2:3 * h + 3] * ow[rs])
        o_ref[0, :, HEAD_DIM * h:HEAD_DIM * (h + 1)] = og[:, HEAD_DIM * g:HEAD_DIM * (g + 1)]


def _overlap_matrix(ng):
    s = np.arange(ng)[:, None] * CMP_STRIDE
    j = np.arange(LANES)[None, :] * SEL_BLOCK
    return ((s < j + SEL_BLOCK) & (s + CMP_BLOCK > j)).astype(np.float32)


def _block_bias_matrix(t, ck):
    key_blk = (np.arange(t) // SEL_BLOCK).reshape(t // ck, 1, ck)
    blk = np.arange(LANES).reshape(1, LANES, 1)
    return jnp.asarray(np.where(key_blk == blk, NEG_INF, 0.0).astype(np.float32), dtype=BF16)


def _attn_prompt(q, qr, gt, ckv, selk, selv, win):
    bsz, t, _ = q.shape
    tq = Q_TILE
    ng = ckv.shape[2]
    ck = min(1024, t)
    nwb = WINDOW // tq + 1
    hpg = NSA_HEADS // 2
    assert t // SEL_BLOCK <= LANES and t % ck == 0
    ovt = jnp.asarray(_overlap_matrix(ng).T, dtype=BF16)

    def wspec(j, col):
        return pl.BlockSpec((1, tq, LANES), lambda b, i: (b, jnp.maximum(i - (nwb - 1) + j, 0), col))

    tile = lambda w: pl.BlockSpec((1, tq, w), lambda b, i: (b, i, 0))
    return pl.pallas_call(
        functools.partial(_attn_prompt_kernel, ck=ck),
        out_shape=jax.ShapeDtypeStruct((bsz, t, NSA_WIDTH), F32),
        grid=(bsz, t // tq),
        in_specs=[tile(NSA_WIDTH), tile(NSA_WIDTH), tile(LANES),
                  pl.BlockSpec((1, 1, ng, LANES), lambda b, i: (b, 0, 0, 0)),
                  pl.BlockSpec((1, 1, ng, LANES), lambda b, i: (b, 1, 0, 0)),
                  pl.BlockSpec((1, t, LANES), lambda b, i: (b, 0, 0)),
                  pl.BlockSpec((1, t, LANES), lambda b, i: (b, 0, 0)),
                  pl.BlockSpec((1, t, LANES), lambda b, i: (b, 0, 1))]
                 + [wspec(j, 0) for j in range(nwb)] + [wspec(j, 1) for j in range(nwb)]
                 + [pl.BlockSpec((LANES, ng), lambda b, i: (0, 0)),
                    pl.BlockSpec((t // ck, LANES, ck), lambda b, i: (0, 0, 0))],
        out_specs=tile(NSA_WIDTH),
        scratch_shapes=[pltpu.VMEM((NSA_HEADS * tq, LANES), BF16), pltpu.VMEM((NSA_HEADS * tq, LANES), BF16),
                        pltpu.VMEM((NSA_HEADS * tq, 1), F32), pltpu.VMEM((NSA_HEADS * tq, LANES), F32),
                        pltpu.VMEM((2, NSA_HEADS * tq, ck), F32)],
        compiler_params=_cparams(("arbitrary", "arbitrary")),
        name="attn_prompt",
    )(q, qr, gt, ckv, ckv, selk, selv, selv, *([win] * (2 * nwb)), ovt, _block_bias_matrix(t, ck))


def _sample_cmp_kernel(q_ref, ckv_ref, ov_ref, oc_ref, imp_ref, *, past_len):
    nh = NSA_HEADS
    scale = HEAD_DIM ** -0.5
    ng = ckv_ref.shape[2]
    mc = (_iota((nh, ng), 1) * CMP_STRIDE + (CMP_BLOCK - 1)) <= past_len
    same_group = (_iota((nh, nh), 0) // (nh // 2) == _iota((nh, nh), 1) // (nh // 2)).astype(BF16)
    for i in range(q_ref.shape[0]):
        s = _mm1(q_ref[i], ckv_ref[i, 0], 1, 1) * scale
        p = _masked_softmax(s, mc)
        oc_ref[i] = _mm1(p, ckv_ref[i, 1])
        imp_ref[i] = _mm_xr(_mm_xl(same_group, p), ov_ref[...])


def _sample_cmp(q, ckv, past_len):
    bs, nh, _ = q.shape
    ng = ckv.shape[2]
    spb = 8 if bs % 8 == 0 else 1
    vec = pl.BlockSpec((spb, nh, LANES), lambda b: (b, 0, 0))
    return pl.pallas_call(
        functools.partial(_sample_cmp_kernel, past_len=past_len),
        out_shape=[jax.ShapeDtypeStruct((bs, nh, LANES), F32)] * 2,
        grid=(bs // spb,),
        in_specs=[vec, pl.BlockSpec((spb, 2, ng, LANES), lambda b: (b, 0, 0, 0)),
                  pl.BlockSpec((ng, LANES), lambda b: (0, 0))],
        out_specs=[vec, vec],
        compiler_params=_cparams(("arbitrary",)),
        name="sample_cmp",
    )(q, ckv, jnp.asarray(_overlap_matrix(ng), dtype=BF16))


def _sample_topk_kernel(imp_ref, sel_ref, *, past_len):
    sel_ref[...] = _selection_mask(imp_ref[...], past_len, 1).astype(BF16)


def _sample_topk(imp, past_len):
    rows = imp.shape[0]
    return pl.pallas_call(
        functools.partial(_sample_topk_kernel, past_len=past_len),
        out_shape=jax.ShapeDtypeStruct((rows, LANES), BF16),
        grid=(1,),
        in_specs=[pl.BlockSpec((rows, LANES), lambda i: (0, 0))],
        out_specs=pl.BlockSpec((rows, LANES), lambda i: (0, 0)),
        compiler_params=_cparams(("arbitrary",)),
        name="sample_topk",
    )(imp)


def _attn_sample_kernel(pt_ref, qr_ref, gt_ref, oc_ref, selb_ref, *rest, past_len, n_pages):
    pages = rest[:n_pages]
    nsel_ref, win_ref, nwin_ref, o_ref, wout_ref = rest[n_pages:]
    scale = HEAD_DIM ** -0.5
    qpos = past_len
    qr = qr_ref[0]
    gt = gt_ref[0]
    oc = oc_ref[0]
    selb = selb_ref[0]
    page = pages[0].shape[1]
    nsel = nsel_ref[0]
    qrb = qr.astype(BF16)
    s_new = jnp.sum(qr * nsel[:, 0:LANES], axis=-1, keepdims=True) * scale
    k_all = jnp.concatenate([pg[0, :, 0:LANES].astype(BF16) for pg in pages], axis=0)
    v_all = jnp.concatenate([pg[0, :, LANES:2 * LANES].astype(BF16) for pg in pages], axis=0)
    nkeys = n_pages * page
    s_all = _dg(qrb, k_all, 1, 1) * scale
    expand = (_iota((LANES, nkeys), 0) == _iota((LANES, nkeys), 1) // SEL_BLOCK).astype(BF16)
    ok = _dg(selb, expand) > 0.5
    s_all = jnp.where(ok, s_all, NEG_INF)
    m = jnp.maximum(s_new, jnp.max(s_all, axis=-1, keepdims=True))
    e_new = jnp.exp(s_new - m)
    e = jnp.where(ok, jnp.exp(s_all - m), 0.0)
    den = e_new + jnp.sum(e, axis=-1, keepdims=True)
    osel = (e_new * nsel[:, LANES:2 * LANES] + _dg(e.astype(BF16), v_all)) / den
    wb = win_ref.shape[1]
    wnd = win_ref[0]
    nwin = nwin_ref[0]
    sw = _dg(qrb, wnd[:, 0:LANES].astype(BF16), 1, 1) * scale
    kwpos = past_len - wb + _iota((1, wb), 1)
    mw = (kwpos <= qpos) & (kwpos > qpos - WINDOW) & (kwpos >= 0)
    sw = jnp.where(mw, sw, NEG_INF)
    sw_new = jnp.sum(qr * nwin[:, 0:LANES], axis=-1, keepdims=True) * scale
    mx = jnp.maximum(jnp.max(sw, axis=-1, keepdims=True), sw_new)
    ew = jnp.where(mw, jnp.exp(sw - mx), 0.0)
    ew_new = jnp.exp(sw_new - mx)
    ow = (_dg(ew.astype(BF16), wnd[:, LANES:2 * LANES].astype(BF16)) + ew_new * nwin[:, LANES:2 * LANES]) / (
        jnp.sum(ew, axis=-1, keepdims=True) + ew_new)
    o_ref[0] = gt[:, 0:1] * oc + gt[:, 1:2] * osel + gt[:, 2:3] * ow
    shifted = pltpu.roll(wnd, wb - 1, 0)
    wout_ref[0] = jnp.where(_iota((wb, 1), 0) == wb - 1, nwin, shifted)


def _attn_sample(page_table, q, qr, gt, ckv, cache_sel, page_off, nsel, cache_win, win_off, nwin, past_len):
    bs, n_pages = page_table.shape
    page = cache_sel.shape[1]
    wb = cache_win.shape[1]
    nh = NSA_HEADS
    assert wb == WINDOW and past_len // SEL_BLOCK + 1 <= LANES
    oc, imp = _sample_cmp(q, ckv, past_len)
    selb = _sample_topk(imp.reshape(bs * nh, LANES), past_len).reshape(bs, nh, LANES)

    def page_spec(j):
        return pl.BlockSpec((1, page, 256), lambda b, pt: (pt[b, j] + page_off, 0, 0))

    vec = lambda w: pl.BlockSpec((1, nh, w), lambda b, pt: (b, 0, 0))
    rowspec = pl.BlockSpec((1, 1, 256), lambda b, pt: (b, 0, 0))
    return pl.pallas_call(
        functools.partial(_attn_sample_kernel, past_len=past_len, n_pages=n_pages),
        out_shape=[jax.ShapeDtypeStruct((bs, nh, LANES), F32), jax.ShapeDtypeStruct((bs, wb, 256), F32)],
        grid_spec=pltpu.PrefetchScalarGridSpec(
            num_scalar_prefetch=1,
            grid=(bs,),
            in_specs=[vec(LANES), vec(LANES), vec(LANES), vec(LANES)]
                     + [page_spec(j) for j in range(n_pages)]
                     + [rowspec,
                        pl.BlockSpec((1, wb, 256), lambda b, pt: (b + win_off, 0, 0)),
                        rowspec],
            out_specs=[vec(LANES), pl.BlockSpec((1, wb, 256), lambda b, pt: (b, 0, 0))]),
        compiler_params=_cparams(("arbitrary",)),
        name="attn_sample",
    )(page_table, qr, gt, oc, selb, *([cache_sel] * n_pages), nsel, cache_win, nwin)


def _hilo(w):
    hi = w.astype(BF16)
    return hi, (w - hi.astype(F32)).astype(BF16)


def _tile_heads(v, reps):
    return jnp.tile(v.reshape(1, -1), (1, reps))


def _layer_weights(l, norm_mix, norm_ffn, w_in, rwkv_mu, rwkv_w0, rwkv_w_up, rwkv_a0, rwkv_a_up, rwkv_g_up,
                   rwkv_k_k, rwkv_k_a, rwkv_r_k, rwkv_ln_w, rwkv_ln_b, qk_norm, cmp_pos, cmp_w1, cmp_b1, cmp_w2,
                   cmp_b2, w_out, ffn_w_up, ffn_w_down):
    lw = {}
    d = w_in.shape[1]
    lw['norm_mix'] = norm_mix[l].reshape(1, d)
    lw['norm_ffn'] = norm_ffn[l].reshape(1, d)
    lw['wr'] = (w_in[l][:, :SHIFT_COLS].astype(BF16),)
    w_n = jnp.pad(w_in[l][:, SHIFT_COLS:], ((0, 0), (0, NSA_COLS_PAD - NSA_COLS)))
    lw['wn'] = (w_n.astype(BF16),)
    lw['mu'] = rwkv_mu[l].reshape(1, -1)
    lw['w0'] = rwkv_w0[l].reshape(1, -1)
    lw['a0'] = rwkv_a0[l].reshape(1, -1)
    lw['k_k'] = rwkv_k_k[l].reshape(1, -1)
    lw['k_a'] = rwkv_k_a[l].reshape(1, -1)
    lw['r_k'] = rwkv_r_k[l].reshape(1, -1)
    lw['ln_w'] = rwkv_ln_w[l].reshape(1, -1)
    lw['ln_b'] = rwkv_ln_b[l].reshape(1, -1)
    zeros = jnp.zeros_like(rwkv_w_up[l])
    wc = jnp.concatenate([jnp.concatenate([rwkv_w_up[l], zeros], axis=1),
                          jnp.concatenate([zeros, rwkv_a_up[l]], axis=1)], axis=0)
    lw['wc_h'], lw['wc_l'] = _hilo(wc)
    lw['gu_h'], lw['gu_l'] = _hilo(rwkv_g_up[l])
    qn = qk_norm[l]
    lw['nw'] = jnp.tile(qn, (1, 2))
    lw['nw_kc'] = jnp.tile(qn[1:2], (1, 2))
    w1 = cmp_w1[l].reshape(2, 2, CMP_STRIDE, HEAD_DIM, CMP_HIDDEN)
    eye = jnp.eye(2, dtype=F32)
    w1g = jnp.einsum('khldj,gf->klgdhfj', w1, eye)
    w1g = w1g.reshape(2, CMP_STRIDE * 2 * HEAD_DIM, 2 * 2 * CMP_HIDDEN)
    lw['c1'] = w1g.astype(BF16)
    lw['cmp_pos'] = cmp_pos[l].reshape(2, 1, CMP_BLOCK * HEAD_DIM)
    lw['cmp_w1'] = cmp_w1[l]
    lw['cmp_b1'] = cmp_b1[l].reshape(2, 1, CMP_HIDDEN)
    w2 = cmp_w2[l]
    lw['cmp_w2bd'] = jnp.einsum('kjd,gf->kgjfd', w2, eye).reshape(2, 2 * CMP_HIDDEN, 2 * HEAD_DIM)
    lw['cmp_b2t'] = jnp.tile(cmp_b2[l].reshape(2, 1, HEAD_DIM), (1, 1, 2))
    lw['wo1'] = w_out[l][:RWKV_WIDTH].astype(BF16)
    lw['wo2'] = w_out[l][RWKV_WIDTH:].astype(BF16)
    lw['wu'] = ffn_w_up[l].astype(BF16)
    lw['wd'] = ffn_w_down[l].astype(BF16)
    return lw


def _rope_tables(pos):
    half = HEAD_DIM // 2
    inv_freq = ROPE_THETA ** (-jnp.arange(half, dtype=F32) / half)
    ang = pos.astype(F32)[:, None] * inv_freq[None, :]
    cos = jnp.cos(ang)
    sin = jnp.sin(ang)
    return jnp.tile(cos, (1, 4)), jnp.tile(jnp.concatenate([-sin, sin], axis=1), (1, 2))


def _mods6(m):
    d = m.shape[-1] // 6
    return [m[..., i * d:(i + 1) * d] for i in range(6)]


def _prompt_layer(x, mods, lw, tables):
    bsz, t, d = x.shape
    m = bsz * t
    tm = min(512, t)
    tpb = t // tm
    sh1, sc1, g1, sh2, sc2, g2 = [a.reshape(bsz, 1, d) for a in _mods6(mods)]
    x2 = x.reshape(m, d)
    p_r = _in_proj(x2, lw['norm_mix'], sc1, sh1, lw['wr'], tm, tpb)
    p_n = _in_proj(x2, lw['norm_mix'], sc1, sh1, lw['wn'], tm, tpb)
    shift0 = jnp.zeros((bsz, 1, SHIFT_COLS), F32)
    r, ld, k, v, a, b, g = _rwkv_prep(p_r, shift0, lw, tm, tpb, True)
    seq = lambda z: z.reshape(bsz, t, RWKV_WIDTH)
    q1, y0, gm, ha = _rwkv_chunk(seq(r), seq(ld), seq(k), seq(v), seq(a), seq(b))
    y, h_fin = _rwkv_scan(q1, y0, gm, ha)
    y_r = _rwkv_post(y.reshape(m, RWKV_WIDTH), r, k, v, g, lw, tm)
    s_new = h_fin.reshape(bsz, HEAD_DIM, RWKV_HEADS, HEAD_DIM).transpose(0, 2, 3, 1)
    shift_new = p_r.reshape(bsz, t, SHIFT_COLS)[:, -1]
    q, qr, cmp_rows, sel_rows, win_rows, gt, sel_k, sel_v, win_b = _nsa_proj(
        p_n, tables[0], tables[1], lw['nw'], tm, tpb)
    seqw = lambda z: z.reshape(bsz, t, z.shape[-1])
    ab = _cmp1_prompt(seqw(cmp_rows), lw['c1'])
    ckv = _cmp2(ab, lw['cmp_c1'], lw)
    y_n = _attn_prompt(seqw(q), seqw(qr), seqw(gt), ckv, seqw(sel_k), seqw(sel_v), seqw(win_b))
    x2 = _out_proj(x2, y_r, y_n.reshape(m, NSA_WIDTH), g1, lw['wo1'], lw['wo2'], tm, tpb)
    tmf = min(1024, t)
    x2 = _ffn(x2, lw['norm_ffn'], sc2, sh2, g2, lw['wu'], lw['wd'], tmf, t // tmf)
    rows6 = lambda z: z.reshape(bsz, t, 2, 2, HEAD_DIM)
    wlen = min(WINDOW, t)
    return (x2.reshape(bsz, t, d), s_new, shift_new, rows6(cmp_rows), rows6(sel_rows), rows6(win_rows)[:, -wlen:])


def _sample_layer(x, mods, lw, tables, l, state_rwkv, state_shift, page_table, cache_cmp, cache_sel, cache_win,
                  n_pool, past_len):
    bs, d = x.shape
    tm = bs
    sh1, sc1, g1, sh2, sc2, g2 = [a.reshape(1, bs, d) for a in _mods6(mods)]
    p_r = _in_proj(x, lw['norm_mix'], sc1, sh1, lw['wr'], tm, 1)
    p_n = _in_proj(x, lw['norm_mix'], sc1, sh1, lw['wn'], tm, 1)
    r, ld, k, v, a, b, g = _rwkv_prep(p_r, state_shift[l], lw, tm, 1, False)
    s_new, y = _rwkv_step(state_rwkv, l, r, ld, k, a, b, v)
    y_r = _rwkv_post(y.reshape(bs, RWKV_WIDTH), r, k, v, g, lw, tm)
    q, qr, cmp_rows, sel_rows, win_rows, gt, _, _, _ = _nsa_proj(p_n, tables[0], tables[1], lw['nw'], tm, 1)
    ab = _cmp1_sample(page_table, cache_cmp, l * n_pool, lw['c1'])
    ckv = _cmp2(ab, lw['cmp_c1'], lw)
    grp = (jnp.arange(NSA_HEADS) // (NSA_HEADS // 2))[None, :, None]
    half = (jnp.arange(LANES) // HEAD_DIM)[None, None, :]

    def pad_heads(z):
        z = z.reshape(bs, NSA_HEADS, HEAD_DIM)
        return jnp.where(grp == half, jnp.tile(z, (1, 1, 2)), 0.0)

    gates = jnp.pad(gt[:, :3 * NSA_HEADS].reshape(bs, NSA_HEADS, 3), ((0, 0), (0, 0), (0, LANES - 3)))
    o, win_new = _attn_sample(page_table, pad_heads(q), pad_heads(qr), gates, ckv, cache_sel, l * n_pool,
                              sel_rows.reshape(bs, 1, 256), cache_win, l * bs, win_rows.reshape(bs, 1, 256), past_len)
    o = o.reshape(bs, NSA_HEADS, 2, HEAD_DIM)
    y_n = jnp.concatenate([o[:, :NSA_HEADS // 2, 0], o[:, NSA_HEADS // 2:, 1]], axis=1).reshape(bs, NSA_WIDTH)
    x = _out_proj(x, y_r, y_n, g1, lw['wo1'], lw['wo2'], tm, 1)
    x = _ffn(x, lw['norm_ffn'], sc2, sh2, g2, lw['wu'], lw['wd'], tm, 1)
    rows6 = lambda z: z.reshape(bs, 1, 2, 2, HEAD_DIM)
    return (x, s_new, p_r, rows6(cmp_rows), rows6(sel_rows), win_new.reshape(bs, -1, 2, 2, HEAD_DIM))


def kernel(x_prompt, x_sample, cache_cmp_kv, cache_sel_kv, cache_win_kv, state_rwkv, state_shift, page_table,
           c_prompt, c_sample, ada_w, ada_b, norm_mix, norm_ffn, w_in, rwkv_mu, rwkv_w0, rwkv_w_up, rwkv_a0,
           rwkv_a_up, rwkv_g_up, rwkv_k_k, rwkv_k_a, rwkv_r_k, rwkv_ln_w, rwkv_ln_b, qk_norm, cmp_pos, cmp_w1,
           cmp_b1, cmp_w2, cmp_b2, w_out, ffn_w_up, ffn_w_down):
    depth = ada_w.shape[0]
    bsz, t, d = x_prompt.shape
    bs = x_sample.shape[0]
    n_pool, page = cache_cmp_kv.shape[1], cache_cmp_kv.shape[2]
    n_pages = page_table.shape[1]
    past_len = n_pages * page
    wb = cache_win_kv.shape[2]
    bc = -(-(bsz + bs) // 8) * 8
    c_all = jnp.pad(jnp.concatenate([c_prompt, c_sample], axis=0), ((0, bc - bsz - bs), (0, 0)))
    mods = _ada_mods(c_all, ada_w, ada_b)
    tab_p = _rope_tables(jnp.arange(t, dtype=jnp.int32))
    tab_s = _rope_tables(jnp.full((bs,), past_len, jnp.int32))
    cache_cmp = cache_cmp_kv.reshape(depth * n_pool, page, 256)
    cache_sel = cache_sel_kv.reshape(depth * n_pool, page, 256)
    cache_win = cache_win_kv.reshape(depth * bs, wb, 256)
    st_rwkv = state_rwkv.reshape((depth * bs,) + state_rwkv.shape[2:])
    y_p = x_prompt
    y_s = x_sample.reshape(bs, d)
    outs_p = []
    outs_s = []
    for l in range(depth):
        lw = _layer_weights(l, norm_mix, norm_ffn, w_in, rwkv_mu, rwkv_w0, rwkv_w_up, rwkv_a0, rwkv_a_up,
                            rwkv_g_up, rwkv_k_k, rwkv_k_a, rwkv_r_k, rwkv_ln_w, rwkv_ln_b, qk_norm, cmp_pos,
                            cmp_w1, cmp_b1, cmp_w2, cmp_b2, w_out, ffn_w_up, ffn_w_down)
        lw['cmp_c1'] = _cmp_bias(lw)
        res_p = _prompt_layer(y_p, mods[l, :bsz], lw, tab_p)
        y_p = res_p[0]
        outs_p.append(res_p[1:])
        res_s = _sample_layer(y_s, mods[l, bsz:bsz + bs], lw, tab_s, l, st_rwkv, state_shift, page_table,
                              cache_cmp, cache_sel, cache_win, n_pool, past_len)
        y_s = res_s[0]
        outs_s.append(res_s[1:])
    stack = lambda outs, i: jnp.stack([o[i] for o in outs])
    return (y_p, y_s.reshape(bs, 1, d),
            stack(outs_p, 2), stack(outs_s, 2),
            stack(outs_p, 3), stack(outs_s, 3),
            stack(outs_p, 4), stack(outs_s, 4),
            stack(outs_p, 0), stack(outs_s, 0),
            stack(outs_p, 1), stack(outs_s, 1))
```

```python
import functools

import numpy as np
import jax
import jax.numpy as jnp
from jax import lax
from jax.experimental import pallas as pl
from jax.experimental.pallas import tpu as pltpu

F32 = jnp.float32
BF16 = jnp.bfloat16

HEAD_DIM = 64
RWKV_WIDTH = 512
NSA_WIDTH = 512
RWKV_HEADS = 8
NSA_HEADS = 8
KV_WIDTH = 128
SHIFT_COLS = 1792
NSA_COLS = 1304
NSA_COLS_PAD = 1408
CMP_BLOCK = 32
CMP_STRIDE = 16
CMP_HIDDEN = 128
SEL_BLOCK = 64
SEL_TOP_N = 16
WINDOW = 512
ROPE_THETA = 10000.0
NORM_EPS = 1e-6
GN_EPS = 64e-5
NEG_INF = -1e30
FORCE = 1e9
LANES = 128
RWKV_CHUNK = 64
Q_TILE = 128
VMEM_LIMIT = 56 * 1024 * 1024


def _cparams(sem):
    return pltpu.CompilerParams(dimension_semantics=sem, vmem_limit_bytes=VMEM_LIMIT)


def _split2(x):
    hi = x.astype(BF16)
    lo = (x - hi.astype(F32)).astype(BF16)
    return hi, lo


def _split3(x):
    hi = x.astype(BF16)
    r = x - hi.astype(F32)
    mid = r.astype(BF16)
    lo = (r - mid.astype(F32)).astype(BF16)
    return hi, mid, lo


def _dg(a, b, ca=1, cb=0):
    return lax.dot_general(a, b, (((ca,), (cb,)), ((), ())), preferred_element_type=F32)


def _mm(a, b, ca=1, cb=0):
    ah, al = _split2(a)
    bh, bl = _split2(b)
    return _dg(ah, bh, ca, cb) + _dg(ah, bl, ca, cb) + _dg(al, bh, ca, cb)


def _bf(x):
    return x.astype(BF16)


def _mm1(a, b, ca=1, cb=0):
    return _dg(_bf(a), _bf(b), ca, cb)


def _mm_w(ah, al, wh, wl):
    return _dg(ah, wh) + _dg(ah, wl) + _dg(al, wh)


def _mmp(a2, b2, ca=1, cb=0):
    return _dg(a2[0], b2[0], ca, cb) + _dg(a2[0], b2[1], ca, cb) + _dg(a2[1], b2[0], ca, cb)


def _mm_xr(a, e, ca=1, cb=0):
    a0, a1, a2 = _split3(a)
    return _dg(a0, e, ca, cb) + _dg(a1, e, ca, cb) + _dg(a2, e, ca, cb)


def _mm_xl(e, b, ca=1, cb=0):
    b0, b1, b2 = _split3(b)
    return _dg(e, b0, ca, cb) + _dg(e, b1, ca, cb) + _dg(e, b2, ca, cb)


def _iota(shape, dim):
    return lax.broadcasted_iota(jnp.int32, shape, dim)


def _head_blockdiag():
    return (_iota((LANES, LANES), 0) // HEAD_DIM == _iota((LANES, LANES), 1) // HEAD_DIM).astype(BF16)


def _segsum(x, bd):
    w = x.shape[1]
    outs = [_mm_xr(x[:, j * LANES:(j + 1) * LANES], bd) for j in range(w // LANES)]
    return outs[0] if len(outs) == 1 else jnp.concatenate(outs, axis=1)


def _sigmoid(x):
    return 1.0 / (1.0 + jnp.exp(-x))


def _norm_mod(x, g, sc, sh):
    ms = jnp.mean(x * x, axis=-1, keepdims=True)
    y = x * lax.rsqrt(ms + NORM_EPS) * g
    return y * (1.0 + sc) + sh


def _topk_mask(score, n, axis):
    size = score.shape[axis]
    pos = _iota(score.shape, axis).astype(F32)
    sel = jnp.zeros_like(score)
    sc = score
    for _ in range(n):
        m = jnp.max(sc, axis=axis, keepdims=True)
        idx = jnp.min(jnp.where(sc == m, pos, float(size)), axis=axis, keepdims=True)
        hit = pos == idx
        sel = jnp.where(hit & (m > 0.5 * NEG_INF), 1.0, sel)
        sc = jnp.where(hit, -3e38, sc)
    return sel


def _ada_kernel(c_ref, w_ref, b_ref, o_ref):
    c = c_ref[...]
    o_ref[0] = _mm(c * _sigmoid(c), w_ref[0]) + b_ref[0]


def _ada_mods(c_all, ada_w, ada_b):
    depth, d, n = ada_w.shape
    bc = c_all.shape[0]
    tn = 1024
    return pl.pallas_call(
        _ada_kernel,
        out_shape=jax.ShapeDtypeStruct((depth, bc, n), F32),
        grid=(depth, n // tn),
        in_specs=[pl.BlockSpec((bc, d), lambda l, j: (0, 0)),
                  pl.BlockSpec((1, d, tn), lambda l, j: (l, 0, j)),
                  pl.BlockSpec((1, 1, tn), lambda l, j: (l, 0, j))],
        out_specs=pl.BlockSpec((1, bc, tn), lambda l, j: (l, 0, j)),
        compiler_params=_cparams(("arbitrary", "arbitrary")),
        name="ada_mods",
    )(c_all, ada_w, ada_b.reshape(depth, 1, n))


def _in_proj_kernel(x_ref, g_ref, sc_ref, sh_ref, *refs):
    w_refs, o_ref = refs[:-1], refs[-1]
    h = _norm_mod(x_ref[...], g_ref[...], sc_ref[0], sh_ref[0])
    if len(w_refs) == 1:
        o_ref[...] = _dg(h.astype(BF16), w_refs[0][...])
    else:
        hh, hl = _split2(h)
        o_ref[...] = _mm_w(hh, hl, w_refs[0][...], w_refs[1][...])


def _in_proj(x, g, sc, sh, ws, tm, tpb):
    m, d = x.shape
    n = ws[0].shape[1]
    r = sc.shape[1]
    return pl.pallas_call(
        _in_proj_kernel,
        out_shape=jax.ShapeDtypeStruct((m, n), F32),
        grid=(m // tm,),
        in_specs=[pl.BlockSpec((tm, d), lambda i: (i, 0)),
                  pl.BlockSpec((1, d), lambda i: (0, 0)),
                  pl.BlockSpec((1, r, d), lambda i: (i // tpb, 0, 0)),
                  pl.BlockSpec((1, r, d), lambda i: (i // tpb, 0, 0))]
                 + [pl.BlockSpec((d, n), lambda i: (0, 0)) for _ in ws],
        out_specs=pl.BlockSpec((tm, n), lambda i: (i, 0)),
        compiler_params=_cparams(("arbitrary",)),
        name="in_proj",
    )(x, g, sc, sh, *ws)


def _out_proj_kernel(x_ref, a1_ref, a2_ref, gate_ref, w1, w2, o_ref):
    acc = _dg(a1_ref[...].astype(BF16), w1[...]) + _dg(a2_ref[...].astype(BF16), w2[...])
    o_ref[...] = x_ref[...] + gate_ref[0] * acc


def _out_proj(x, a1, a2, gate, w1, w2, tm, tpb):
    m, d = x.shape
    k1 = a1.shape[1]
    k2 = a2.shape[1]
    r = gate.shape[1]
    return pl.pallas_call(
        _out_proj_kernel,
        out_shape=jax.ShapeDtypeStruct((m, d), F32),
        grid=(m // tm,),
        in_specs=[pl.BlockSpec((tm, d), lambda i: (i, 0)),
                  pl.BlockSpec((tm, k1), lambda i: (i, 0)),
                  pl.BlockSpec((tm, k2), lambda i: (i, 0)),
                  pl.BlockSpec((1, r, d), lambda i: (i // tpb, 0, 0)),
                  pl.BlockSpec((k1, d), lambda i: (0, 0)),
                  pl.BlockSpec((k2, d), lambda i: (0, 0))],
        out_specs=pl.BlockSpec((tm, d), lambda i: (i, 0)),
        compiler_params=_cparams(("arbitrary",)),
        name="out_proj",
    )(x, a1, a2, gate, w1, w2)


def _ffn_kernel(x_ref, g_ref, sc_ref, sh_ref, gate_ref, wg, wu, wd, o_ref, h_s, acc_s):
    j = pl.program_id(1)

    @pl.when(j == 0)
    def _():
        h_s[...] = _norm_mod(x_ref[...], g_ref[...], sc_ref[0], sh_ref[0]).astype(BF16)
        acc_s[...] = jnp.zeros_like(acc_s)

    h = h_s[...]
    gate = _dg(h, wg[...])
    up = _dg(h, wu[...])
    act = gate * _sigmoid(gate) * up
    acc_s[...] += _dg(act.astype(BF16), wd[...])

    @pl.when(j == pl.num_programs(1) - 1)
    def _():
        o_ref[...] = x_ref[...] + gate_ref[0] * acc_s[...]


def _ffn(x, g, sc, sh, gate, w_up, w_down, tm, tpb):
    m, d = x.shape
    dff = w_down.shape[0]
    tf = 256
    nf = dff // tf
    r = sc.shape[1]
    mod = pl.BlockSpec((1, r, d), lambda i, j: (i // tpb, 0, 0))
    return pl.pallas_call(
        _ffn_kernel,
        out_shape=jax.ShapeDtypeStruct((m, d), F32),
        grid=(m // tm, nf),
        in_specs=[pl.BlockSpec((tm, d), lambda i, j: (i, 0)),
                  pl.BlockSpec((1, d), lambda i, j: (0, 0)),
                  mod, mod, mod,
                  pl.BlockSpec((d, tf), lambda i, j: (0, j)),
                  pl.BlockSpec((d, tf), lambda i, j: (0, j + nf)),
                  pl.BlockSpec((tf, d), lambda i, j: (j, 0))],
        out_specs=pl.BlockSpec((tm, d), lambda i, j: (i, 0)),
        scratch_shapes=[pltpu.VMEM((tm, d), BF16), pltpu.VMEM((tm, d), F32)],
        compiler_params=_cparams(("arbitrary", "arbitrary")),
        name="ffn",
    )(x, g, sc, sh, gate, w_up, w_up, w_down)


def _rwkv_prep_kernel(p_ref, prev_ref, mu_ref, wch, wcl, guh, gul, w0_ref, a0_ref, kk_ref, ka_ref,
                      r_o, ld_o, k_o, v_o, a_o, b_o, g_o, *scratch, seq_mode, tpb):
    p = p_ref[...]
    tm = p.shape[0]
    if seq_mode:
        carry = scratch[0]
        i = pl.program_id(0)

        @pl.when(i % tpb == 0)
        def _():
            carry[...] = prev_ref[0]

        rolled = pltpu.roll(p, 1, 0)
        prev = jnp.where(_iota((tm, 1), 0) == 0, carry[...], rolled)
        carry[...] = p[tm - 1:tm, :]
    else:
        prev = prev_ref[...]
    xs = p + mu_ref[...] * (prev - p)
    w_ = RWKV_WIDTH
    r = xs[:, 0:w_]
    k = xs[:, w_:2 * w_]
    v = xs[:, 2 * w_:3 * w_]
    z = xs[:, 3 * w_:3 * w_ + LANES]
    gd = xs[:, 3 * w_ + LANES:3 * w_ + 2 * LANES]
    z = jnp.where(_iota((tm, LANES), 1) < 64, jnp.tanh(z), z)
    zh, zl = _split2(z)
    wa = _mm_w(zh, zl, wch[...], wcl[...])
    zw = -(w0_ref[...] + wa[:, 0:w_])
    softplus = jnp.maximum(zw, 0.0) + jnp.log1p(jnp.exp(-jnp.abs(zw)))
    wlog = -softplus - 0.5
    a_sig = _sigmoid(a0_ref[...] + wa[:, w_:2 * w_])
    sh_, sl_ = _split2(_sigmoid(gd))
    g = _mm_w(sh_, sl_, guh[...], gul[...])
    bd = _head_blockdiag()
    kkv = k * kk_ref[...]
    nrm = jnp.sqrt(_segsum(kkv * kkv, bd))
    kkn = kkv / jnp.maximum(nrm, 1e-12)
    r_o[...] = r
    ld_o[...] = -jnp.exp(wlog)
    k_o[...] = k * (1.0 + (a_sig - 1.0) * ka_ref[...])
    v_o[...] = v
    a_o[...] = -kkn
    b_o[...] = kkn * a_sig
    g_o[...] = g


def _rwkv_prep(p_r, prev, lw, tm, tpb, seq_mode):
    m = p_r.shape[0]
    w_ = RWKV_WIDTH
    row = lambda n: pl.BlockSpec((1, n), lambda i: (0, 0))
    full = lambda a: pl.BlockSpec(a.shape, lambda i: (0, 0))
    if seq_mode:
        prev_spec = pl.BlockSpec((1, 1, SHIFT_COLS), lambda i: (i // tpb, 0, 0))
        scratch = [pltpu.VMEM((1, SHIFT_COLS), F32)]
    else:
        prev_spec = pl.BlockSpec((tm, SHIFT_COLS), lambda i: (i, 0))
        scratch = []
    out = jax.ShapeDtypeStruct((m, w_), F32)
    ospec = pl.BlockSpec((tm, w_), lambda i: (i, 0))
    return pl.pallas_call(
        functools.partial(_rwkv_prep_kernel, seq_mode=seq_mode, tpb=tpb),
        out_shape=[out] * 7,
        grid=(m // tm,),
        in_specs=[pl.BlockSpec((tm, SHIFT_COLS), lambda i: (i, 0)), prev_spec, row(SHIFT_COLS),
                  full(lw['wc_h']), full(lw['wc_l']), full(lw['gu_h']), full(lw['gu_l']),
                  row(w_), row(w_), row(w_), row(w_)],
        out_specs=[ospec] * 7,
        scratch_shapes=scratch,
        compiler_params=_cparams(("arbitrary",)),
        name="rwkv_prep",
    )(p_r, prev, lw['mu'], lw['wc_h'], lw['wc_l'], lw['gu_h'], lw['gu_l'], lw['w0'], lw['a0'], lw['k_k'], lw['k_a'])


def _rwkv_chunk_kernel(r_ref, ld_ref, k_ref, v_ref, a_ref, b_ref, q1_o, y0_o, g_o, ha_o):
    c = RWKV_CHUNK
    c2 = 2 * c
    nb = r_ref.shape[0]
    tri = (_iota((c, c), 0) >= _iota((c, c), 1)).astype(BF16)
    at, rt, bt, kt, bh, kh, v, p_end = ([] for _ in range(8))
    for bi in range(nb):
        ld = ld_ref[bi]
        lc = _mm_xl(tri, ld)
        lend = lc[c - 1:c, :]
        inv_p = jnp.exp(-lc)
        p_to_end = jnp.exp(lend - lc)
        p_end.append(jnp.exp(lend))
        v.append(v_ref[bi])
        rt.append(r_ref[bi] * jnp.exp(lc))
        at.append(a_ref[bi] * jnp.exp(lc - ld))
        bt.append(b_ref[bi] * inv_p)
        kt.append(k_ref[bi] * inv_p)
        bh.append(b_ref[bi] * p_to_end)
        kh.append(k_ref[bi] * p_to_end)
    row = _iota((c2, c2), 0)
    col = _iota((c2, c2), 1)
    strict = row % c > col % c
    incl = row % c >= col % c
    eye = row == col
    lo = _iota((c, LANES), 1) < HEAD_DIM

    def pair_rows(x):
        return jnp.concatenate([jnp.where(lo, x, 0.0), jnp.where(lo, 0.0, x)], axis=0)

    chains = [(bi, slice(LANES * j, LANES * (j + 1))) for bi in range(nb) for j in range(RWKV_WIDTH // LANES)]
    pairs = range(len(chains))
    at2 = [pair_rows(at[bi][:, sl]) for bi, sl in chains]
    rt2 = [pair_rows(rt[bi][:, sl]) for bi, sl in chains]
    v2s = [_bf(pair_rows(v[bi][:, sl])) for bi, sl in chains]
    lhs = [_bf(jnp.concatenate([at2[j], rt2[j]], axis=0)) for j in pairs]
    xb = [_dg(lhs[j], _bf(pair_rows(bt[bi][:, sl])), 1, 1) for j, (bi, sl) in enumerate(chains)]
    xk = [_dg(lhs[j], _bf(pair_rows(kt[bi][:, sl])), 1, 1) for j, (bi, sl) in enumerate(chains)]
    a_ab = [jnp.where(strict, xb[j][0:c2], 0.0) for j in pairs]
    a_rb = [_bf(jnp.where(incl, xb[j][c2:2 * c2], 0.0)) for j in pairs]
    a_ak = [_bf(jnp.where(strict, xk[j][0:c2], 0.0)) for j in pairs]
    a_rk = [_bf(jnp.where(incl, xk[j][c2:2 * c2], 0.0)) for j in pairs]
    akv = [_dg(a_ak[j], v2s[j]) for j in pairs]
    x = [jnp.where(eye, 1.0, 0.0) + a_ab[j] for j in pairs]
    mp = a_ab
    for _ in range(int(np.log2(c)) - 1):
        mps = [_bf(mp[j]) for j in pairs]
        mp = [_dg(mps[j], mps[j]) for j in pairs]
        x = [x[j] + _dg(_bf(mp[j]), _bf(x[j])) for j in pairs]
    wu = [_bf(_dg(_bf(x[j]), _bf(jnp.concatenate([at2[j], akv[j]], axis=1)))) for j in pairs]
    qy = [_dg(a_rb[j], wu[j]) for j in pairs]
    ark_v = [_dg(a_rk[j], v2s[j]) for j in pairs]
    gh = [_dg(_bf(pair_rows(bh[bi][:, sl])), wu[j], 0, 0) for j, (bi, sl) in enumerate(chains)]
    khv = [_dg(_bf(pair_rows(kh[bi][:, sl])), v2s[j], 0, 0) for j, (bi, sl) in enumerate(chains)]
    for j, (bi, sl) in enumerate(chains):
        q1 = rt2[j] + qy[j][:, 0:LANES]
        y0 = qy[j][:, LANES:2 * LANES] + ark_v[j]
        q1_o[bi, :, sl] = q1[0:c] + q1[c:c2]
        y0_o[bi, :, sl] = y0[0:c] + y0[c:c2]
        g_o[bi, 0, :, sl] = jnp.where(eye, p_end[bi][:, sl], 0.0) + gh[j][:, 0:LANES]
        ha_o[bi, 0, :, sl] = gh[j][:, LANES:2 * LANES] + khv[j]


def _rwkv_chunk(r, ld, k, v, a, b):
    bsz, t, w_ = r.shape
    c = RWKV_CHUNK
    nch = t // c
    nb = 2 if bsz % 2 == 0 else 1
    ispec = pl.BlockSpec((nb, c, w_), lambda i, j: (i, j, 0))
    sspec = pl.BlockSpec((nb, 1, 2 * HEAD_DIM, w_), lambda i, j: (i, j, 0, 0))
    return pl.pallas_call(
        _rwkv_chunk_kernel,
        out_shape=[jax.ShapeDtypeStruct((bsz, t, w_), F32)] * 2
                  + [jax.ShapeDtypeStruct((bsz, nch, 2 * HEAD_DIM, w_), F32)] * 2,
        grid=(bsz // nb, nch),
        in_specs=[ispec] * 6,
        out_specs=[ispec, ispec, sspec, sspec],
        compiler_params=_cparams(("arbitrary", "arbitrary")),
        name="rwkv_chunk",
    )(r, ld, k, v, a, b)


def _rwkv_scan_kernel(q1_ref, y0_ref, g_ref, ha_ref, y_o, hf_o, h_s):
    j = pl.program_id(0)

    @pl.when(j == 0)
    def _():
        h_s[...] = jnp.zeros_like(h_s)

    c = RWKV_CHUNK
    cps = g_ref.shape[1]
    src = j % 2
    dst = (j + 1) % 2
    chains = [(bi, slice(LANES * p, LANES * (p + 1)))
              for bi in range(q1_ref.shape[0]) for p in range(RWKV_WIDTH // LANES)]
    h = [h_s[src, bi, :, sl] for bi, sl in chains]
    for cc in range(cps):
        rows = slice(c * cc, c * (cc + 1))
        for n, (bi, sl) in enumerate(chains):
            h0 = _split2(h[n])
            y_o[bi, rows, sl] = _mmp(_split2(q1_ref[bi, rows, sl]), h0) + y0_ref[bi, rows, sl]
            h[n] = _mmp(_split2(g_ref[bi, cc, :, sl]), h0) + ha_ref[bi, cc, :, sl]
    for n, (bi, sl) in enumerate(chains):
        h_s[dst, bi, :, sl] = h[n]

    @pl.when(j == pl.num_programs(0) - 1)
    def _():
        for n, (bi, sl) in enumerate(chains):
            hf_o[bi, :, sl] = h[n][0:HEAD_DIM] + h[n][HEAD_DIM:2 * HEAD_DIM]


def _rwkv_scan(q1, y0, g, ha):
    bsz, t, w_ = q1.shape
    nch = t // RWKV_CHUNK
    cps = 4 if nch % 4 == 0 else 1
    c = cps * RWKV_CHUNK
    ispec = pl.BlockSpec((bsz, c, w_), lambda j: (0, j, 0))
    sspec = pl.BlockSpec((bsz, cps, 2 * HEAD_DIM, w_), lambda j: (0, j, 0, 0))
    return pl.pallas_call(
        _rwkv_scan_kernel,
        out_shape=[jax.ShapeDtypeStruct((bsz, t, w_), F32), jax.ShapeDtypeStruct((bsz, HEAD_DIM, w_), F32)],
        grid=(t // c,),
        in_specs=[ispec, ispec, sspec, sspec],
        out_specs=[ispec, pl.BlockSpec((bsz, HEAD_DIM, w_), lambda j: (0, 0, 0))],
        scratch_shapes=[pltpu.VMEM((2, bsz, 2 * HEAD_DIM, w_), F32)],
        compiler_params=_cparams(("arbitrary",)),
        name="rwkv_scan",
    )(q1, y0, g, ha)


def _rwkv_step_kernel(s_ref, r_ref, ld_ref, k_ref, a_ref, b_ref, v_ref, so_ref, y_ref):
    s = s_ref[...]
    sa = jnp.sum(s * a_ref[...], axis=-1, keepdims=True)
    s_new = s * jnp.exp(ld_ref[...]) + sa * b_ref[...] + v_ref[...] * k_ref[...]
    so_ref[...] = s_new
    y_ref[...] = jnp.sum(s_new * r_ref[...], axis=-1, keepdims=True)


def _rwkv_step(state, layer, r, ld, k, a, b, v):
    bs = r.shape[0]
    nh, hd = RWKV_HEADS, HEAD_DIM
    bt = min(8, bs)
    rowv = lambda x: x.reshape(bs, nh, 1, hd)
    vspec = pl.BlockSpec((bt, nh, 1, hd), lambda i: (i, 0, 0, 0))
    cspec = pl.BlockSpec((bt, nh, hd, 1), lambda i: (i, 0, 0, 0))
    off = layer * (bs // bt)
    return pl.pallas_call(
        _rwkv_step_kernel,
        out_shape=[jax.ShapeDtypeStruct((bs, nh, hd, hd), F32), jax.ShapeDtypeStruct((bs, nh, hd, 1), F32)],
        grid=(bs // bt,),
        in_specs=[pl.BlockSpec((bt, nh, hd, hd), lambda i: (i + off, 0, 0, 0)),
                  vspec, vspec, vspec, vspec, vspec, cspec],
        out_specs=[pl.BlockSpec((bt, nh, hd, hd), lambda i: (i, 0, 0, 0)), cspec],
        compiler_params=_cparams(("arbitrary",)),
        name="rwkv_step",
    )(state, rowv(r), rowv(ld), rowv(k), rowv(a), rowv(b), v.reshape(bs, nh, hd, 1))


def _rwkv_post_kernel(y_ref, r_ref, k_ref, v_ref, g_ref, rk_ref, lw_ref, lb_ref, o_ref):
    bd = _head_blockdiag()
    y = y_ref[...]
    inv = 1.0 / HEAD_DIM
    mu = _segsum(y, bd) * inv
    d = y - mu
    var = _segsum(d * d, bd) * inv
    yn = d * lax.rsqrt(var + GN_EPS) * lw_ref[...] + lb_ref[...]
    v = v_ref[...]
    bonus = _segsum(r_ref[...] * k_ref[...] * rk_ref[...], bd) * v
    o_ref[...] = (yn + bonus) * g_ref[...]


def _rwkv_post(y, r, k, v, g, lw, tm):
    m, w_ = y.shape
    tile = pl.BlockSpec((tm, w_), lambda i: (i, 0))
    row = pl.BlockSpec((1, w_), lambda i: (0, 0))
    return pl.pallas_call(
        _rwkv_post_kernel,
        out_shape=jax.ShapeDtypeStruct((m, w_), F32),
        grid=(m // tm,),
        in_specs=[tile] * 5 + [row] * 3,
        out_specs=tile,
        compiler_params=_cparams(("arbitrary",)),
        name="rwkv_post",
    )(y, r, k, v, g, lw['r_k'], lw['ln_w'], lw['ln_b'])


def _nsa_proj_kernel(p_ref, cos_ref, sin_ref, nw_ref, q_o, qr_o, cmp_o, sel_o, win_o, gt_o, selk_o, selv_o, winb_o):
    tm = p_ref.shape[0]
    cos = cos_ref[...]
    sin = sin_ref[...]
    bd = _head_blockdiag()
    first = (_iota((tm, LANES), 1) % HEAD_DIM) < HEAD_DIM // 2
    nw = nw_ref[...]

    def norm(x, w):
        ms = _segsum(x * x, bd) * (1.0 / HEAD_DIM)
        return x * lax.rsqrt(ms + NORM_EPS) * w

    def rope(x):
        rot = jnp.where(first, pltpu.roll(x, LANES - HEAD_DIM // 2, 1), pltpu.roll(x, HEAD_DIM // 2, 1))
        return x * cos + rot * sin

    for j in range(NSA_WIDTH // LANES):
        sl = slice(LANES * j, LANES * (j + 1))
        xn = norm(p_ref[:, sl], nw[0:1])
        q_o[:, sl] = xn
        qr_o[:, sl] = rope(xn)
    o = NSA_WIDTH
    cmp_o[...] = p_ref[:, o:o + 2 * KV_WIDTH]
    ks = rope(norm(p_ref[:, o + 256:o + 384], nw[2:3]))
    vs = p_ref[:, o + 384:o + 512]
    kw = rope(norm(p_ref[:, o + 512:o + 640], nw[3:4]))
    vw = p_ref[:, o + 640:o + 768]
    sel_o[:, 0:LANES] = ks
    sel_o[:, LANES:2 * LANES] = vs
    win_o[:, 0:LANES] = kw
    win_o[:, LANES:2 * LANES] = vw
    gt_o[...] = _sigmoid(p_ref[:, o + 768:o + 896])
    lo = _iota((tm, LANES), 1) < HEAD_DIM
    selk_o[...] = ks.astype(BF16)
    selv_o[:, 0:LANES] = jnp.where(lo, vs, 1.0).astype(BF16)
    selv_o[:, LANES:2 * LANES] = jnp.where(lo, 1.0, vs).astype(BF16)
    winb_o[:, 0:LANES] = kw.astype(BF16)
    winb_o[:, LANES:2 * LANES] = vw.astype(BF16)


def _nsa_proj(p_n, cos_t, sin_t, nw, tm, tpb):
    m = p_n.shape[0]
    tt = cos_t.shape[0] // tm
    tab = pl.BlockSpec((tm, LANES), lambda i: (i % tt, 0))
    shapes = [(m, NSA_WIDTH), (m, NSA_WIDTH), (m, 256), (m, 256), (m, 256), (m, LANES),
              (m, LANES), (m, 256), (m, 256)]
    dtypes = [F32] * 6 + [BF16] * 3
    return pl.pallas_call(
        _nsa_proj_kernel,
        out_shape=[jax.ShapeDtypeStruct(s, dt) for s, dt in zip(shapes, dtypes)],
        grid=(m // tm,),
        in_specs=[pl.BlockSpec((tm, NSA_COLS_PAD), lambda i: (i, 0)), tab, tab,
                  pl.BlockSpec((4, LANES), lambda i: (0, 0))],
        out_specs=[pl.BlockSpec((tm, s[1]), lambda i: (i, 0)) for s in shapes],
        compiler_params=_cparams(("arbitrary",)),
        name="nsa_proj",
    )(p_n, cos_t, sin_t, nw)


def _cmp1_kernel(*refs, n_parts, paged):
    if paged:
        refs = refs[1:]
    parts = refs[:n_parts]
    w_ref, o_ref, xs = refs[n_parts:]
    gpp = parts[0].shape[1] // CMP_STRIDE
    for j, pr in enumerate(parts):
        for l in range(CMP_STRIDE):
            xs[gpp * j:gpp * (j + 1), LANES * l:LANES * (l + 1)] = pr[0, pl.ds(l, gpp, stride=CMP_STRIDE), :]
    res = _dg(_bf(xs[...]), w_ref[0])
    ng = o_ref.shape[2]
    for s in range(o_ref.shape[0]):
        o_ref[s, 0] = res[ng * s:ng * (s + 1)]


def _cmp1_prompt(cmp_rows, w):
    bsz, t, _ = cmp_rows.shape
    rows = min(2048, t)
    ng = rows // CMP_STRIDE
    kdim = CMP_STRIDE * LANES
    return pl.pallas_call(
        functools.partial(_cmp1_kernel, n_parts=1, paged=False),
        out_shape=jax.ShapeDtypeStruct((bsz, 2, t // CMP_STRIDE, 512), F32),
        grid=(bsz, t // rows, 2),
        in_specs=[pl.BlockSpec((1, rows, LANES), lambda b, i, kv: (b, i, kv)),
                  pl.BlockSpec((1, kdim, 512), lambda b, i, kv: (kv, 0, 0))],
        out_specs=pl.BlockSpec((1, 1, ng, 512), lambda b, i, kv: (b, kv, i, 0)),
        scratch_shapes=[pltpu.VMEM((ng, kdim), F32)],
        compiler_params=_cparams(("arbitrary", "arbitrary", "arbitrary")),
        name="cmp1_prompt",
    )(cmp_rows, w)


def _cmp1_sample(page_table, cache, page_off, w):
    bs, n_pages = page_table.shape
    page = cache.shape[1]
    ng = n_pages * page // CMP_STRIDE
    kdim = CMP_STRIDE * LANES
    sps = 4 if bs % 4 == 0 else 1

    def page_spec(s, j):
        return pl.BlockSpec((1, page, LANES), lambda b, kv, pt: (pt[b * sps + s, j] + page_off, 0, kv))

    return pl.pallas_call(
        functools.partial(_cmp1_kernel, n_parts=sps * n_pages, paged=True),
        out_shape=jax.ShapeDtypeStruct((bs, 2, ng, 512), F32),
        grid_spec=pltpu.PrefetchScalarGridSpec(
            num_scalar_prefetch=1,
            grid=(bs // sps, 2),
            in_specs=[page_spec(s, j) for s in range(sps) for j in range(n_pages)] + [
                pl.BlockSpec((1, kdim, 512), lambda b, kv, pt: (kv, 0, 0))],
            out_specs=pl.BlockSpec((sps, 1, ng, 512), lambda b, kv, pt: (b, kv, 0, 0)),
            scratch_shapes=[pltpu.VMEM((sps * ng, kdim), F32)]),
        compiler_params=_cparams(("arbitrary", "arbitrary")),
        name="cmp1_sample",
    )(page_table, *([cache] * (sps * n_pages)), w)


def _cmp_bias_kernel(pos_ref, w1_ref, b1_ref, o_ref):
    posb = jnp.broadcast_to(pos_ref[0], (8, pos_ref.shape[2]))
    o_ref[0] = _mm(posb, w1_ref[0])[0:1] + b1_ref[0]


def _cmp_bias(lw):
    kdim = CMP_BLOCK * HEAD_DIM
    return pl.pallas_call(
        _cmp_bias_kernel,
        out_shape=jax.ShapeDtypeStruct((2, 1, CMP_HIDDEN), F32),
        grid=(2,),
        in_specs=[pl.BlockSpec((1, 1, kdim), lambda kv: (kv, 0, 0)),
                  pl.BlockSpec((1, kdim, CMP_HIDDEN), lambda kv: (kv, 0, 0)),
                  pl.BlockSpec((1, 1, CMP_HIDDEN), lambda kv: (kv, 0, 0))],
        out_specs=pl.BlockSpec((1, 1, CMP_HIDDEN), lambda kv: (kv, 0, 0)),
        compiler_params=_cparams(("arbitrary",)),
        name="cmp_bias",
    )(lw['cmp_pos'], lw['cmp_w1'], lw['cmp_b1'])


def _cmp2_kernel(ab_ref, c1_ref, w2_ref, b2_ref, nw_ref, o_ref):
    kv = pl.program_id(1)
    spb, _, ng, width = ab_ref.shape
    ab = ab_ref[:, 0].reshape(spb * ng, width)
    a_part = ab[:, 0:256]
    b_next = pltpu.roll(ab[:, 256:512], spb * ng - 1, 0)
    c1 = c1_ref[0]
    h = a_part + b_next + jnp.concatenate([c1, c1], axis=1)
    h = 0.5 * h * (1.0 + jnp.tanh(np.sqrt(2.0 / np.pi) * (h + 0.044715 * (h * h * h))))
    out = _mm1(h, w2_ref[0]) + b2_ref[0]
    ms = _segsum(out * out, _head_blockdiag()) * (1.0 / HEAD_DIM)
    normed = out * lax.rsqrt(ms + NORM_EPS) * nw_ref[...]
    o_ref[:, 0] = jnp.where(kv == 0, normed, out).reshape(spb, ng, LANES)


def _cmp2(ab, c1, lw):
    bsz, _, ng, _ = ab.shape
    spb = 8 if bsz % 8 == 0 else 1
    return pl.pallas_call(
        _cmp2_kernel,
        out_shape=jax.ShapeDtypeStruct((bsz, 2, ng, LANES), F32),
        grid=(bsz // spb, 2),
        in_specs=[pl.BlockSpec((spb, 1, ng, 512), lambda b, kv: (b, kv, 0, 0)),
                  pl.BlockSpec((1, 1, CMP_HIDDEN), lambda b, kv: (kv, 0, 0)),
                  pl.BlockSpec((1, 2 * CMP_HIDDEN, LANES), lambda b, kv: (kv, 0, 0)),
                  pl.BlockSpec((1, 1, LANES), lambda b, kv: (kv, 0, 0)),
                  pl.BlockSpec((1, LANES), lambda b, kv: (0, 0))],
        out_specs=pl.BlockSpec((spb, 1, ng, LANES), lambda b, kv: (b, kv, 0, 0)),
        compiler_params=_cparams(("arbitrary", "arbitrary")),
        name="cmp2",
    )(ab, c1, lw['cmp_w2bd'], lw['cmp_b2t'], lw['nw_kc'])


def _masked_exp(s, mask):
    s = jnp.where(mask, s, NEG_INF)
    mx = jnp.maximum(jnp.max(s, axis=-1, keepdims=True), 0.1 * NEG_INF)
    e = jnp.exp(s - mx)
    den = jnp.sum(e, axis=-1, keepdims=True)
    return e, 1.0 / jnp.maximum(den, 1e-30)


def _masked_softmax(s, mask):
    e, rden = _masked_exp(s, mask)
    return e * rden


def _selection_mask(imp, qpos, axis):
    blk = _iota(imp.shape, axis)
    cur = qpos // SEL_BLOCK
    valid = blk * SEL_BLOCK <= qpos
    forced = (blk == 0) | (blk == cur) | (blk == cur - 1)
    score = jnp.where(valid, jnp.where(forced, FORCE, imp), NEG_INF)
    return _topk_mask(score, SEL_TOP_N, axis)


def _attn_prompt_kernel(q_ref, qr_ref, gt_ref, kc_ref, vc_ref, ks_ref, vs0_ref, vs1_ref, *rest, ck):
    nwb = WINDOW // Q_TILE + 1
    kw_refs = rest[0:nwb]
    vw_refs = rest[nwb:2 * nwb]
    ovt_ref, en_ref, o_ref, qg_s, qrg_s, m_s, acc_s, sc_s = rest[2 * nwb:]
    qi = pl.program_id(1)
    tq = Q_TILE
    hpg = NSA_HEADS // 2
    scale = HEAD_DIM ** -0.5
    ng = kc_ref.shape[2]
    lo_lanes = _iota((tq, LANES), 1) < HEAD_DIM
    qpos = qi * tq + _iota((tq, 1), 0)
    qpos_t = qi * tq + _iota((1, tq), 1)
    qpos_all = jnp.concatenate([qpos] * NSA_HEADS, axis=0)
    gt = gt_ref[0]
    kc = kc_ref[0, 0]
    vc = vc_ref[0, 0]
    kw = jnp.concatenate([r[0] for r in kw_refs], axis=0)
    vw = jnp.concatenate([r[0] for r in vw_refs], axis=0)
    kwpos = jnp.concatenate([(qi - (nwb - 1) + j) * tq + _iota((1, tq), 1) for j in range(nwb)], axis=1)
    nch = (qi * tq + tq + ck - 1) // ck
    rows = hpg * tq
    for h in range(NSA_HEADS):
        g = h // hpg
        cb = h // 2
        rs = slice(tq * h, tq * (h + 1))
        keep = lo_lanes if g == 0 else jnp.logical_not(lo_lanes)
        blk = q_ref[0, :, LANES * cb:LANES * (cb + 1)]
        blkr = qr_ref[0, :, LANES * cb:LANES * (cb + 1)]
        if (h % 2 == 0) != (g == 0):
            blk = pltpu.roll(blk, HEAD_DIM, 1)
            blkr = pltpu.roll(blkr, HEAD_DIM, 1)
        qg_s[rs, :] = (jnp.where(keep, blk, 0.0) * scale).astype(BF16)
        qrg_s[rs, :] = (jnp.where(keep, blkr, 0.0) * scale).astype(BF16)
    qrg = qrg_s[...]
    kcb = _bf(kc)
    s = _dg(qg_s[...], kcb, 1, 1)
    mc = (_iota((2 * rows, ng), 1) * CMP_STRIDE + (CMP_BLOCK - 1)) <= qpos_all
    p = _masked_softmax(s, mc)
    oc = _mm1(p, vc)
    nselb = []
    for g in range(2):
        b0 = rows * g
        psum = p[b0:b0 + tq] + p[b0 + tq:b0 + 2 * tq] + p[b0 + 2 * tq:b0 + 3 * tq] + p[b0 + 3 * tq:b0 + 4 * tq]
        imp_t = _mm_xl(ovt_ref[...], psum.T)
        sel_t = _selection_mask(imp_t, qpos_t, 0)
        nselb.append((1.0 - sel_t).T.astype(BF16))
    m_s[...] = jnp.full(m_s.shape, 0.1 * NEG_INF, F32)
    acc_s[...] = jnp.zeros(acc_s.shape, F32)

    def scores(c, slot):
        st = pl.multiple_of(c * ck, ck)
        sc_s[slot] = _dg(qrg, ks_ref[0, pl.ds(st, ck), :], 1, 1)

    def chunk(c, slot, diagonal):
        st = pl.multiple_of(c * ck, ck)
        causal = jnp.where(st + _iota((tq, ck), 1) <= qpos, 0.0, NEG_INF) if diagonal else None
        for g in range(2):
            vs_ref = vs0_ref if g == 0 else vs1_ref
            vch = vs_ref[0, pl.ds(st, ck), :]
            bias = _dg(nselb[g], en_ref[c])
            if diagonal:
                bias = bias + causal
            for hh in range(hpg):
                rs = slice(rows * g + tq * hh, rows * g + tq * (hh + 1))
                sh = sc_s[slot, rs, :] + bias
                m_old = m_s[rs, :]
                m_new = jnp.maximum(m_old, jnp.max(sh, axis=-1, keepdims=True))
                alpha = jnp.exp(m_old - m_new)
                pe = jnp.exp(sh - m_new)
                acc_s[rs, :] = alpha * acc_s[rs, :] + _dg(pe.astype(BF16), vch)
                m_s[rs, :] = m_new

    def body(c, carry):
        scores(c + 1, (c + 1) % 2)
        chunk(c, c % 2, False)
        return carry

    scores(0, 0)
    lax.fori_loop(0, nch - 1, body, 0)
    chunk(nch - 1, (nch - 1) % 2, True)
    acc = acc_s[...]
    osel = acc / pltpu.roll(acc, HEAD_DIM, 1)
    sw = _dg(qrg, kw, 1, 1)
    mw = (kwpos <= qpos_all) & (kwpos > qpos_all - WINDOW) & (kwpos >= 0)
    ew, rw = _masked_exp(sw, mw)
    ow = _dg(ew.astype(BF16), vw) * rw
    for h in range(NSA_HEADS):
        g = h // hpg
        rs = slice(tq * h, tq * (h + 1))
        og = (gt[:, 3 * h:3 * h + 1] * oc[rs] + gt[:, 3 * h + 1:3 * h + 2] * osel[rs]
              + gt[:, 3 * h + 2:3 * h + 3] * ow[rs])
        o_ref[0, :, HEAD_DIM * h:HEAD_DIM * (h + 1)] = og[:, HEAD_DIM * g:HEAD_DIM * (g + 1)]


def _overlap_matrix(ng):
    s = np.arange(ng)[:, None] * CMP_STRIDE
    j = np.arange(LANES)[None, :] * SEL_BLOCK
    return ((s < j + SEL_BLOCK) & (s + CMP_BLOCK > j)).astype(np.float32)


def _block_bias_matrix(t, ck):
    key_blk = (np.arange(t) // SEL_BLOCK).reshape(t // ck, 1, ck)
    blk = np.arange(LANES).reshape(1, LANES, 1)
    return jnp.asarray(np.where(key_blk == blk, NEG_INF, 0.0).astype(np.float32), dtype=BF16)


def _attn_prompt(q, qr, gt, ckv, selk, selv, win):
    bsz, t, _ = q.shape
    tq = Q_TILE
    ng = ckv.shape[2]
    ck = min(1024, t)
    nwb = WINDOW // tq + 1
    hpg = NSA_HEADS // 2
    assert t // SEL_BLOCK <= LANES and t % ck == 0
    ovt = jnp.asarray(_overlap_matrix(ng).T, dtype=BF16)

    def wspec(j, col):
        return pl.BlockSpec((1, tq, LANES), lambda b, i: (b, jnp.maximum(i - (nwb - 1) + j, 0), col))

    tile = lambda w: pl.BlockSpec((1, tq, w), lambda b, i: (b, i, 0))
    return pl.pallas_call(
        functools.partial(_attn_prompt_kernel, ck=ck),
        out_shape=jax.ShapeDtypeStruct((bsz, t, NSA_WIDTH), F32),
        grid=(bsz, t // tq),
        in_specs=[tile(NSA_WIDTH), tile(NSA_WIDTH), tile(LANES),
                  pl.BlockSpec((1, 1, ng, LANES), lambda b, i: (b, 0, 0, 0)),
                  pl.BlockSpec((1, 1, ng, LANES), lambda b, i: (b, 1, 0, 0)),
                  pl.BlockSpec((1, t, LANES), lambda b, i: (b, 0, 0)),
                  pl.BlockSpec((1, t, LANES), lambda b, i: (b, 0, 0)),
                  pl.BlockSpec((1, t, LANES), lambda b, i: (b, 0, 1))]
                 + [wspec(j, 0) for j in range(nwb)] + [wspec(j, 1) for j in range(nwb)]
                 + [pl.BlockSpec((LANES, ng), lambda b, i: (0, 0)),
                    pl.BlockSpec((t // ck, LANES, ck), lambda b, i: (0, 0, 0))],
        out_specs=tile(NSA_WIDTH),
        scratch_shapes=[pltpu.VMEM((NSA_HEADS * tq, LANES), BF16), pltpu.VMEM((NSA_HEADS * tq, LANES), BF16),
                        pltpu.VMEM((NSA_HEADS * tq, 1), F32), pltpu.VMEM((NSA_HEADS * tq, LANES), F32),
                        pltpu.VMEM((2, NSA_HEADS * tq, ck), F32)],
        compiler_params=_cparams(("arbitrary", "arbitrary")),
        name="attn_prompt",
    )(q, qr, gt, ckv, ckv, selk, selv, selv, *([win] * (2 * nwb)), ovt, _block_bias_matrix(t, ck))


def _sample_cmp_kernel(q_ref, ckv_ref, ov_ref, oc_ref, imp_ref, *, past_len):
    nh = NSA_HEADS
    scale = HEAD_DIM ** -0.5
    ng = ckv_ref.shape[2]
    mc = (_iota((nh, ng), 1) * CMP_STRIDE + (CMP_BLOCK - 1)) <= past_len
    same_group = (_iota((nh, nh), 0) // (nh // 2) == _iota((nh, nh), 1) // (nh // 2)).astype(BF16)
    for i in range(q_ref.shape[0]):
        s = _mm1(q_ref[i], ckv_ref[i, 0], 1, 1) * scale
        p = _masked_softmax(s, mc)
        oc_ref[i] = _mm1(p, ckv_ref[i, 1])
        imp_ref[i] = _mm_xr(_mm_xl(same_group, p), ov_ref[...])


def _sample_cmp(q, ckv, past_len):
    bs, nh, _ = q.shape
    ng = ckv.shape[2]
    spb = 8 if bs % 8 == 0 else 1
    vec = pl.BlockSpec((spb, nh, LANES), lambda b: (b, 0, 0))
    return pl.pallas_call(
        functools.partial(_sample_cmp_kernel, past_len=past_len),
        out_shape=[jax.ShapeDtypeStruct((bs, nh, LANES), F32)] * 2,
        grid=(bs // spb,),
        in_specs=[vec, pl.BlockSpec((spb, 2, ng, LANES), lambda b: (b, 0, 0, 0)),
                  pl.BlockSpec((ng, LANES), lambda b: (0, 0))],
        out_specs=[vec, vec],
        compiler_params=_cparams(("arbitrary",)),
        name="sample_cmp",
    )(q, ckv, jnp.asarray(_overlap_matrix(ng), dtype=BF16))


def _sample_topk_kernel(imp_ref, sel_ref, *, past_len):
    sel_ref[...] = _selection_mask(imp_ref[...], past_len, 1).astype(BF16)


def _sample_topk(imp, past_len):
    rows = imp.shape[0]
    return pl.pallas_call(
        functools.partial(_sample_topk_kernel, past_len=past_len),
        out_shape=jax.ShapeDtypeStruct((rows, LANES), BF16),
        grid=(1,),
        in_specs=[pl.BlockSpec((rows, LANES), lambda i: (0, 0))],
        out_specs=pl.BlockSpec((rows, LANES), lambda i: (0, 0)),
        compiler_params=_cparams(("arbitrary",)),
        name="sample_topk",
    )(imp)


def _attn_sample_kernel(pt_ref, qr_ref, gt_ref, oc_ref, selb_ref, *rest, past_len, n_pages):
    pages = rest[:n_pages]
    nsel_ref, win_ref, nwin_ref, o_ref, wout_ref = rest[n_pages:]
    scale = HEAD_DIM ** -0.5
    qpos = past_len
    qr = qr_ref[0]
    gt = gt_ref[0]
    oc = oc_ref[0]
    selb = selb_ref[0]
    page = pages[0].shape[1]
    nsel = nsel_ref[0]
    qrb = qr.astype(BF16)
    s_new = jnp.sum(qr * nsel[:, 0:LANES], axis=-1, keepdims=True) * scale
    k_all = jnp.concatenate([pg[0, :, 0:LANES].astype(BF16) for pg in pages], axis=0)
    v_all = jnp.concatenate([pg[0, :, LANES:2 * LANES].astype(BF16) for pg in pages], axis=0)
    nkeys = n_pages * page
    s_all = _dg(qrb, k_all, 1, 1) * scale
    expand = (_iota((LANES, nkeys), 0) == _iota((LANES, nkeys), 1) // SEL_BLOCK).astype(BF16)
    ok = _dg(selb, expand) > 0.5
    s_all = jnp.where(ok, s_all, NEG_INF)
    m = jnp.maximum(s_new, jnp.max(s_all, axis=-1, keepdims=True))
    e_new = jnp.exp(s_new - m)
    e = jnp.where(ok, jnp.exp(s_all - m), 0.0)
    den = e_new + jnp.sum(e, axis=-1, keepdims=True)
    osel = (e_new * nsel[:, LANES:2 * LANES] + _dg(e.astype(BF16), v_all)) / den
    wb = win_ref.shape[1]
    wnd = win_ref[0]
    nwin = nwin_ref[0]
    sw = _dg(qrb, wnd[:, 0:LANES].astype(BF16), 1, 1) * scale
    kwpos = past_len - wb + _iota((1, wb), 1)
    mw = (kwpos <= qpos) & (kwpos > qpos - WINDOW) & (kwpos >= 0)
    sw = jnp.where(mw, sw, NEG_INF)
    sw_new = jnp.sum(qr * nwin[:, 0:LANES], axis=-1, keepdims=True) * scale
    mx = jnp.maximum(jnp.max(sw, axis=-1, keepdims=True), sw_new)
    ew = jnp.where(mw, jnp.exp(sw - mx), 0.0)
    ew_new = jnp.exp(sw_new - mx)
    ow = (_dg(ew.astype(BF16), wnd[:, LANES:2 * LANES].astype(BF16)) + ew_new * nwin[:, LANES:2 * LANES]) / (
        jnp.sum(ew, axis=-1, keepdims=True) + ew_new)
    o_ref[0] = gt[:, 0:1] * oc + gt[:, 1:2] * osel + gt[:, 2:3] * ow
    shifted = pltpu.roll(wnd, wb - 1, 0)
    wout_ref[0] = jnp.where(_iota((wb, 1), 0) == wb - 1, nwin, shifted)


def _attn_sample(page_table, q, qr, gt, ckv, cache_sel, page_off, nsel, cache_win, win_off, nwin, past_len):
    bs, n_pages = page_table.shape
    page = cache_sel.shape[1]
    wb = cache_win.shape[1]
    nh = NSA_HEADS
    assert wb == WINDOW and past_len // SEL_BLOCK + 1 <= LANES
    oc, imp = _sample_cmp(q, ckv, past_len)
    selb = _sample_topk(imp.reshape(bs * nh, LANES), past_len).reshape(bs, nh, LANES)

    def page_spec(j):
        return pl.BlockSpec((1, page, 256), lambda b, pt: (pt[b, j] + page_off, 0, 0))

    vec = lambda w: pl.BlockSpec((1, nh, w), lambda b, pt: (b, 0, 0))
    rowspec = pl.BlockSpec((1, 1, 256), lambda b, pt: (b, 0, 0))
    return pl.pallas_call(
        functools.partial(_attn_sample_kernel, past_len=past_len, n_pages=n_pages),
        out_shape=[jax.ShapeDtypeStruct((bs, nh, LANES), F32), jax.ShapeDtypeStruct((bs, wb, 256), F32)],
        grid_spec=pltpu.PrefetchScalarGridSpec(
            num_scalar_prefetch=1,
            grid=(bs,),
            in_specs=[vec(LANES), vec(LANES), vec(LANES), vec(LANES)]
                     + [page_spec(j) for j in range(n_pages)]
                     + [rowspec,
                        pl.BlockSpec((1, wb, 256), lambda b, pt: (b + win_off, 0, 0)),
                        rowspec],
            out_specs=[vec(LANES), pl.BlockSpec((1, wb, 256), lambda b, pt: (b, 0, 0))]),
        compiler_params=_cparams(("arbitrary",)),
        name="attn_sample",
    )(page_table, qr, gt, oc, selb, *([cache_sel] * n_pages), nsel, cache_win, nwin)


def _hilo(w):
    hi = w.astype(BF16)
    return hi, (w - hi.astype(F32)).astype(BF16)


def _tile_heads(v, reps):
    return jnp.tile(v.reshape(1, -1), (1, reps))


def _layer_weights(l, norm_mix, norm_ffn, w_in, rwkv_mu, rwkv_w0, rwkv_w_up, rwkv_a0, rwkv_a_up, rwkv_g_up,
                   rwkv_k_k, rwkv_k_a, rwkv_r_k, rwkv_ln_w, rwkv_ln_b, qk_norm, cmp_pos, cmp_w1, cmp_b1, cmp_w2,
                   cmp_b2, w_out, ffn_w_up, ffn_w_down):
    lw = {}
    d = w_in.shape[1]
    lw['norm_mix'] = norm_mix[l].reshape(1, d)
    lw['norm_ffn'] = norm_ffn[l].reshape(1, d)
    lw['wr'] = (w_in[l][:, :SHIFT_COLS].astype(BF16),)
    w_n = jnp.pad(w_in[l][:, SHIFT_COLS:], ((0, 0), (0, NSA_COLS_PAD - NSA_COLS)))
    lw['wn'] = (w_n.astype(BF16),)
    lw['mu'] = rwkv_mu[l].reshape(1, -1)
    lw['w0'] = rwkv_w0[l].reshape(1, -1)
    lw['a0'] = rwkv_a0[l].reshape(1, -1)
    lw['k_k'] = rwkv_k_k[l].reshape(1, -1)
    lw['k_a'] = rwkv_k_a[l].reshape(1, -1)
    lw['r_k'] = rwkv_r_k[l].reshape(1, -1)
    lw['ln_w'] = rwkv_ln_w[l].reshape(1, -1)
    lw['ln_b'] = rwkv_ln_b[l].reshape(1, -1)
    zeros = jnp.zeros_like(rwkv_w_up[l])
    wc = jnp.concatenate([jnp.concatenate([rwkv_w_up[l], zeros], axis=1),
                          jnp.concatenate([zeros, rwkv_a_up[l]], axis=1)], axis=0)
    lw['wc_h'], lw['wc_l'] = _hilo(wc)
    lw['gu_h'], lw['gu_l'] = _hilo(rwkv_g_up[l])
    qn = qk_norm[l]
    lw['nw'] = jnp.tile(qn, (1, 2))
    lw['nw_kc'] = jnp.tile(qn[1:2], (1, 2))
    w1 = cmp_w1[l].reshape(2, 2, CMP_STRIDE, HEAD_DIM, CMP_HIDDEN)
    eye = jnp.eye(2, dtype=F32)
    w1g = jnp.einsum('khldj,gf->klgdhfj', w1, eye)
    w1g = w1g.reshape(2, CMP_STRIDE * 2 * HEAD_DIM, 2 * 2 * CMP_HIDDEN)
    lw['c1'] = w1g.astype(BF16)
    lw['cmp_pos'] = cmp_pos[l].reshape(2, 1, CMP_BLOCK * HEAD_DIM)
    lw['cmp_w1'] = cmp_w1[l]
    lw['cmp_b1'] = cmp_b1[l].reshape(2, 1, CMP_HIDDEN)
    w2 = cmp_w2[l]
    lw['cmp_w2bd'] = jnp.einsum('kjd,gf->kgjfd', w2, eye).reshape(2, 2 * CMP_HIDDEN, 2 * HEAD_DIM)
    lw['cmp_b2t'] = jnp.tile(cmp_b2[l].reshape(2, 1, HEAD_DIM), (1, 1, 2))
    lw['wo1'] = w_out[l][:RWKV_WIDTH].astype(BF16)
    lw['wo2'] = w_out[l][RWKV_WIDTH:].astype(BF16)
    lw['wu'] = ffn_w_up[l].astype(BF16)
    lw['wd'] = ffn_w_down[l].astype(BF16)
    return lw


def _rope_tables(pos):
    half = HEAD_DIM // 2
    inv_freq = ROPE_THETA ** (-jnp.arange(half, dtype=F32) / half)
    ang = pos.astype(F32)[:, None] * inv_freq[None, :]
    cos = jnp.cos(ang)
    sin = jnp.sin(ang)
    return jnp.tile(cos, (1, 4)), jnp.tile(jnp.concatenate([-sin, sin], axis=1), (1, 2))


def _mods6(m):
    d = m.shape[-1] // 6
    return [m[..., i * d:(i + 1) * d] for i in range(6)]


def _prompt_layer(x, mods, lw, tables):
    bsz, t, d = x.shape
    m = bsz * t
    tm = min(512, t)
    tpb = t // tm
    sh1, sc1, g1, sh2, sc2, g2 = [a.reshape(bsz, 1, d) for a in _mods6(mods)]
    x2 = x.reshape(m, d)
    p_r = _in_proj(x2, lw['norm_mix'], sc1, sh1, lw['wr'], tm, tpb)
    p_n = _in_proj(x2, lw['norm_mix'], sc1, sh1, lw['wn'], tm, tpb)
    shift0 = jnp.zeros((bsz, 1, SHIFT_COLS), F32)
    r, ld, k, v, a, b, g = _rwkv_prep(p_r, shift0, lw, tm, tpb, True)
    seq = lambda z: z.reshape(bsz, t, RWKV_WIDTH)
    q1, y0, gm, ha = _rwkv_chunk(seq(r), seq(ld), seq(k), seq(v), seq(a), seq(b))
    y, h_fin = _rwkv_scan(q1, y0, gm, ha)
    y_r = _rwkv_post(y.reshape(m, RWKV_WIDTH), r, k, v, g, lw, tm)
    s_new = h_fin.reshape(bsz, HEAD_DIM, RWKV_HEADS, HEAD_DIM).transpose(0, 2, 3, 1)
    shift_new = p_r.reshape(bsz, t, SHIFT_COLS)[:, -1]
    q, qr, cmp_rows, sel_rows, win_rows, gt, sel_k, sel_v, win_b = _nsa_proj(
        p_n, tables[0], tables[1], lw['nw'], tm, tpb)
    seqw = lambda z: z.reshape(bsz, t, z.shape[-1])
    ab = _cmp1_prompt(seqw(cmp_rows), lw['c1'])
    ckv = _cmp2(ab, lw['cmp_c1'], lw)
    y_n = _attn_prompt(seqw(q), seqw(qr), seqw(gt), ckv, seqw(sel_k), seqw(sel_v), seqw(win_b))
    x2 = _out_proj(x2, y_r, y_n.reshape(m, NSA_WIDTH), g1, lw['wo1'], lw['wo2'], tm, tpb)
    tmf = min(1024, t)
    x2 = _ffn(x2, lw['norm_ffn'], sc2, sh2, g2, lw['wu'], lw['wd'], tmf, t // tmf)
    rows6 = lambda z: z.reshape(bsz, t, 2, 2, HEAD_DIM)
    wlen = min(WINDOW, t)
    return (x2.reshape(bsz, t, d), s_new, shift_new, rows6(cmp_rows), rows6(sel_rows), rows6(win_rows)[:, -wlen:])


def _sample_layer(x, mods, lw, tables, l, state_rwkv, state_shift, page_table, cache_cmp, cache_sel, cache_win,
                  n_pool, past_len):
    bs, d = x.shape
    tm = bs
    sh1, sc1, g1, sh2, sc2, g2 = [a.reshape(1, bs, d) for a in _mods6(mods)]
    p_r = _in_proj(x, lw['norm_mix'], sc1, sh1, lw['wr'], tm, 1)
    p_n = _in_proj(x, lw['norm_mix'], sc1, sh1, lw['wn'], tm, 1)
    r, ld, k, v, a, b, g = _rwkv_prep(p_r, state_shift[l], lw, tm, 1, False)
    s_new, y = _rwkv_step(state_rwkv, l, r, ld, k, a, b, v)
    y_r = _rwkv_post(y.reshape(bs, RWKV_WIDTH), r, k, v, g, lw, tm)
    q, qr, cmp_rows, sel_rows, win_rows, gt, _, _, _ = _nsa_proj(p_n, tables[0], tables[1], lw['nw'], tm, 1)
    ab = _cmp1_sample(page_table, cache_cmp, l * n_pool, lw['c1'])
    ckv = _cmp2(ab, lw['cmp_c1'], lw)
    grp = (jnp.arange(NSA_HEADS) // (NSA_HEADS // 2))[None, :, None]
    half = (jnp.arange(LANES) // HEAD_DIM)[None, None, :]

    def pad_heads(z):
        z = z.reshape(bs, NSA_HEADS, HEAD_DIM)
        return jnp.where(grp == half, jnp.tile(z, (1, 1, 2)), 0.0)

    gates = jnp.pad(gt[:, :3 * NSA_HEADS].reshape(bs, NSA_HEADS, 3), ((0, 0), (0, 0), (0, LANES - 3)))
    o, win_new = _attn_sample(page_table, pad_heads(q), pad_heads(qr), gates, ckv, cache_sel, l * n_pool,
                              sel_rows.reshape(bs, 1, 256), cache_win, l * bs, win_rows.reshape(bs, 1, 256), past_len)
    o = o.reshape(bs, NSA_HEADS, 2, HEAD_DIM)
    y_n = jnp.concatenate([o[:, :NSA_HEADS // 2, 0], o[:, NSA_HEADS // 2:, 1]], axis=1).reshape(bs, NSA_WIDTH)
    x = _out_proj(x, y_r, y_n, g1, lw['wo1'], lw['wo2'], tm, 1)
    x = _ffn(x, lw['norm_ffn'], sc2, sh2, g2, lw['wu'], lw['wd'], tm, 1)
    rows6 = lambda z: z.reshape(bs, 1, 2, 2, HEAD_DIM)
    return (x, s_new, p_r, rows6(cmp_rows), rows6(sel_rows), win_new.reshape(bs, -1, 2, 2, HEAD_DIM))


def kernel(x_prompt, x_sample, cache_cmp_kv, cache_sel_kv, cache_win_kv, state_rwkv, state_shift, page_table,
           c_prompt, c_sample, ada_w, ada_b, norm_mix, norm_ffn, w_in, rwkv_mu, rwkv_w0, rwkv_w_up, rwkv_a0,
           rwkv_a_up, rwkv_g_up, rwkv_k_k, rwkv_k_a, rwkv_r_k, rwkv_ln_w, rwkv_ln_b, qk_norm, cmp_pos, cmp_w1,
           cmp_b1, cmp_w2, cmp_b2, w_out, ffn_w_up, ffn_w_down):
    depth = ada_w.shape[0]
    bsz, t, d = x_prompt.shape
    bs = x_sample.shape[0]
    n_pool, page = cache_cmp_kv.shape[1], cache_cmp_kv.shape[2]
    n_pages = page_table.shape[1]
    past_len = n_pages * page
    wb = cache_win_kv.shape[2]
    bc = -(-(bsz + bs) // 8) * 8
    c_all = jnp.pad(jnp.concatenate([c_prompt, c_sample], axis=0), ((0, bc - bsz - bs), (0, 0)))
    mods = _ada_mods(c_all, ada_w, ada_b)
    tab_p = _rope_tables(jnp.arange(t, dtype=jnp.int32))
    tab_s = _rope_tables(jnp.full((bs,), past_len, jnp.int32))
    cache_cmp = cache_cmp_kv.reshape(depth * n_pool, page, 256)
    cache_sel = cache_sel_kv.reshape(depth * n_pool, page, 256)
    cache_win = cache_win_kv.reshape(depth * bs, wb, 256)
    st_rwkv = state_rwkv.reshape((depth * bs,) + state_rwkv.shape[2:])
    y_p = x_prompt
    y_s = x_sample.reshape(bs, d)
    outs_p = []
    outs_s = []
    for l in range(depth):
        lw = _layer_weights(l, norm_mix, norm_ffn, w_in, rwkv_mu, rwkv_w0, rwkv_w_up, rwkv_a0, rwkv_a_up,
                            rwkv_g_up, rwkv_k_k, rwkv_k_a, rwkv_r_k, rwkv_ln_w, rwkv_ln_b, qk_norm, cmp_pos,
                            cmp_w1, cmp_b1, cmp_w2, cmp_b2, w_out, ffn_w_up, ffn_w_down)
        lw['cmp_c1'] = _cmp_bias(lw)
        res_p = _prompt_layer(y_p, mods[l, :bsz], lw, tab_p)
        y_p = res_p[0]
        outs_p.append(res_p[1:])
        res_s = _sample_layer(y_s, mods[l, bsz:bsz + bs], lw, tab_s, l, st_rwkv, state_shift, page_table,
                              cache_cmp, cache_sel, cache_win, n_pool, past_len)
        y_s = res_s[0]
        outs_s.append(res_s[1:])
    stack = lambda outs, i: jnp.stack([o[i] for o in outs])
    return (y_p, y_s.reshape(bs, 1, d),
            stack(outs_p, 2), stack(outs_s, 2),
            stack(outs_p, 3), stack(outs_s, 3),
            stack(outs_p, 4), stack(outs_s, 4),
            stack(outs_p, 0), stack(outs_s, 0),
            stack(outs_p, 1), stack(outs_s, 1))
```

```python
import functools

import numpy as np
import jax
import jax.numpy as jnp
from jax import lax
from jax.experimental import pallas as pl
from jax.experimental.pallas import tpu as pltpu

F32 = jnp.float32
BF16 = jnp.bfloat16

HEAD_DIM = 64
RWKV_WIDTH = 512
NSA_WIDTH = 512
RWKV_HEADS = 8
NSA_HEADS = 8
KV_WIDTH = 128
SHIFT_COLS = 1792
NSA_COLS = 1304
NSA_COLS_PAD = 1408
CMP_BLOCK = 32
CMP_STRIDE = 16
CMP_HIDDEN = 128
SEL_BLOCK = 64
SEL_TOP_N = 16
WINDOW = 512
ROPE_THETA = 10000.0
NORM_EPS = 1e-6
GN_EPS = 64e-5
NEG_INF = -1e30
FORCE = 1e9
LANES = 128
RWKV_CHUNK = 64
Q_TILE = 128
VMEM_LIMIT = 56 * 1024 * 1024


def _cparams(sem):
    return pltpu.CompilerParams(dimension_semantics=sem, vmem_limit_bytes=VMEM_LIMIT)


def _split2(x):
    hi = x.astype(BF16)
    lo = (x - hi.astype(F32)).astype(BF16)
    return hi, lo


def _split3(x):
    hi = x.astype(BF16)
    r = x - hi.astype(F32)
    mid = r.astype(BF16)
    lo = (r - mid.astype(F32)).astype(BF16)
    return hi, mid, lo


def _dg(a, b, ca=1, cb=0):
    return lax.dot_general(a, b, (((ca,), (cb,)), ((), ())), preferred_element_type=F32)


def _mm(a, b, ca=1, cb=0):
    ah, al = _split2(a)
    bh, bl = _split2(b)
    return _dg(ah, bh, ca, cb) + _dg(ah, bl, ca, cb) + _dg(al, bh, ca, cb)


def _bf(x):
    return x.astype(BF16)


def _mm1(a, b, ca=1, cb=0):
    return _dg(_bf(a), _bf(b), ca, cb)


def _mm_w(ah, al, wh, wl):
    return _dg(ah, wh) + _dg(ah, wl) + _dg(al, wh)


def _mmp(a2, b2, ca=1, cb=0):
    return _dg(a2[0], b2[0], ca, cb) + _dg(a2[0], b2[1], ca, cb) + _dg(a2[1], b2[0], ca, cb)


def _mm_xr(a, e, ca=1, cb=0):
    a0, a1, a2 = _split3(a)
    return _dg(a0, e, ca, cb) + _dg(a1, e, ca, cb) + _dg(a2, e, ca, cb)


def _mm_xl(e, b, ca=1, cb=0):
    b0, b1, b2 = _split3(b)
    return _dg(e, b0, ca, cb) + _dg(e, b1, ca, cb) + _dg(e, b2, ca, cb)


def _iota(shape, dim):
    return lax.broadcasted_iota(jnp.int32, shape, dim)


def _head_blockdiag():
    return (_iota((LANES, LANES), 0) // HEAD_DIM == _iota((LANES, LANES), 1) // HEAD_DIM).astype(BF16)


def _segsum(x, bd):
    w = x.shape[1]
    outs = [_mm_xr(x[:, j * LANES:(j + 1) * LANES], bd) for j in range(w // LANES)]
    return outs[0] if len(outs) == 1 else jnp.concatenate(outs, axis=1)


def _sigmoid(x):
    return 1.0 / (1.0 + jnp.exp(-x))


def _norm_mod(x, g, sc, sh):
    ms = jnp.mean(x * x, axis=-1, keepdims=True)
    y = x * lax.rsqrt(ms + NORM_EPS) * g
    return y * (1.0 + sc) + sh


def _topk_mask(score, n, axis):
    size = score.shape[axis]
    pos = _iota(score.shape, axis).astype(F32)
    sel = jnp.zeros_like(score)
    sc = score
    for _ in range(n):
        m = jnp.max(sc, axis=axis, keepdims=True)
        idx = jnp.min(jnp.where(sc == m, pos, float(size)), axis=axis, keepdims=True)
        hit = pos == idx
        sel = jnp.where(hit & (m > 0.5 * NEG_INF), 1.0, sel)
        sc = jnp.where(hit, -3e38, sc)
    return sel


def _ada_kernel(c_ref, w_ref, b_ref, o_ref):
    c = c_ref[...]
    o_ref[0] = _mm(c * _sigmoid(c), w_ref[0]) + b_ref[0]


def _ada_mods(c_all, ada_w, ada_b):
    depth, d, n = ada_w.shape
    bc = c_all.shape[0]
    tn = 1024
    return pl.pallas_call(
        _ada_kernel,
        out_shape=jax.ShapeDtypeStruct((depth, bc, n), F32),
        grid=(depth, n // tn),
        in_specs=[pl.BlockSpec((bc, d), lambda l, j: (0, 0)),
                  pl.BlockSpec((1, d, tn), lambda l, j: (l, 0, j)),
                  pl.BlockSpec((1, 1, tn), lambda l, j: (l, 0, j))],
        out_specs=pl.BlockSpec((1, bc, tn), lambda l, j: (l, 0, j)),
        compiler_params=_cparams(("arbitrary", "arbitrary")),
        name="ada_mods",
    )(c_all, ada_w, ada_b.reshape(depth, 1, n))


def _in_proj_kernel(x_ref, g_ref, sc_ref, sh_ref, wr_ref, wn_ref, or_ref, on_ref):
    h = _norm_mod(x_ref[...], g_ref[...], sc_ref[0], sh_ref[0]).astype(BF16)
    or_ref[...] = _dg(h, wr_ref[...])
    on_ref[...] = _dg(h, wn_ref[...])


def _in_proj(x, g, sc, sh, wr, wn, tm, tpb):
    m, d = x.shape
    nr = wr.shape[1]
    nn = wn.shape[1]
    r = sc.shape[1]
    return pl.pallas_call(
        _in_proj_kernel,
        out_shape=[jax.ShapeDtypeStruct((m, nr), F32), jax.ShapeDtypeStruct((m, nn), F32)],
        grid=(m // tm,),
        in_specs=[pl.BlockSpec((tm, d), lambda i: (i, 0)),
                  pl.BlockSpec((1, d), lambda i: (0, 0)),
                  pl.BlockSpec((1, r, d), lambda i: (i // tpb, 0, 0)),
                  pl.BlockSpec((1, r, d), lambda i: (i // tpb, 0, 0)),
                  pl.BlockSpec((d, nr), lambda i: (0, 0)),
                  pl.BlockSpec((d, nn), lambda i: (0, 0))],
        out_specs=[pl.BlockSpec((tm, nr), lambda i: (i, 0)), pl.BlockSpec((tm, nn), lambda i: (i, 0))],
        compiler_params=_cparams(("arbitrary",)),
        name="in_proj",
    )(x, g, sc, sh, wr, wn)


def _out_proj_kernel(x_ref, a1_ref, a2_ref, gate_ref, w1, w2, o_ref):
    acc = _dg(a1_ref[...].astype(BF16), w1[...]) + _dg(a2_ref[...].astype(BF16), w2[...])
    o_ref[...] = x_ref[...] + gate_ref[0] * acc


def _out_proj(x, a1, a2, gate, w1, w2, tm, tpb):
    m, d = x.shape
    k1 = a1.shape[1]
    k2 = a2.shape[1]
    r = gate.shape[1]
    return pl.pallas_call(
        _out_proj_kernel,
        out_shape=jax.ShapeDtypeStruct((m, d), F32),
        grid=(m // tm,),
        in_specs=[pl.BlockSpec((tm, d), lambda i: (i, 0)),
                  pl.BlockSpec((tm, k1), lambda i: (i, 0)),
                  pl.BlockSpec((tm, k2), lambda i: (i, 0)),
                  pl.BlockSpec((1, r, d), lambda i: (i // tpb, 0, 0)),
                  pl.BlockSpec((k1, d), lambda i: (0, 0)),
                  pl.BlockSpec((k2, d), lambda i: (0, 0))],
        out_specs=pl.BlockSpec((tm, d), lambda i: (i, 0)),
        compiler_params=_cparams(("arbitrary",)),
        name="out_proj",
    )(x, a1, a2, gate, w1, w2)


def _ffn_kernel(x_ref, g_ref, sc_ref, sh_ref, gate_ref, wg, wu, wd, o_ref, h_s, acc_s):
    j = pl.program_id(1)

    @pl.when(j == 0)
    def _():
        h_s[...] = _norm_mod(x_ref[...], g_ref[...], sc_ref[0], sh_ref[0]).astype(BF16)
        acc_s[...] = jnp.zeros_like(acc_s)

    h = h_s[...]
    gate = _dg(h, wg[...])
    up = _dg(h, wu[...])
    act = gate * _sigmoid(gate) * up
    acc_s[...] += _dg(act.astype(BF16), wd[...])

    @pl.when(j == pl.num_programs(1) - 1)
    def _():
        o_ref[...] = x_ref[...] + gate_ref[0] * acc_s[...]


def _ffn(x, g, sc, sh, gate, w_up, w_down, tm, tpb):
    m, d = x.shape
    dff = w_down.shape[0]
    tf = 256
    nf = dff // tf
    r = sc.shape[1]
    mod = pl.BlockSpec((1, r, d), lambda i, j: (i // tpb, 0, 0))
    return pl.pallas_call(
        _ffn_kernel,
        out_shape=jax.ShapeDtypeStruct((m, d), F32),
        grid=(m // tm, nf),
        in_specs=[pl.BlockSpec((tm, d), lambda i, j: (i, 0)),
                  pl.BlockSpec((1, d), lambda i, j: (0, 0)),
                  mod, mod, mod,
                  pl.BlockSpec((d, tf), lambda i, j: (0, j)),
                  pl.BlockSpec((d, tf), lambda i, j: (0, j + nf)),
                  pl.BlockSpec((tf, d), lambda i, j: (j, 0))],
        out_specs=pl.BlockSpec((tm, d), lambda i, j: (i, 0)),
        scratch_shapes=[pltpu.VMEM((tm, d), BF16), pltpu.VMEM((tm, d), F32)],
        compiler_params=_cparams(("arbitrary", "arbitrary")),
        name="ffn",
    )(x, g, sc, sh, gate, w_up, w_up, w_down)


def _rwkv_prep_kernel(p_ref, prev_ref, mu_ref, wch, wcl, guh, gul, w0_ref, a0_ref, kk_ref, ka_ref,
                      r_o, ld_o, k_o, v_o, a_o, b_o, g_o, *scratch, seq_mode, tpb):
    p = p_ref[...]
    tm = p.shape[0]
    if seq_mode:
        carry = scratch[0]
        i = pl.program_id(0)

        @pl.when(i % tpb == 0)
        def _():
            carry[...] = prev_ref[0]

        rolled = pltpu.roll(p, 1, 0)
        prev = jnp.where(_iota((tm, 1), 0) == 0, carry[...], rolled)
        carry[...] = p[tm - 1:tm, :]
    else:
        prev = prev_ref[...]
    xs = p + mu_ref[...] * (prev - p)
    w_ = RWKV_WIDTH
    r = xs[:, 0:w_]
    k = xs[:, w_:2 * w_]
    v = xs[:, 2 * w_:3 * w_]
    z = xs[:, 3 * w_:3 * w_ + LANES]
    gd = xs[:, 3 * w_ + LANES:3 * w_ + 2 * LANES]
    z = jnp.where(_iota((tm, LANES), 1) < 64, jnp.tanh(z), z)
    zh, zl = _split2(z)
    wa = _mm_w(zh, zl, wch[...], wcl[...])
    zw = -(w0_ref[...] + wa[:, 0:w_])
    softplus = jnp.maximum(zw, 0.0) + jnp.log1p(jnp.exp(-jnp.abs(zw)))
    wlog = -softplus - 0.5
    a_sig = _sigmoid(a0_ref[...] + wa[:, w_:2 * w_])
    sh_, sl_ = _split2(_sigmoid(gd))
    g = _mm_w(sh_, sl_, guh[...], gul[...])
    bd = _head_blockdiag()
    kkv = k * kk_ref[...]
    nrm = jnp.sqrt(_segsum(kkv * kkv, bd))
    kkn = kkv / jnp.maximum(nrm, 1e-12)
    r_o[...] = r
    ld_o[...] = -jnp.exp(wlog)
    k_o[...] = k * (1.0 + (a_sig - 1.0) * ka_ref[...])
    v_o[...] = v
    a_o[...] = -kkn
    b_o[...] = kkn * a_sig
    g_o[...] = g


def _rwkv_prep(p_r, prev, lw, tm, tpb, seq_mode):
    m = p_r.shape[0]
    w_ = RWKV_WIDTH
    row = lambda n: pl.BlockSpec((1, n), lambda i: (0, 0))
    full = lambda a: pl.BlockSpec(a.shape, lambda i: (0, 0))
    if seq_mode:
        prev_spec = pl.BlockSpec((1, 1, SHIFT_COLS), lambda i: (i // tpb, 0, 0))
        scratch = [pltpu.VMEM((1, SHIFT_COLS), F32)]
    else:
        prev_spec = pl.BlockSpec((tm, SHIFT_COLS), lambda i: (i, 0))
        scratch = []
    out = jax.ShapeDtypeStruct((m, w_), F32)
    ospec = pl.BlockSpec((tm, w_), lambda i: (i, 0))
    return pl.pallas_call(
        functools.partial(_rwkv_prep_kernel, seq_mode=seq_mode, tpb=tpb),
        out_shape=[out] * 7,
        grid=(m // tm,),
        in_specs=[pl.BlockSpec((tm, SHIFT_COLS), lambda i: (i, 0)), prev_spec, row(SHIFT_COLS),
                  full(lw['wc_h']), full(lw['wc_l']), full(lw['gu_h']), full(lw['gu_l']),
                  row(w_), row(w_), row(w_), row(w_)],
        out_specs=[ospec] * 7,
        scratch_shapes=scratch,
        compiler_params=_cparams(("arbitrary",)),
        name="rwkv_prep",
    )(p_r, prev, lw['mu'], lw['wc_h'], lw['wc_l'], lw['gu_h'], lw['gu_l'], lw['w0'], lw['a0'], lw['k_k'], lw['k_a'])


def _rwkv_chunk_kernel(r_ref, ld_ref, k_ref, v_ref, a_ref, b_ref, q1_o, y0_o, g_o, ha_o):
    c = RWKV_CHUNK
    c2 = 2 * c
    nb = r_ref.shape[0]
    tri = (_iota((c, c), 0) >= _iota((c, c), 1)).astype(BF16)
    at, rt, bt, kt, bh, kh, v, p_end = ([] for _ in range(8))
    for bi in range(nb):
        ld = ld_ref[bi]
        lc = _mm_xl(tri, ld)
        lend = lc[c - 1:c, :]
        inv_p = jnp.exp(-lc)
        p_to_end = jnp.exp(lend - lc)
        p_end.append(jnp.exp(lend))
        v.append(v_ref[bi])
        rt.append(r_ref[bi] * jnp.exp(lc))
        at.append(a_ref[bi] * jnp.exp(lc - ld))
        bt.append(b_ref[bi] * inv_p)
        kt.append(k_ref[bi] * inv_p)
        bh.append(b_ref[bi] * p_to_end)
        kh.append(k_ref[bi] * p_to_end)
    row = _iota((c2, c2), 0)
    col = _iota((c2, c2), 1)
    strict = row % c > col % c
    incl = row % c >= col % c
    eye = row == col
    lo = _iota((c, LANES), 1) < HEAD_DIM

    def pair_rows(x):
        return jnp.concatenate([jnp.where(lo, x, 0.0), jnp.where(lo, 0.0, x)], axis=0)

    chains = [(bi, slice(LANES * j, LANES * (j + 1))) for bi in range(nb) for j in range(RWKV_WIDTH // LANES)]
    pairs = range(len(chains))
    at2 = [pair_rows(at[bi][:, sl]) for bi, sl in chains]
    rt2 = [pair_rows(rt[bi][:, sl]) for bi, sl in chains]
    v2s = [_bf(pair_rows(v[bi][:, sl])) for bi, sl in chains]
    lhs = [_bf(jnp.concatenate([at2[j], rt2[j]], axis=0)) for j in pairs]
    xb = [_dg(lhs[j], _bf(pair_rows(bt[bi][:, sl])), 1, 1) for j, (bi, sl) in enumerate(chains)]
    xk = [_dg(lhs[j], _bf(pair_rows(kt[bi][:, sl])), 1, 1) for j, (bi, sl) in enumerate(chains)]
    a_ab = [jnp.where(strict, xb[j][0:c2], 0.0) for j in pairs]
    a_rb = [_bf(jnp.where(incl, xb[j][c2:2 * c2], 0.0)) for j in pairs]
    a_ak = [_bf(jnp.where(strict, xk[j][0:c2], 0.0)) for j in pairs]
    a_rk = [_bf(jnp.where(incl, xk[j][c2:2 * c2], 0.0)) for j in pairs]
    akv = [_dg(a_ak[j], v2s[j]) for j in pairs]
    x = [jnp.where(eye, 1.0, 0.0) + a_ab[j] for j in pairs]
    mp = a_ab
    for _ in range(int(np.log2(c)) - 1):
        mps = [_bf(mp[j]) for j in pairs]
        mp = [_dg(mps[j], mps[j]) for j in pairs]
        x = [x[j] + _dg(_bf(mp[j]), _bf(x[j])) for j in pairs]
    wu = [_bf(_dg(_bf(x[j]), _bf(jnp.concatenate([at2[j], akv[j]], axis=1)))) for j in pairs]
    qy = [_dg(a_rb[j], wu[j]) for j in pairs]
    ark_v = [_dg(a_rk[j], v2s[j]) for j in pairs]
    gh = [_dg(_bf(pair_rows(bh[bi][:, sl])), wu[j], 0, 0) for j, (bi, sl) in enumerate(chains)]
    khv = [_dg(_bf(pair_rows(kh[bi][:, sl])), v2s[j], 0, 0) for j, (bi, sl) in enumerate(chains)]
    for j, (bi, sl) in enumerate(chains):
        q1 = rt2[j] + qy[j][:, 0:LANES]
        y0 = qy[j][:, LANES:2 * LANES] + ark_v[j]
        q1_o[bi, :, sl] = q1[0:c] + q1[c:c2]
        y0_o[bi, :, sl] = y0[0:c] + y0[c:c2]
        g_o[bi, 0, :, sl] = jnp.where(eye, p_end[bi][:, sl], 0.0) + gh[j][:, 0:LANES]
        ha_o[bi, 0, :, sl] = gh[j][:, LANES:2 * LANES] + khv[j]


def _rwkv_chunk(r, ld, k, v, a, b):
    bsz, t, w_ = r.shape
    c = RWKV_CHUNK
    nch = t // c
    nb = 2 if bsz % 2 == 0 else 1
    ispec = pl.BlockSpec((nb, c, w_), lambda i, j: (i, j, 0))
    sspec = pl.BlockSpec((nb, 1, 2 * HEAD_DIM, w_), lambda i, j: (i, j, 0, 0))
    return pl.pallas_call(
        _rwkv_chunk_kernel,
        out_shape=[jax.ShapeDtypeStruct((bsz, t, w_), F32)] * 2
                  + [jax.ShapeDtypeStruct((bsz, nch, 2 * HEAD_DIM, w_), F32)] * 2,
        grid=(bsz // nb, nch),
        in_specs=[ispec] * 6,
        out_specs=[ispec, ispec, sspec, sspec],
        compiler_params=_cparams(("arbitrary", "arbitrary")),
        name="rwkv_chunk",
    )(r, ld, k, v, a, b)


def _rwkv_scan_kernel(q1_ref, y0_ref, g_ref, ha_ref, y_o, hf_o, h_s):
    j = pl.program_id(0)

    @pl.when(j == 0)
    def _():
        h_s[...] = jnp.zeros_like(h_s)

    c = RWKV_CHUNK
    cps = g_ref.shape[1]
    src = j % 2
    dst = (j + 1) % 2
    chains = [(bi, slice(LANES * p, LANES * (p + 1)))
              for bi in range(q1_ref.shape[0]) for p in range(RWKV_WIDTH // LANES)]
    h = [h_s[src, bi, :, sl] for bi, sl in chains]
    for cc in range(cps):
        rows = slice(c * cc, c * (cc + 1))
        for n, (bi, sl) in enumerate(chains):
            h0 = _split2(h[n])
            y_o[bi, rows, sl] = _mmp(_split2(q1_ref[bi, rows, sl]), h0) + y0_ref[bi, rows, sl]
            h[n] = _mmp(_split2(g_ref[bi, cc, :, sl]), h0) + ha_ref[bi, cc, :, sl]
    for n, (bi, sl) in enumerate(chains):
        h_s[dst, bi, :, sl] = h[n]

    @pl.when(j == pl.num_programs(0) - 1)
    def _():
        for n, (bi, sl) in enumerate(chains):
            hf_o[bi, :, sl] = h[n][0:HEAD_DIM] + h[n][HEAD_DIM:2 * HEAD_DIM]


def _rwkv_scan(q1, y0, g, ha):
    bsz, t, w_ = q1.shape
    nch = t // RWKV_CHUNK
    cps = 4 if nch % 4 == 0 else 1
    c = cps * RWKV_CHUNK
    ispec = pl.BlockSpec((bsz, c, w_), lambda j: (0, j, 0))
    sspec = pl.BlockSpec((bsz, cps, 2 * HEAD_DIM, w_), lambda j: (0, j, 0, 0))
    return pl.pallas_call(
        _rwkv_scan_kernel,
        out_shape=[jax.ShapeDtypeStruct((bsz, t, w_), F32), jax.ShapeDtypeStruct((bsz, HEAD_DIM, w_), F32)],
        grid=(t // c,),
        in_specs=[ispec, ispec, sspec, sspec],
        out_specs=[ispec, pl.BlockSpec((bsz, HEAD_DIM, w_), lambda j: (0, 0, 0))],
        scratch_shapes=[pltpu.VMEM((2, bsz, 2 * HEAD_DIM, w_), F32)],
        compiler_params=_cparams(("arbitrary",)),
        name="rwkv_scan",
    )(q1, y0, g, ha)


def _rwkv_step_kernel(s_ref, r_ref, ld_ref, k_ref, a_ref, b_ref, v_ref, so_ref, y_ref):
    s = s_ref[...]
    sa = jnp.sum(s * a_ref[...], axis=-1, keepdims=True)
    s_new = s * jnp.exp(ld_ref[...]) + sa * b_ref[...] + v_ref[...] * k_ref[...]
    so_ref[...] = s_new
    y_ref[...] = jnp.sum(s_new * r_ref[...], axis=-1, keepdims=True)


def _rwkv_step(state, layer, r, ld, k, a, b, v):
    bs = r.shape[0]
    nh, hd = RWKV_HEADS, HEAD_DIM
    bt = min(8, bs)
    rowv = lambda x: x.reshape(bs, nh, 1, hd)
    vspec = pl.BlockSpec((bt, nh, 1, hd), lambda i: (i, 0, 0, 0))
    cspec = pl.BlockSpec((bt, nh, hd, 1), lambda i: (i, 0, 0, 0))
    off = layer * (bs // bt)
    return pl.pallas_call(
        _rwkv_step_kernel,
        out_shape=[jax.ShapeDtypeStruct((bs, nh, hd, hd), F32), jax.ShapeDtypeStruct((bs, nh, hd, 1), F32)],
        grid=(bs // bt,),
        in_specs=[pl.BlockSpec((bt, nh, hd, hd), lambda i: (i + off, 0, 0, 0)),
                  vspec, vspec, vspec, vspec, vspec, cspec],
        out_specs=[pl.BlockSpec((bt, nh, hd, hd), lambda i: (i, 0, 0, 0)), cspec],
        compiler_params=_cparams(("arbitrary",)),
        name="rwkv_step",
    )(state, rowv(r), rowv(ld), rowv(k), rowv(a), rowv(b), v.reshape(bs, nh, hd, 1))


def _rwkv_post_kernel(y_ref, r_ref, k_ref, v_ref, g_ref, rk_ref, lw_ref, lb_ref, o_ref):
    bd = _head_blockdiag()
    y = y_ref[...]
    inv = 1.0 / HEAD_DIM
    mu = _segsum(y, bd) * inv
    d = y - mu
    var = _segsum(d * d, bd) * inv
    yn = d * lax.rsqrt(var + GN_EPS) * lw_ref[...] + lb_ref[...]
    v = v_ref[...]
    bonus = _segsum(r_ref[...] * k_ref[...] * rk_ref[...], bd) * v
    o_ref[...] = (yn + bonus) * g_ref[...]


def _rwkv_post(y, r, k, v, g, lw, tm):
    m, w_ = y.shape
    tile = pl.BlockSpec((tm, w_), lambda i: (i, 0))
    row = pl.BlockSpec((1, w_), lambda i: (0, 0))
    return pl.pallas_call(
        _rwkv_post_kernel,
        out_shape=jax.ShapeDtypeStruct((m, w_), F32),
        grid=(m // tm,),
        in_specs=[tile] * 5 + [row] * 3,
        out_specs=tile,
        compiler_params=_cparams(("arbitrary",)),
        name="rwkv_post",
    )(y, r, k, v, g, lw['r_k'], lw['ln_w'], lw['ln_b'])


def _nsa_proj_kernel(p_ref, cos_ref, sin_ref, nw_ref, q_o, qr_o, cmp_o, sel_o, win_o, gt_o, selk_o, selv_o, winb_o):
    tm = p_ref.shape[0]
    cos = cos_ref[...]
    sin = sin_ref[...]
    bd = _head_blockdiag()
    first = (_iota((tm, LANES), 1) % HEAD_DIM) < HEAD_DIM // 2
    nw = nw_ref[...]

    def norm(x, w):
        ms = _segsum(x * x, bd) * (1.0 / HEAD_DIM)
        return x * lax.rsqrt(ms + NORM_EPS) * w

    def rope(x):
        rot = jnp.where(first, pltpu.roll(x, LANES - HEAD_DIM // 2, 1), pltpu.roll(x, HEAD_DIM // 2, 1))
        return x * cos + rot * sin

    for j in range(NSA_WIDTH // LANES):
        sl = slice(LANES * j, LANES * (j + 1))
        xn = norm(p_ref[:, sl], nw[0:1])
        q_o[:, sl] = xn
        qr_o[:, sl] = rope(xn)
    o = NSA_WIDTH
    cmp_o[...] = p_ref[:, o:o + 2 * KV_WIDTH]
    ks = rope(norm(p_ref[:, o + 256:o + 384], nw[2:3]))
    vs = p_ref[:, o + 384:o + 512]
    kw = rope(norm(p_ref[:, o + 512:o + 640], nw[3:4]))
    vw = p_ref[:, o + 640:o + 768]
    sel_o[:, 0:LANES] = ks
    sel_o[:, LANES:2 * LANES] = vs
    win_o[:, 0:LANES] = kw
    win_o[:, LANES:2 * LANES] = vw
    gt_o[...] = _sigmoid(p_ref[:, o + 768:o + 896])
    lo = _iota((tm, LANES), 1) < HEAD_DIM
    selk_o[...] = ks.astype(BF16)
    selv_o[:, 0:LANES] = jnp.where(lo, vs, 1.0).astype(BF16)
    selv_o[:, LANES:2 * LANES] = jnp.where(lo, 1.0, vs).astype(BF16)
    winb_o[:, 0:LANES] = kw.astype(BF16)
    winb_o[:, LANES:2 * LANES] = vw.astype(BF16)


def _nsa_proj(p_n, cos_t, sin_t, nw, tm, tpb):
    m = p_n.shape[0]
    tt = cos_t.shape[0] // tm
    tab = pl.BlockSpec((tm, LANES), lambda i: (i % tt, 0))
    shapes = [(m, NSA_WIDTH), (m, NSA_WIDTH), (m, 256), (m, 256), (m, 256), (m, LANES),
              (m, LANES), (m, 256), (m, 256)]
    dtypes = [F32] * 6 + [BF16] * 3
    return pl.pallas_call(
        _nsa_proj_kernel,
        out_shape=[jax.ShapeDtypeStruct(s, dt) for s, dt in zip(shapes, dtypes)],
        grid=(m // tm,),
        in_specs=[pl.BlockSpec((tm, NSA_COLS_PAD), lambda i: (i, 0)), tab, tab,
                  pl.BlockSpec((4, LANES), lambda i: (0, 0))],
        out_specs=[pl.BlockSpec((tm, s[1]), lambda i: (i, 0)) for s in shapes],
        compiler_params=_cparams(("arbitrary",)),
        name="nsa_proj",
    )(p_n, cos_t, sin_t, nw)


def _cmp1_kernel(*refs, n_parts, paged):
    if paged:
        refs = refs[1:]
    parts = refs[:n_parts]
    w_ref, o_ref, xs = refs[n_parts:]
    gpp = parts[0].shape[1] // CMP_STRIDE
    for j, pr in enumerate(parts):
        for l in range(CMP_STRIDE):
            xs[gpp * j:gpp * (j + 1), LANES * l:LANES * (l + 1)] = pr[0, pl.ds(l, gpp, stride=CMP_STRIDE), :]
    res = _dg(_bf(xs[...]), w_ref[0])
    ng = o_ref.shape[2]
    for s in range(o_ref.shape[0]):
        o_ref[s, 0] = res[ng * s:ng * (s + 1)]


def _cmp1_prompt(cmp_rows, w):
    bsz, t, _ = cmp_rows.shape
    rows = min(2048, t)
    ng = rows // CMP_STRIDE
    kdim = CMP_STRIDE * LANES
    return pl.pallas_call(
        functools.partial(_cmp1_kernel, n_parts=1, paged=False),
        out_shape=jax.ShapeDtypeStruct((bsz, 2, t // CMP_STRIDE, 512), F32),
        grid=(bsz, t // rows, 2),
        in_specs=[pl.BlockSpec((1, rows, LANES), lambda b, i, kv: (b, i, kv)),
                  pl.BlockSpec((1, kdim, 512), lambda b, i, kv: (kv, 0, 0))],
        out_specs=pl.BlockSpec((1, 1, ng, 512), lambda b, i, kv: (b, kv, i, 0)),
        scratch_shapes=[pltpu.VMEM((ng, kdim), F32)],
        compiler_params=_cparams(("arbitrary", "arbitrary", "arbitrary")),
        name="cmp1_prompt",
    )(cmp_rows, w)


def _cmp1_sample(page_table, cache, page_off, w):
    bs, n_pages = page_table.shape
    page = cache.shape[1]
    ng = n_pages * page // CMP_STRIDE
    kdim = CMP_STRIDE * LANES
    sps = 4 if bs % 4 == 0 else 1

    def page_spec(s, j):
        return pl.BlockSpec((1, page, LANES), lambda b, kv, pt: (pt[b * sps + s, j] + page_off, 0, kv))

    return pl.pallas_call(
        functools.partial(_cmp1_kernel, n_parts=sps * n_pages, paged=True),
        out_shape=jax.ShapeDtypeStruct((bs, 2, ng, 512), F32),
        grid_spec=pltpu.PrefetchScalarGridSpec(
            num_scalar_prefetch=1,
            grid=(bs // sps, 2),
            in_specs=[page_spec(s, j) for s in range(sps) for j in range(n_pages)] + [
                pl.BlockSpec((1, kdim, 512), lambda b, kv, pt: (kv, 0, 0))],
            out_specs=pl.BlockSpec((sps, 1, ng, 512), lambda b, kv, pt: (b, kv, 0, 0)),
            scratch_shapes=[pltpu.VMEM((sps * ng, kdim), F32)]),
        compiler_params=_cparams(("arbitrary", "arbitrary")),
        name="cmp1_sample",
    )(page_table, *([cache] * (sps * n_pages)), w)


def _cmp_bias_kernel(pos_ref, w1_ref, b1_ref, o_ref):
    posb = jnp.broadcast_to(pos_ref[0], (8, pos_ref.shape[2]))
    o_ref[0] = _mm(posb, w1_ref[0])[0:1] + b1_ref[0]


def _cmp_bias(lw):
    kdim = CMP_BLOCK * HEAD_DIM
    return pl.pallas_call(
        _cmp_bias_kernel,
        out_shape=jax.ShapeDtypeStruct((2, 1, CMP_HIDDEN), F32),
        grid=(2,),
        in_specs=[pl.BlockSpec((1, 1, kdim), lambda kv: (kv, 0, 0)),
                  pl.BlockSpec((1, kdim, CMP_HIDDEN), lambda kv: (kv, 0, 0)),
                  pl.BlockSpec((1, 1, CMP_HIDDEN), lambda kv: (kv, 0, 0))],
        out_specs=pl.BlockSpec((1, 1, CMP_HIDDEN), lambda kv: (kv, 0, 0)),
        compiler_params=_cparams(("arbitrary",)),
        name="cmp_bias",
    )(lw['cmp_pos'], lw['cmp_w1'], lw['cmp_b1'])


def _cmp2_kernel(ab_ref, c1_ref, w2_ref, b2_ref, nw_ref, o_ref):
    kv = pl.program_id(1)
    spb, _, ng, width = ab_ref.shape
    ab = ab_ref[:, 0].reshape(spb * ng, width)
    a_part = ab[:, 0:256]
    b_next = pltpu.roll(ab[:, 256:512], spb * ng - 1, 0)
    c1 = c1_ref[0]
    h = a_part + b_next + jnp.concatenate([c1, c1], axis=1)
    h = 0.5 * h * (1.0 + jnp.tanh(np.sqrt(2.0 / np.pi) * (h + 0.044715 * (h * h * h))))
    out = _mm1(h, w2_ref[0]) + b2_ref[0]
    ms = _segsum(out * out, _head_blockdiag()) * (1.0 / HEAD_DIM)
    normed = out * lax.rsqrt(ms + NORM_EPS) * nw_ref[...]
    o_ref[:, 0] = jnp.where(kv == 0, normed, out).reshape(spb, ng, LANES)


def _cmp2(ab, c1, lw):
    bsz, _, ng, _ = ab.shape
    spb = 8 if bsz % 8 == 0 else 1
    return pl.pallas_call(
        _cmp2_kernel,
        out_shape=jax.ShapeDtypeStruct((bsz, 2, ng, LANES), F32),
        grid=(bsz // spb, 2),
        in_specs=[pl.BlockSpec((spb, 1, ng, 512), lambda b, kv: (b, kv, 0, 0)),
                  pl.BlockSpec((1, 1, CMP_HIDDEN), lambda b, kv: (kv, 0, 0)),
                  pl.BlockSpec((1, 2 * CMP_HIDDEN, LANES), lambda b, kv: (kv, 0, 0)),
                  pl.BlockSpec((1, 1, LANES), lambda b, kv: (kv, 0, 0)),
                  pl.BlockSpec((1, LANES), lambda b, kv: (0, 0))],
        out_specs=pl.BlockSpec((spb, 1, ng, LANES), lambda b, kv: (b, kv, 0, 0)),
        compiler_params=_cparams(("arbitrary", "arbitrary")),
        name="cmp2",
    )(ab, c1, lw['cmp_w2bd'], lw['cmp_b2t'], lw['nw_kc'])


def _masked_exp(s, mask):
    s = jnp.where(mask, s, NEG_INF)
    mx = jnp.maximum(jnp.max(s, axis=-1, keepdims=True), 0.1 * NEG_INF)
    e = jnp.exp(s - mx)
    den = jnp.sum(e, axis=-1, keepdims=True)
    return e, 1.0 / jnp.maximum(den, 1e-30)


def _masked_softmax(s, mask):
    e, rden = _masked_exp(s, mask)
    return e * rden


def _selection_mask(imp, qpos, axis):
    blk = _iota(imp.shape, axis)
    cur = qpos // SEL_BLOCK
    valid = blk * SEL_BLOCK <= qpos
    forced = (blk == 0) | (blk == cur) | (blk == cur - 1)
    score = jnp.where(valid, jnp.where(forced, FORCE, imp), NEG_INF)
    return _topk_mask(score, SEL_TOP_N, axis)


def _attn_prompt_kernel(q_ref, qr_ref, gt_ref, kc_ref, vc_ref, ks_ref, vs0_ref, vs1_ref, *rest, ck):
    nwb = WINDOW // Q_TILE + 1
    kw_refs = rest[0:nwb]
    vw_refs = rest[nwb:2 * nwb]
    ovt_ref, en_ref, o_ref, qg_s, qrg_s, m_s, acc_s, sc_s = rest[2 * nwb:]
    qi = pl.program_id(1)
    tq = Q_TILE
    hpg = NSA_HEADS // 2
    scale = HEAD_DIM ** -0.5
    ng = kc_ref.shape[2]
    lo_lanes = _iota((tq, LANES), 1) < HEAD_DIM
    qpos = qi * tq + _iota((tq, 1), 0)
    qpos_t = qi * tq + _iota((1, tq), 1)
    qpos_all = jnp.concatenate([qpos] * NSA_HEADS, axis=0)
    gt = gt_ref[0]
    kc = kc_ref[0, 0]
    vc = vc_ref[0, 0]
    kw = jnp.concatenate([r[0] for r in kw_refs], axis=0)
    vw = jnp.concatenate([r[0] for r in vw_refs], axis=0)
    kwpos = jnp.concatenate([(qi - (nwb - 1) + j) * tq + _iota((1, tq), 1) for j in range(nwb)], axis=1)
    nch = (qi * tq + tq + ck - 1) // ck
    rows = hpg * tq
    for h in range(NSA_HEADS):
        g = h // hpg
        cb = h // 2
        rs = slice(tq * h, tq * (h + 1))
        keep = lo_lanes if g == 0 else jnp.logical_not(lo_lanes)
        blk = q_ref[0, :, LANES * cb:LANES * (cb + 1)]
        blkr = qr_ref[0, :, LANES * cb:LANES * (cb + 1)]
        if (h % 2 == 0) != (g == 0):
            blk = pltpu.roll(blk, HEAD_DIM, 1)
            blkr = pltpu.roll(blkr, HEAD_DIM, 1)
        qg_s[rs, :] = (jnp.where(keep, blk, 0.0) * scale).astype(BF16)
        qrg_s[rs, :] = (jnp.where(keep, blkr, 0.0) * scale).astype(BF16)
    qrg = qrg_s[...]
    kcb = _bf(kc)
    s = _dg(qg_s[...], kcb, 1, 1)
    mc = (_iota((2 * rows, ng), 1) * CMP_STRIDE + (CMP_BLOCK - 1)) <= qpos_all
    p = _masked_softmax(s, mc)
    oc = _mm1(p, vc)
    nselb = []
    for g in range(2):
        b0 = rows * g
        psum = p[b0:b0 + tq] + p[b0 + tq:b0 + 2 * tq] + p[b0 + 2 * tq:b0 + 3 * tq] + p[b0 + 3 * tq:b0 + 4 * tq]
        imp_t = _mm_xl(ovt_ref[...], psum.T)
        sel_t = _selection_mask(imp_t, qpos_t, 0)
        nselb.append((1.0 - sel_t).T.astype(BF16))
    m_s[...] = jnp.full(m_s.shape, 0.1 * NEG_INF, F32)
    acc_s[...] = jnp.zeros(acc_s.shape, F32)

    def scores(c, slot):
        st = pl.multiple_of(c * ck, ck)
        sc_s[slot] = _dg(qrg, ks_ref[0, pl.ds(st, ck), :], 1, 1)

    def span(c, slot, off, width, diagonal):
        st = pl.multiple_of(c * ck + off, width)
        causal = jnp.where(st + _iota((tq, width), 1) <= qpos, 0.0, NEG_INF) if diagonal else None
        for g in range(2):
            vs_ref = vs0_ref if g == 0 else vs1_ref
            vch = vs_ref[0, pl.ds(st, width), :]
            bias = _dg(nselb[g], en_ref[c, :, off:off + width])
            if diagonal:
                bias = bias + causal
            for hh in range(hpg):
                rs = slice(rows * g + tq * hh, rows * g + tq * (hh + 1))
                sh = sc_s[slot, rs, off:off + width] + bias
                m_old = m_s[rs, :]
                m_new = jnp.maximum(m_old, jnp.max(sh, axis=-1, keepdims=True))
                alpha = jnp.exp(m_old - m_new)
                pe = jnp.exp(sh - m_new)
                acc_s[rs, :] = alpha * acc_s[rs, :] + _dg(pe.astype(BF16), vch)
                m_s[rs, :] = m_new

    def chunk(c, slot, diagonal):
        if not diagonal:
            span(c, slot, 0, ck, False)
            return
        half = ck // 2
        span(c, slot, 0, half, True)

        @pl.when(qi * tq + tq - 1 >= c * ck + half)
        def _():
            span(c, slot, half, half, True)

    def body(c, carry):
        scores(c + 1, (c + 1) % 2)
        chunk(c, c % 2, False)
        return carry

    scores(0, 0)
    lax.fori_loop(0, nch - 1, body, 0)
    chunk(nch - 1, (nch - 1) % 2, True)
    acc = acc_s[...]
    osel = acc / pltpu.roll(acc, HEAD_DIM, 1)
    sw = _dg(qrg, kw, 1, 1)
    mw = (kwpos <= qpos_all) & (kwpos > qpos_all - WINDOW) & (kwpos >= 0)
    ew, rw = _masked_exp(sw, mw)
    ow = _dg(ew.astype(BF16), vw) * rw
    for h in range(NSA_HEADS):
        g = h // hpg
        rs = slice(tq * h, tq * (h + 1))
        og = (gt[:, 3 * h:3 * h + 1] * oc[rs] + gt[:, 3 * h + 1:3 * h + 2] * osel[rs]
              + gt[:, 3 * h + 2:3 * h + 3] * ow[rs])
        o_ref[0, :, HEAD_DIM * h:HEAD_DIM * (h + 1)] = og[:, HEAD_DIM * g:HEAD_DIM * (g + 1)]


def _overlap_matrix(ng):
    s = np.arange(ng)[:, None] * CMP_STRIDE
    j = np.arange(LANES)[None, :] * SEL_BLOCK
    return ((s < j + SEL_BLOCK) & (s + CMP_BLOCK > j)).astype(np.float32)


def _block_bias_matrix(t, ck):
    key_blk = (np.arange(t) // SEL_BLOCK).reshape(t // ck, 1, ck)
    blk = np.arange(LANES).reshape(1, LANES, 1)
    return jnp.asarray(np.where(key_blk == blk, NEG_INF, 0.0).astype(np.float32), dtype=BF16)


def _attn_prompt(q, qr, gt, ckv, selk, selv, win):
    bsz, t, _ = q.shape
    tq = Q_TILE
    ng = ckv.shape[2]
    ck = min(1024, t)
    nwb = WINDOW // tq + 1
    hpg = NSA_HEADS // 2
    assert t // SEL_BLOCK <= LANES and t % ck == 0
    ovt = jnp.asarray(_overlap_matrix(ng).T, dtype=BF16)

    def wspec(j, col):
        return pl.BlockSpec((1, tq, LANES), lambda b, i: (b, jnp.maximum(i - (nwb - 1) + j, 0), col))

    tile = lambda w: pl.BlockSpec((1, tq, w), lambda b, i: (b, i, 0))
    return pl.pallas_call(
        functools.partial(_attn_prompt_kernel, ck=ck),
        out_shape=jax.ShapeDtypeStruct((bsz, t, NSA_WIDTH), F32),
        grid=(bsz, t // tq),
        in_specs=[tile(NSA_WIDTH), tile(NSA_WIDTH), tile(LANES),
                  pl.BlockSpec((1, 1, ng, LANES), lambda b, i: (b, 0, 0, 0)),
                  pl.BlockSpec((1, 1, ng, LANES), lambda b, i: (b, 1, 0, 0)),
                  pl.BlockSpec((1, t, LANES), lambda b, i: (b, 0, 0)),
                  pl.BlockSpec((1, t, LANES), lambda b, i: (b, 0, 0)),
                  pl.BlockSpec((1, t, LANES), lambda b, i: (b, 0, 1))]
                 + [wspec(j, 0) for j in range(nwb)] + [wspec(j, 1) for j in range(nwb)]
                 + [pl.BlockSpec((LANES, ng), lambda b, i: (0, 0)),
                    pl.BlockSpec((t // ck, LANES, ck), lambda b, i: (0, 0, 0))],
        out_specs=tile(NSA_WIDTH),
        scratch_shapes=[pltpu.VMEM((NSA_HEADS * tq, LANES), BF16), pltpu.VMEM((NSA_HEADS * tq, LANES), BF16),
                        pltpu.VMEM((NSA_HEADS * tq, 1), F32), pltpu.VMEM((NSA_HEADS * tq, LANES), F32),
                        pltpu.VMEM((2, NSA_HEADS * tq, ck), F32)],
        compiler_params=_cparams(("arbitrary", "arbitrary")),
        name="attn_prompt",
    )(q, qr, gt, ckv, ckv, selk, selv, selv, *([win] * (2 * nwb)), ovt, _block_bias_matrix(t, ck))


def _sample_cmp_kernel(q_ref, ckv_ref, ov_ref, oc_ref, imp_ref, *, past_len):
    nh = NSA_HEADS
    scale = HEAD_DIM ** -0.5
    ng = ckv_ref.shape[2]
    mc = (_iota((nh, ng), 1) * CMP_STRIDE + (CMP_BLOCK - 1)) <= past_len
    same_group = (_iota((nh, nh), 0) // (nh // 2) == _iota((nh, nh), 1) // (nh // 2)).astype(BF16)
    for i in range(q_ref.shape[0]):
        s = _mm1(q_ref[i], ckv_ref[i, 0], 1, 1) * scale
        p = _masked_softmax(s, mc)
        oc_ref[i] = _mm1(p, ckv_ref[i, 1])
        imp_ref[i] = _mm_xr(_mm_xl(same_group, p), ov_ref[...])


def _sample_cmp(q, ckv, past_len):
    bs, nh, _ = q.shape
    ng = ckv.shape[2]
    spb = 8 if bs % 8 == 0 else 1
    vec = pl.BlockSpec((spb, nh, LANES), lambda b: (b, 0, 0))
    return pl.pallas_call(
        functools.partial(_sample_cmp_kernel, past_len=past_len),
        out_shape=[jax.ShapeDtypeStruct((bs, nh, LANES), F32)] * 2,
        grid=(bs // spb,),
        in_specs=[vec, pl.BlockSpec((spb, 2, ng, LANES), lambda b: (b, 0, 0, 0)),
                  pl.BlockSpec((ng, LANES), lambda b: (0, 0))],
        out_specs=[vec, vec],
        compiler_params=_cparams(("arbitrary",)),
        name="sample_cmp",
    )(q, ckv, jnp.asarray(_overlap_matrix(ng), dtype=BF16))


def _sample_topk_kernel(imp_ref, sel_ref, *, past_len):
    sel_ref[...] = _selection_mask(imp_ref[...], past_len, 1).astype(BF16)


def _sample_topk(imp, past_len):
    rows = imp.shape[0]
    return pl.pallas_call(
        functools.partial(_sample_topk_kernel, past_len=past_len),
        out_shape=jax.ShapeDtypeStruct((rows, LANES), BF16),
        grid=(1,),
        in_specs=[pl.BlockSpec((rows, LANES), lambda i: (0, 0))],
        out_specs=pl.BlockSpec((rows, LANES), lambda i: (0, 0)),
        compiler_params=_cparams(("arbitrary",)),
        name="sample_topk",
    )(imp)


def _attn_sample_kernel(pt_ref, qr_ref, gt_ref, oc_ref, selb_ref, *rest, past_len, n_pages):
    pages = rest[:n_pages]
    nsel_ref, win_ref, nwin_ref, o_ref, wout_ref = rest[n_pages:]
    scale = HEAD_DIM ** -0.5
    qpos = past_len
    qr = qr_ref[0]
    gt = gt_ref[0]
    oc = oc_ref[0]
    selb = selb_ref[0]
    page = pages[0].shape[1]
    nsel = nsel_ref[0]
    qrb = qr.astype(BF16)
    s_new = jnp.sum(qr * nsel[:, 0:LANES], axis=-1, keepdims=True) * scale
    k_all = jnp.concatenate([pg[0, :, 0:LANES].astype(BF16) for pg in pages], axis=0)
    v_all = jnp.concatenate([pg[0, :, LANES:2 * LANES].astype(BF16) for pg in pages], axis=0)
    nkeys = n_pages * page
    s_all = _dg(qrb, k_all, 1, 1) * scale
    expand = (_iota((LANES, nkeys), 0) == _iota((LANES, nkeys), 1) // SEL_BLOCK).astype(BF16)
    ok = _dg(selb, expand) > 0.5
    s_all = jnp.where(ok, s_all, NEG_INF)
    m = jnp.maximum(s_new, jnp.max(s_all, axis=-1, keepdims=True))
    e_new = jnp.exp(s_new - m)
    e = jnp.where(ok, jnp.exp(s_all - m), 0.0)
    den = e_new + jnp.sum(e, axis=-1, keepdims=True)
    osel = (e_new * nsel[:, LANES:2 * LANES] + _dg(e.astype(BF16), v_all)) / den
    wb = win_ref.shape[1]
    wnd = win_ref[0]
    nwin = nwin_ref[0]
    sw = _dg(qrb, wnd[:, 0:LANES].astype(BF16), 1, 1) * scale
    kwpos = past_len - wb + _iota((1, wb), 1)
    mw = (kwpos <= qpos) & (kwpos > qpos - WINDOW) & (kwpos >= 0)
    sw = jnp.where(mw, sw, NEG_INF)
    sw_new = jnp.sum(qr * nwin[:, 0:LANES], axis=-1, keepdims=True) * scale
    mx = jnp.maximum(jnp.max(sw, axis=-1, keepdims=True), sw_new)
    ew = jnp.where(mw, jnp.exp(sw - mx), 0.0)
    ew_new = jnp.exp(sw_new - mx)
    ow = (_dg(ew.astype(BF16), wnd[:, LANES:2 * LANES].astype(BF16)) + ew_new * nwin[:, LANES:2 * LANES]) / (
        jnp.sum(ew, axis=-1, keepdims=True) + ew_new)
    o_ref[0] = gt[:, 0:1] * oc + gt[:, 1:2] * osel + gt[:, 2:3] * ow
    shifted = pltpu.roll(wnd, wb - 1, 0)
    wout_ref[0] = jnp.where(_iota((wb, 1), 0) == wb - 1, nwin, shifted)


def _attn_sample(page_table, q, qr, gt, ckv, cache_sel, page_off, nsel, cache_win, win_off, nwin, past_len):
    bs, n_pages = page_table.shape
    page = cache_sel.shape[1]
    wb = cache_win.shape[1]
    nh = NSA_HEADS
    assert wb == WINDOW and past_len // SEL_BLOCK + 1 <= LANES
    oc, imp = _sample_cmp(q, ckv, past_len)
    selb = _sample_topk(imp.reshape(bs * nh, LANES), past_len).reshape(bs, nh, LANES)

    def page_spec(j):
        return pl.BlockSpec((1, page, 256), lambda b, pt: (pt[b, j] + page_off, 0, 0))

    vec = lambda w: pl.BlockSpec((1, nh, w), lambda b, pt: (b, 0, 0))
    rowspec = pl.BlockSpec((1, 1, 256), lambda b, pt: (b, 0, 0))
    return pl.pallas_call(
        functools.partial(_attn_sample_kernel, past_len=past_len, n_pages=n_pages),
        out_shape=[jax.ShapeDtypeStruct((bs, nh, LANES), F32), jax.ShapeDtypeStruct((bs, wb, 256), F32)],
        grid_spec=pltpu.PrefetchScalarGridSpec(
            num_scalar_prefetch=1,
            grid=(bs,),
            in_specs=[vec(LANES), vec(LANES), vec(LANES), vec(LANES)]
                     + [page_spec(j) for j in range(n_pages)]
                     + [rowspec,
                        pl.BlockSpec((1, wb, 256), lambda b, pt: (b + win_off, 0, 0)),
                        rowspec],
            out_specs=[vec(LANES), pl.BlockSpec((1, wb, 256), lambda b, pt: (b, 0, 0))]),
        compiler_params=_cparams(("arbitrary",)),
        name="attn_sample",
    )(page_table, qr, gt, oc, selb, *([cache_sel] * n_pages), nsel, cache_win, nwin)


def _hilo(w):
    hi = w.astype(BF16)
    return hi, (w - hi.astype(F32)).astype(BF16)


def _tile_heads(v, reps):
    return jnp.tile(v.reshape(1, -1), (1, reps))


def _layer_weights(l, norm_mix, norm_ffn, w_in, rwkv_mu, rwkv_w0, rwkv_w_up, rwkv_a0, rwkv_a_up, rwkv_g_up,
                   rwkv_k_k, rwkv_k_a, rwkv_r_k, rwkv_ln_w, rwkv_ln_b, qk_norm, cmp_pos, cmp_w1, cmp_b1, cmp_w2,
                   cmp_b2, w_out, ffn_w_up, ffn_w_down):
    lw = {}
    d = w_in.shape[1]
    lw['norm_mix'] = norm_mix[l].reshape(1, d)
    lw['norm_ffn'] = norm_ffn[l].reshape(1, d)
    lw['wr'] = w_in[l][:, :SHIFT_COLS].astype(BF16)
    w_n = jnp.pad(w_in[l][:, SHIFT_COLS:], ((0, 0), (0, NSA_COLS_PAD - NSA_COLS)))
    lw['wn'] = w_n.astype(BF16)
    lw['mu'] = rwkv_mu[l].reshape(1, -1)
    lw['w0'] = rwkv_w0[l].reshape(1, -1)
    lw['a0'] = rwkv_a0[l].reshape(1, -1)
    lw['k_k'] = rwkv_k_k[l].reshape(1, -1)
    lw['k_a'] = rwkv_k_a[l].reshape(1, -1)
    lw['r_k'] = rwkv_r_k[l].reshape(1, -1)
    lw['ln_w'] = rwkv_ln_w[l].reshape(1, -1)
    lw['ln_b'] = rwkv_ln_b[l].reshape(1, -1)
    zeros = jnp.zeros_like(rwkv_w_up[l])
    wc = jnp.concatenate([jnp.concatenate([rwkv_w_up[l], zeros], axis=1),
                          jnp.concatenate([zeros, rwkv_a_up[l]], axis=1)], axis=0)
    lw['wc_h'], lw['wc_l'] = _hilo(wc)
    lw['gu_h'], lw['gu_l'] = _hilo(rwkv_g_up[l])
    qn = qk_norm[l]
    lw['nw'] = jnp.tile(qn, (1, 2))
    lw['nw_kc'] = jnp.tile(qn[1:2], (1, 2))
    w1 = cmp_w1[l].reshape(2, 2, CMP_STRIDE, HEAD_DIM, CMP_HIDDEN)
    eye = jnp.eye(2, dtype=F32)
    w1g = jnp.einsum('khldj,gf->klgdhfj', w1, eye)
    w1g = w1g.reshape(2, CMP_STRIDE * 2 * HEAD_DIM, 2 * 2 * CMP_HIDDEN)
    lw['c1'] = w1g.astype(BF16)
    lw['cmp_pos'] = cmp_pos[l].reshape(2, 1, CMP_BLOCK * HEAD_DIM)
    lw['cmp_w1'] = cmp_w1[l]
    lw['cmp_b1'] = cmp_b1[l].reshape(2, 1, CMP_HIDDEN)
    w2 = cmp_w2[l]
    lw['cmp_w2bd'] = jnp.einsum('kjd,gf->kgjfd', w2, eye).reshape(2, 2 * CMP_HIDDEN, 2 * HEAD_DIM)
    lw['cmp_b2t'] = jnp.tile(cmp_b2[l].reshape(2, 1, HEAD_DIM), (1, 1, 2))
    lw['wo1'] = w_out[l][:RWKV_WIDTH].astype(BF16)
    lw['wo2'] = w_out[l][RWKV_WIDTH:].astype(BF16)
    lw['wu'] = ffn_w_up[l].astype(BF16)
    lw['wd'] = ffn_w_down[l].astype(BF16)
    return lw


def _rope_tables(pos):
    half = HEAD_DIM // 2
    inv_freq = ROPE_THETA ** (-jnp.arange(half, dtype=F32) / half)
    ang = pos.astype(F32)[:, None] * inv_freq[None, :]
    cos = jnp.cos(ang)
    sin = jnp.sin(ang)
    return jnp.tile(cos, (1, 4)), jnp.tile(jnp.concatenate([-sin, sin], axis=1), (1, 2))


def _mods6(m):
    d = m.shape[-1] // 6
    return [m[..., i * d:(i + 1) * d] for i in range(6)]


def _prompt_layer(x, mods, lw, tables):
    bsz, t, d = x.shape
    m = bsz * t
    tm = min(512, t)
    tpb = t // tm
    sh1, sc1, g1, sh2, sc2, g2 = [a.reshape(bsz, 1, d) for a in _mods6(mods)]
    x2 = x.reshape(m, d)
    p_r, p_n = _in_proj(x2, lw['norm_mix'], sc1, sh1, lw['wr'], lw['wn'], tm, tpb)
    shift0 = jnp.zeros((bsz, 1, SHIFT_COLS), F32)
    r, ld, k, v, a, b, g = _rwkv_prep(p_r, shift0, lw, tm, tpb, True)
    seq = lambda z: z.reshape(bsz, t, RWKV_WIDTH)
    q1, y0, gm, ha = _rwkv_chunk(seq(r), seq(ld), seq(k), seq(v), seq(a), seq(b))
    y, h_fin = _rwkv_scan(q1, y0, gm, ha)
    y_r = _rwkv_post(y.reshape(m, RWKV_WIDTH), r, k, v, g, lw, tm)
    s_new = h_fin.reshape(bsz, HEAD_DIM, RWKV_HEADS, HEAD_DIM).transpose(0, 2, 3, 1)
    shift_new = p_r.reshape(bsz, t, SHIFT_COLS)[:, -1]
    q, qr, cmp_rows, sel_rows, win_rows, gt, sel_k, sel_v, win_b = _nsa_proj(
        p_n, tables[0], tables[1], lw['nw'], tm, tpb)
    seqw = lambda z: z.reshape(bsz, t, z.shape[-1])
    ab = _cmp1_prompt(seqw(cmp_rows), lw['c1'])
    ckv = _cmp2(ab, lw['cmp_c1'], lw)
    y_n = _attn_prompt(seqw(q), seqw(qr), seqw(gt), ckv, seqw(sel_k), seqw(sel_v), seqw(win_b))
    x2 = _out_proj(x2, y_r, y_n.reshape(m, NSA_WIDTH), g1, lw['wo1'], lw['wo2'], tm, tpb)
    tmf = min(1024, t)
    x2 = _ffn(x2, lw['norm_ffn'], sc2, sh2, g2, lw['wu'], lw['wd'], tmf, t // tmf)
    rows6 = lambda z: z.reshape(bsz, t, 2, 2, HEAD_DIM)
    wlen = min(WINDOW, t)
    return (x2.reshape(bsz, t, d), s_new, shift_new, rows6(cmp_rows), rows6(sel_rows), rows6(win_rows)[:, -wlen:])


def _sample_layer(x, mods, lw, tables, l, state_rwkv, state_shift, page_table, cache_cmp, cache_sel, cache_win,
                  n_pool, past_len):
    bs, d = x.shape
    tm = bs
    sh1, sc1, g1, sh2, sc2, g2 = [a.reshape(1, bs, d) for a in _mods6(mods)]
    p_r, p_n = _in_proj(x, lw['norm_mix'], sc1, sh1, lw['wr'], lw['wn'], tm, 1)
    r, ld, k, v, a, b, g = _rwkv_prep(p_r, state_shift[l], lw, tm, 1, False)
    s_new, y = _rwkv_step(state_rwkv, l, r, ld, k, a, b, v)
    y_r = _rwkv_post(y.reshape(bs, RWKV_WIDTH), r, k, v, g, lw, tm)
    q, qr, cmp_rows, sel_rows, win_rows, gt, _, _, _ = _nsa_proj(p_n, tables[0], tables[1], lw['nw'], tm, 1)
    ab = _cmp1_sample(page_table, cache_cmp, l * n_pool, lw['c1'])
    ckv = _cmp2(ab, lw['cmp_c1'], lw)
    grp = (jnp.arange(NSA_HEADS) // (NSA_HEADS // 2))[None, :, None]
    half = (jnp.arange(LANES) // HEAD_DIM)[None, None, :]

    def pad_heads(z):
        z = z.reshape(bs, NSA_HEADS, HEAD_DIM)
        return jnp.where(grp == half, jnp.tile(z, (1, 1, 2)), 0.0)

    gates = jnp.pad(gt[:, :3 * NSA_HEADS].reshape(bs, NSA_HEADS, 3), ((0, 0), (0, 0), (0, LANES - 3)))
    o, win_new = _attn_sample(page_table, pad_heads(q), pad_heads(qr), gates, ckv, cache_sel, l * n_pool,
                              sel_rows.reshape(bs, 1, 256), cache_win, l * bs, win_rows.reshape(bs, 1, 256), past_len)
    o = o.reshape(bs, NSA_HEADS, 2, HEAD_DIM)
    y_n = jnp.concatenate([o[:, :NSA_HEADS // 2, 0], o[:, NSA_HEADS // 2:, 1]], axis=1).reshape(bs, NSA_WIDTH)
    x = _out_proj(x, y_r, y_n, g1, lw['wo1'], lw['wo2'], tm, 1)
    x = _ffn(x, lw['norm_ffn'], sc2, sh2, g2, lw['wu'], lw['wd'], tm, 1)
    rows6 = lambda z: z.reshape(bs, 1, 2, 2, HEAD_DIM)
    return (x, s_new, p_r, rows6(cmp_rows), rows6(sel_rows), win_new.reshape(bs, -1, 2, 2, HEAD_DIM))


def kernel(x_prompt, x_sample, cache_cmp_kv, cache_sel_kv, cache_win_kv, state_rwkv, state_shift, page_table,
           c_prompt, c_sample, ada_w, ada_b, norm_mix, norm_ffn, w_in, rwkv_mu, rwkv_w0, rwkv_w_up, rwkv_a0,
           rwkv_a_up, rwkv_g_up, rwkv_k_k, rwkv_k_a, rwkv_r_k, rwkv_ln_w, rwkv_ln_b, qk_norm, cmp_pos, cmp_w1,
           cmp_b1, cmp_w2, cmp_b2, w_out, ffn_w_up, ffn_w_down):
    depth = ada_w.shape[0]
    bsz, t, d = x_prompt.shape
    bs = x_sample.shape[0]
    n_pool, page = cache_cmp_kv.shape[1], cache_cmp_kv.shape[2]
    n_pages = page_table.shape[1]
    past_len = n_pages * page
    wb = cache_win_kv.shape[2]
    bc = -(-(bsz + bs) // 8) * 8
    c_all = jnp.pad(jnp.concatenate([c_prompt, c_sample], axis=0), ((0, bc - bsz - bs), (0, 0)))
    mods = _ada_mods(c_all, ada_w, ada_b)
    tab_p = _rope_tables(jnp.arange(t, dtype=jnp.int32))
    tab_s = _rope_tables(jnp.full((bs,), past_len, jnp.int32))
    cache_cmp = cache_cmp_kv.reshape(depth * n_pool, page, 256)
    cache_sel = cache_sel_kv.reshape(depth * n_pool, page, 256)
    cache_win = cache_win_kv.reshape(depth * bs, wb, 256)
    st_rwkv = state_rwkv.reshape((depth * bs,) + state_rwkv.shape[2:])
    y_p = x_prompt
    y_s = x_sample.reshape(bs, d)
    outs_p = []
    outs_s = []
    for l in range(depth):
        lw = _layer_weights(l, norm_mix, norm_ffn, w_in, rwkv_mu, rwkv_w0, rwkv_w_up, rwkv_a0, rwkv_a_up,
                            rwkv_g_up, rwkv_k_k, rwkv_k_a, rwkv_r_k, rwkv_ln_w, rwkv_ln_b, qk_norm, cmp_pos,
                            cmp_w1, cmp_b1, cmp_w2, cmp_b2, w_out, ffn_w_up, ffn_w_down)
        lw['cmp_c1'] = _cmp_bias(lw)
        res_p = _prompt_layer(y_p, mods[l, :bsz], lw, tab_p)
        y_p = res_p[0]
        outs_p.append(res_p[1:])
        res_s = _sample_layer(y_s, mods[l, bsz:bsz + bs], lw, tab_s, l, st_rwkv, state_shift, page_table,
                              cache_cmp, cache_sel, cache_win, n_pool, past_len)
        y_s = res_s[0]
        outs_s.append(res_s[1:])
    stack = lambda outs, i: jnp.stack([o[i] for o in outs])
    return (y_p, y_s.reshape(bs, 1, d),
            stack(outs_p, 2), stack(outs_s, 2),
            stack(outs_p, 3), stack(outs_s, 3),
            stack(outs_p, 4), stack(outs_s, 4),
            stack(outs_p, 0), stack(outs_s, 0),
            stack(outs_p, 1), stack(outs_s, 1))
```

```python
import functools

import numpy as np
import jax
import jax.numpy as jnp
from jax import lax
from jax.experimental import pallas as pl
from jax.experimental.pallas import tpu as pltpu

F32 = jnp.float32
BF16 = jnp.bfloat16

HEAD_DIM = 64
RWKV_WIDTH = 512
NSA_WIDTH = 512
RWKV_HEADS = 8
NSA_HEADS = 8
KV_WIDTH = 128
SHIFT_COLS = 1792
NSA_COLS = 1304
NSA_COLS_PAD = 1408
CMP_BLOCK = 32
CMP_STRIDE = 16
CMP_HIDDEN = 128
SEL_BLOCK = 64
SEL_TOP_N = 16
WINDOW = 512
ROPE_THETA = 10000.0
NORM_EPS = 1e-6
GN_EPS = 64e-5
NEG_INF = -1e30
FORCE = 1e9
LANES = 128
RWKV_CHUNK = 64
Q_TILE = 128
VMEM_LIMIT = 56 * 1024 * 1024


def _cparams(sem):
    return pltpu.CompilerParams(dimension_semantics=sem, vmem_limit_bytes=VMEM_LIMIT)


def _split2(x):
    hi = x.astype(BF16)
    lo = (x - hi.astype(F32)).astype(BF16)
    return hi, lo


def _split3(x):
    hi = x.astype(BF16)
    r = x - hi.astype(F32)
    mid = r.astype(BF16)
    lo = (r - mid.astype(F32)).astype(BF16)
    return hi, mid, lo


def _dg(a, b, ca=1, cb=0):
    return lax.dot_general(a, b, (((ca,), (cb,)), ((), ())), preferred_element_type=F32)


def _mm(a, b, ca=1, cb=0):
    ah, al = _split2(a)
    bh, bl = _split2(b)
    return _dg(ah, bh, ca, cb) + _dg(ah, bl, ca, cb) + _dg(al, bh, ca, cb)


def _bf(x):
    return x.astype(BF16)


def _mm1(a, b, ca=1, cb=0):
    return _dg(_bf(a), _bf(b), ca, cb)


def _mm_w(ah, al, wh, wl):
    return _dg(ah, wh) + _dg(ah, wl) + _dg(al, wh)


def _mmp(a2, b2, ca=1, cb=0):
    return _dg(a2[0], b2[0], ca, cb) + _dg(a2[0], b2[1], ca, cb) + _dg(a2[1], b2[0], ca, cb)


def _mm_xr(a, e, ca=1, cb=0):
    a0, a1, a2 = _split3(a)
    return _dg(a0, e, ca, cb) + _dg(a1, e, ca, cb) + _dg(a2, e, ca, cb)


def _mm_xl(e, b, ca=1, cb=0):
    b0, b1, b2 = _split3(b)
    return _dg(e, b0, ca, cb) + _dg(e, b1, ca, cb) + _dg(e, b2, ca, cb)


def _iota(shape, dim):
    return lax.broadcasted_iota(jnp.int32, shape, dim)


def _head_blockdiag():
    return (_iota((LANES, LANES), 0) // HEAD_DIM == _iota((LANES, LANES), 1) // HEAD_DIM).astype(BF16)


def _segsum(x, bd):
    w = x.shape[1]
    outs = [_mm_xr(x[:, j * LANES:(j + 1) * LANES], bd) for j in range(w // LANES)]
    return outs[0] if len(outs) == 1 else jnp.concatenate(outs, axis=1)


def _sigmoid(x):
    return 1.0 / (1.0 + jnp.exp(-x))


def _norm_mod(x, g, sc, sh):
    ms = jnp.mean(x * x, axis=-1, keepdims=True)
    y = x * lax.rsqrt(ms + NORM_EPS) * g
    return y * (1.0 + sc) + sh


def _topk_mask(score, n, axis):
    size = score.shape[axis]
    pos = _iota(score.shape, axis).astype(F32)
    sel = jnp.zeros_like(score)
    sc = score
    for _ in range(n):
        m = jnp.max(sc, axis=axis, keepdims=True)
        idx = jnp.min(jnp.where(sc == m, pos, float(size)), axis=axis, keepdims=True)
        hit = pos == idx
        sel = jnp.where(hit & (m > 0.5 * NEG_INF), 1.0, sel)
        sc = jnp.where(hit, -3e38, sc)
    return sel


def _ada_kernel(c_ref, w_ref, b_ref, o_ref):
    c = c_ref[...]
    o_ref[0] = _mm(c * _sigmoid(c), w_ref[0]) + b_ref[0]


def _ada_mods(c_all, ada_w, ada_b):
    depth, d, n = ada_w.shape
    bc = c_all.shape[0]
    tn = 1024
    return pl.pallas_call(
        _ada_kernel,
        out_shape=jax.ShapeDtypeStruct((depth, bc, n), F32),
        grid=(depth, n // tn),
        in_specs=[pl.BlockSpec((bc, d), lambda l, j: (0, 0)),
                  pl.BlockSpec((1, d, tn), lambda l, j: (l, 0, j)),
                  pl.BlockSpec((1, 1, tn), lambda l, j: (l, 0, j))],
        out_specs=pl.BlockSpec((1, bc, tn), lambda l, j: (l, 0, j)),
        compiler_params=_cparams(("arbitrary", "arbitrary")),
        name="ada_mods",
    )(c_all, ada_w, ada_b.reshape(depth, 1, n))


def _in_proj_kernel(x_ref, g_ref, sc_ref, sh_ref, wr_ref, wn_ref, or_ref, on_ref):
    h = _norm_mod(x_ref[...], g_ref[...], sc_ref[0], sh_ref[0]).astype(BF16)
    or_ref[...] = _dg(h, wr_ref[...])
    on_ref[...] = _dg(h, wn_ref[...])


def _in_proj(x, g, sc, sh, wr, wn, tm, tpb):
    m, d = x.shape
    nr = wr.shape[1]
    nn = wn.shape[1]
    r = sc.shape[1]
    return pl.pallas_call(
        _in_proj_kernel,
        out_shape=[jax.ShapeDtypeStruct((m, nr), F32), jax.ShapeDtypeStruct((m, nn), F32)],
        grid=(m // tm,),
        in_specs=[pl.BlockSpec((tm, d), lambda i: (i, 0)),
                  pl.BlockSpec((1, d), lambda i: (0, 0)),
                  pl.BlockSpec((1, r, d), lambda i: (i // tpb, 0, 0)),
                  pl.BlockSpec((1, r, d), lambda i: (i // tpb, 0, 0)),
                  pl.BlockSpec((d, nr), lambda i: (0, 0)),
                  pl.BlockSpec((d, nn), lambda i: (0, 0))],
        out_specs=[pl.BlockSpec((tm, nr), lambda i: (i, 0)), pl.BlockSpec((tm, nn), lambda i: (i, 0))],
        compiler_params=_cparams(("arbitrary",)),
        name="in_proj",
    )(x, g, sc, sh, wr, wn)


def _out_proj_kernel(x_ref, a1_ref, a2_ref, gate_ref, w1, w2, o_ref):
    acc = _dg(a1_ref[...].astype(BF16), w1[...]) + _dg(a2_ref[...].astype(BF16), w2[...])
    o_ref[...] = x_ref[...] + gate_ref[0] * acc


def _out_proj(x, a1, a2, gate, w1, w2, tm, tpb):
    m, d = x.shape
    k1 = a1.shape[1]
    k2 = a2.shape[1]
    r = gate.shape[1]
    return pl.pallas_call(
        _out_proj_kernel,
        out_shape=jax.ShapeDtypeStruct((m, d), F32),
        grid=(m // tm,),
        in_specs=[pl.BlockSpec((tm, d), lambda i: (i, 0)),
                  pl.BlockSpec((tm, k1), lambda i: (i, 0)),
                  pl.BlockSpec((tm, k2), lambda i: (i, 0)),
                  pl.BlockSpec((1, r, d), lambda i: (i // tpb, 0, 0)),
                  pl.BlockSpec((k1, d), lambda i: (0, 0)),
                  pl.BlockSpec((k2, d), lambda i: (0, 0))],
        out_specs=pl.BlockSpec((tm, d), lambda i: (i, 0)),
        compiler_params=_cparams(("arbitrary",)),
        name="out_proj",
    )(x, a1, a2, gate, w1, w2)


def _ffn_kernel(x_ref, g_ref, sc_ref, sh_ref, gate_ref, wg, wu, wd, o_ref, h_s, acc_s):
    j = pl.program_id(1)

    @pl.when(j == 0)
    def _():
        h_s[...] = _norm_mod(x_ref[...], g_ref[...], sc_ref[0], sh_ref[0]).astype(BF16)
        acc_s[...] = jnp.zeros_like(acc_s)

    h = h_s[...]
    gate = _dg(h, wg[...])
    up = _dg(h, wu[...])
    act = gate * _sigmoid(gate) * up
    acc_s[...] += _dg(act.astype(BF16), wd[...])

    @pl.when(j == pl.num_programs(1) - 1)
    def _():
        o_ref[...] = x_ref[...] + gate_ref[0] * acc_s[...]


def _ffn(x, g, sc, sh, gate, w_up, w_down, tm, tpb):
    m, d = x.shape
    dff = w_down.shape[0]
    tf = 256
    nf = dff // tf
    r = sc.shape[1]
    mod = pl.BlockSpec((1, r, d), lambda i, j: (i // tpb, 0, 0))
    return pl.pallas_call(
        _ffn_kernel,
        out_shape=jax.ShapeDtypeStruct((m, d), F32),
        grid=(m // tm, nf),
        in_specs=[pl.BlockSpec((tm, d), lambda i, j: (i, 0)),
                  pl.BlockSpec((1, d), lambda i, j: (0, 0)),
                  mod, mod, mod,
                  pl.BlockSpec((d, tf), lambda i, j: (0, j)),
                  pl.BlockSpec((d, tf), lambda i, j: (0, j + nf)),
                  pl.BlockSpec((tf, d), lambda i, j: (j, 0))],
        out_specs=pl.BlockSpec((tm, d), lambda i, j: (i, 0)),
        scratch_shapes=[pltpu.VMEM((tm, d), BF16), pltpu.VMEM((tm, d), F32)],
        compiler_params=_cparams(("arbitrary", "arbitrary")),
        name="ffn",
    )(x, g, sc, sh, gate, w_up, w_up, w_down)


def _rwkv_prep_kernel(p_ref, prev_ref, mu_ref, wch, wcl, guh, gul, w0_ref, a0_ref, kk_ref, ka_ref,
                      r_o, ld_o, k_o, v_o, a_o, b_o, g_o, *scratch, seq_mode, tpb):
    p = p_ref[...]
    tm = p.shape[0]
    if seq_mode:
        carry = scratch[0]
        i = pl.program_id(0)

        @pl.when(i % tpb == 0)
        def _():
            carry[...] = prev_ref[0]

        rolled = pltpu.roll(p, 1, 0)
        prev = jnp.where(_iota((tm, 1), 0) == 0, carry[...], rolled)
        carry[...] = p[tm - 1:tm, :]
    else:
        prev = prev_ref[...]
    xs = p + mu_ref[...] * (prev - p)
    w_ = RWKV_WIDTH
    r = xs[:, 0:w_]
    k = xs[:, w_:2 * w_]
    v = xs[:, 2 * w_:3 * w_]
    z = xs[:, 3 * w_:3 * w_ + LANES]
    gd = xs[:, 3 * w_ + LANES:3 * w_ + 2 * LANES]
    z = jnp.where(_iota((tm, LANES), 1) < 64, jnp.tanh(z), z)
    zh, zl = _split2(z)
    wa = _mm_w(zh, zl, wch[...], wcl[...])
    zw = -(w0_ref[...] + wa[:, 0:w_])
    softplus = jnp.maximum(zw, 0.0) + jnp.log1p(jnp.exp(-jnp.abs(zw)))
    wlog = -softplus - 0.5
    a_sig = _sigmoid(a0_ref[...] + wa[:, w_:2 * w_])
    sh_, sl_ = _split2(_sigmoid(gd))
    g = _mm_w(sh_, sl_, guh[...], gul[...])
    bd = _head_blockdiag()
    kkv = k * kk_ref[...]
    nrm = jnp.sqrt(_segsum(kkv * kkv, bd))
    kkn = kkv / jnp.maximum(nrm, 1e-12)
    r_o[...] = r
    ld_o[...] = -jnp.exp(wlog)
    k_o[...] = k * (1.0 + (a_sig - 1.0) * ka_ref[...])
    v_o[...] = v
    a_o[...] = -kkn
    b_o[...] = kkn * a_sig
    g_o[...] = g


def _rwkv_prep(p_r, prev, lw, tm, tpb, seq_mode):
    m = p_r.shape[0]
    w_ = RWKV_WIDTH
    row = lambda n: pl.BlockSpec((1, n), lambda i: (0, 0))
    full = lambda a: pl.BlockSpec(a.shape, lambda i: (0, 0))
    if seq_mode:
        prev_spec = pl.BlockSpec((1, 1, SHIFT_COLS), lambda i: (i // tpb, 0, 0))
        scratch = [pltpu.VMEM((1, SHIFT_COLS), F32)]
    else:
        prev_spec = pl.BlockSpec((tm, SHIFT_COLS), lambda i: (i, 0))
        scratch = []
    out = jax.ShapeDtypeStruct((m, w_), F32)
    ospec = pl.BlockSpec((tm, w_), lambda i: (i, 0))
    return pl.pallas_call(
        functools.partial(_rwkv_prep_kernel, seq_mode=seq_mode, tpb=tpb),
        out_shape=[out] * 7,
        grid=(m // tm,),
        in_specs=[pl.BlockSpec((tm, SHIFT_COLS), lambda i: (i, 0)), prev_spec, row(SHIFT_COLS),
                  full(lw['wc_h']), full(lw['wc_l']), full(lw['gu_h']), full(lw['gu_l']),
                  row(w_), row(w_), row(w_), row(w_)],
        out_specs=[ospec] * 7,
        scratch_shapes=scratch,
        compiler_params=_cparams(("arbitrary",)),
        name="rwkv_prep",
    )(p_r, prev, lw['mu'], lw['wc_h'], lw['wc_l'], lw['gu_h'], lw['gu_l'], lw['w0'], lw['a0'], lw['k_k'], lw['k_a'])


def _rwkv_chunk_kernel(r_ref, ld_ref, k_ref, v_ref, a_ref, b_ref, q1_o, y0_o, g_o, ha_o):
    c = RWKV_CHUNK
    c2 = 2 * c
    nb = r_ref.shape[0]
    tri = (_iota((c, c), 0) >= _iota((c, c), 1)).astype(BF16)
    at, rt, bt, kt, bh, kh, v, p_end = ([] for _ in range(8))
    for bi in range(nb):
        ld = ld_ref[bi]
        lc = _mm_xl(tri, ld)
        lend = lc[c - 1:c, :]
        inv_p = jnp.exp(-lc)
        p_to_end = jnp.exp(lend - lc)
        p_end.append(jnp.exp(lend))
        v.append(v_ref[bi])
        rt.append(r_ref[bi] * jnp.exp(lc))
        at.append(a_ref[bi] * jnp.exp(lc - ld))
        bt.append(b_ref[bi] * inv_p)
        kt.append(k_ref[bi] * inv_p)
        bh.append(b_ref[bi] * p_to_end)
        kh.append(k_ref[bi] * p_to_end)
    row = _iota((c2, c2), 0)
    col = _iota((c2, c2), 1)
    strict = row % c > col % c
    incl = row % c >= col % c
    eye = row == col
    lo = _iota((c, LANES), 1) < HEAD_DIM

    def pair_rows(x):
        return jnp.concatenate([jnp.where(lo, x, 0.0), jnp.where(lo, 0.0, x)], axis=0)

    chains = [(bi, slice(LANES * j, LANES * (j + 1))) for bi in range(nb) for j in range(RWKV_WIDTH // LANES)]
    pairs = range(len(chains))
    at2 = [pair_rows(at[bi][:, sl]) for bi, sl in chains]
    rt2 = [pair_rows(rt[bi][:, sl]) for bi, sl in chains]
    v2s = [_bf(pair_rows(v[bi][:, sl])) for bi, sl in chains]
    lhs = [_bf(jnp.concatenate([at2[j], rt2[j]], axis=0)) for j in pairs]
    xb = [_dg(lhs[j], _bf(pair_rows(bt[bi][:, sl])), 1, 1) for j, (bi, sl) in enumerate(chains)]
    xk = [_dg(lhs[j], _bf(pair_rows(kt[bi][:, sl])), 1, 1) for j, (bi, sl) in enumerate(chains)]
    a_ab = [jnp.where(strict, xb[j][0:c2], 0.0) for j in pairs]
    a_rb = [_bf(jnp.where(incl, xb[j][c2:2 * c2], 0.0)) for j in pairs]
    a_ak = [_bf(jnp.where(strict, xk[j][0:c2], 0.0)) for j in pairs]
    a_rk = [_bf(jnp.where(incl, xk[j][c2:2 * c2], 0.0)) for j in pairs]
    akv = [_dg(a_ak[j], v2s[j]) for j in pairs]
    x = [jnp.where(eye, 1.0, 0.0) + a_ab[j] for j in pairs]
    mp = a_ab
    for _ in range(int(np.log2(c)) - 1):
        mps = [_bf(mp[j]) for j in pairs]
        mp = [_dg(mps[j], mps[j]) for j in pairs]
        x = [x[j] + _dg(_bf(mp[j]), _bf(x[j])) for j in pairs]
    wu = [_bf(_dg(_bf(x[j]), _bf(jnp.concatenate([at2[j], akv[j]], axis=1)))) for j in pairs]
    qy = [_dg(a_rb[j], wu[j]) for j in pairs]
    ark_v = [_dg(a_rk[j], v2s[j]) for j in pairs]
    gh = [_dg(_bf(pair_rows(bh[bi][:, sl])), wu[j], 0, 0) for j, (bi, sl) in enumerate(chains)]
    khv = [_dg(_bf(pair_rows(kh[bi][:, sl])), v2s[j], 0, 0) for j, (bi, sl) in enumerate(chains)]
    for j, (bi, sl) in enumerate(chains):
        q1 = rt2[j] + qy[j][:, 0:LANES]
        y0 = qy[j][:, LANES:2 * LANES] + ark_v[j]
        q1_o[bi, :, sl] = q1[0:c] + q1[c:c2]
        y0_o[bi, :, sl] = y0[0:c] + y0[c:c2]
        g_o[bi, 0, :, sl] = jnp.where(eye, p_end[bi][:, sl], 0.0) + gh[j][:, 0:LANES]
        ha_o[bi, 0, :, sl] = gh[j][:, LANES:2 * LANES] + khv[j]


def _rwkv_chunk(r, ld, k, v, a, b):
    bsz, t, w_ = r.shape
    c = RWKV_CHUNK
    nch = t // c
    nb = 2 if bsz % 2 == 0 else 1
    ispec = pl.BlockSpec((nb, c, w_), lambda i, j: (i, j, 0))
    sspec = pl.BlockSpec((nb, 1, 2 * HEAD_DIM, w_), lambda i, j: (i, j, 0, 0))
    return pl.pallas_call(
        _rwkv_chunk_kernel,
        out_shape=[jax.ShapeDtypeStruct((bsz, t, w_), F32)] * 2
                  + [jax.ShapeDtypeStruct((bsz, nch, 2 * HEAD_DIM, w_), F32)] * 2,
        grid=(bsz // nb, nch),
        in_specs=[ispec] * 6,
        out_specs=[ispec, ispec, sspec, sspec],
        compiler_params=_cparams(("arbitrary", "arbitrary")),
        name="rwkv_chunk",
    )(r, ld, k, v, a, b)


def _rwkv_scan_kernel(q1_ref, y0_ref, g_ref, ha_ref, y_o, hf_o, h_s):
    j = pl.program_id(0)

    @pl.when(j == 0)
    def _():
        h_s[...] = jnp.zeros_like(h_s)

    c = RWKV_CHUNK
    cps = g_ref.shape[1]
    src = j % 2
    dst = (j + 1) % 2
    chains = [(bi, slice(LANES * p, LANES * (p + 1)))
              for bi in range(q1_ref.shape[0]) for p in range(RWKV_WIDTH // LANES)]
    h = [h_s[src, bi, :, sl] for bi, sl in chains]
    for cc in range(cps):
        rows = slice(c * cc, c * (cc + 1))
        for n, (bi, sl) in enumerate(chains):
            h0 = _split2(h[n])
            y_o[bi, rows, sl] = _mmp(_split2(q1_ref[bi, rows, sl]), h0) + y0_ref[bi, rows, sl]
            h[n] = _mmp(_split2(g_ref[bi, cc, :, sl]), h0) + ha_ref[bi, cc, :, sl]
    for n, (bi, sl) in enumerate(chains):
        h_s[dst, bi, :, sl] = h[n]

    @pl.when(j == pl.num_programs(0) - 1)
    def _():
        for n, (bi, sl) in enumerate(chains):
            hf_o[bi, :, sl] = h[n][0:HEAD_DIM] + h[n][HEAD_DIM:2 * HEAD_DIM]


def _rwkv_scan(q1, y0, g, ha):
    bsz, t, w_ = q1.shape
    nch = t // RWKV_CHUNK
    cps = 8 if nch % 8 == 0 else (4 if nch % 4 == 0 else 1)
    c = cps * RWKV_CHUNK
    ispec = pl.BlockSpec((bsz, c, w_), lambda j: (0, j, 0))
    sspec = pl.BlockSpec((bsz, cps, 2 * HEAD_DIM, w_), lambda j: (0, j, 0, 0))
    return pl.pallas_call(
        _rwkv_scan_kernel,
        out_shape=[jax.ShapeDtypeStruct((bsz, t, w_), F32), jax.ShapeDtypeStruct((bsz, HEAD_DIM, w_), F32)],
        grid=(t // c,),
        in_specs=[ispec, ispec, sspec, sspec],
        out_specs=[ispec, pl.BlockSpec((bsz, HEAD_DIM, w_), lambda j: (0, 0, 0))],
        scratch_shapes=[pltpu.VMEM((2, bsz, 2 * HEAD_DIM, w_), F32)],
        compiler_params=_cparams(("arbitrary",)),
        name="rwkv_scan",
    )(q1, y0, g, ha)


def _rwkv_step_kernel(s_ref, r_ref, ld_ref, k_ref, a_ref, b_ref, v_ref, so_ref, y_ref):
    s = s_ref[...]
    sa = jnp.sum(s * a_ref[...], axis=-1, keepdims=True)
    s_new = s * jnp.exp(ld_ref[...]) + sa * b_ref[...] + v_ref[...] * k_ref[...]
    so_ref[...] = s_new
    y_ref[...] = jnp.sum(s_new * r_ref[...], axis=-1, keepdims=True)


def _rwkv_step(state, layer, r, ld, k, a, b, v):
    bs = r.shape[0]
    nh, hd = RWKV_HEADS, HEAD_DIM
    bt = min(8, bs)
    rowv = lambda x: x.reshape(bs, nh, 1, hd)
    vspec = pl.BlockSpec((bt, nh, 1, hd), lambda i: (i, 0, 0, 0))
    cspec = pl.BlockSpec((bt, nh, hd, 1), lambda i: (i, 0, 0, 0))
    off = layer * (bs // bt)
    return pl.pallas_call(
        _rwkv_step_kernel,
        out_shape=[jax.ShapeDtypeStruct((bs, nh, hd, hd), F32), jax.ShapeDtypeStruct((bs, nh, hd, 1), F32)],
        grid=(bs // bt,),
        in_specs=[pl.BlockSpec((bt, nh, hd, hd), lambda i: (i + off, 0, 0, 0)),
                  vspec, vspec, vspec, vspec, vspec, cspec],
        out_specs=[pl.BlockSpec((bt, nh, hd, hd), lambda i: (i, 0, 0, 0)), cspec],
        compiler_params=_cparams(("arbitrary",)),
        name="rwkv_step",
    )(state, rowv(r), rowv(ld), rowv(k), rowv(a), rowv(b), v.reshape(bs, nh, hd, 1))


def _rwkv_post_kernel(y_ref, r_ref, k_ref, v_ref, g_ref, rk_ref, lw_ref, lb_ref, o_ref):
    bd = _head_blockdiag()
    y = y_ref[...]
    inv = 1.0 / HEAD_DIM
    mu = _segsum(y, bd) * inv
    d = y - mu
    var = _segsum(d * d, bd) * inv
    yn = d * lax.rsqrt(var + GN_EPS) * lw_ref[...] + lb_ref[...]
    v = v_ref[...]
    bonus = _segsum(r_ref[...] * k_ref[...] * rk_ref[...], bd) * v
    o_ref[...] = (yn + bonus) * g_ref[...]


def _rwkv_post(y, r, k, v, g, lw, tm):
    m, w_ = y.shape
    tile = pl.BlockSpec((tm, w_), lambda i: (i, 0))
    row = pl.BlockSpec((1, w_), lambda i: (0, 0))
    return pl.pallas_call(
        _rwkv_post_kernel,
        out_shape=jax.ShapeDtypeStruct((m, w_), F32),
        grid=(m // tm,),
        in_specs=[tile] * 5 + [row] * 3,
        out_specs=tile,
        compiler_params=_cparams(("arbitrary",)),
        name="rwkv_post",
    )(y, r, k, v, g, lw['r_k'], lw['ln_w'], lw['ln_b'])


def _nsa_proj_kernel(p_ref, cos_ref, sin_ref, nw_ref, q_o, qr_o, cmp_o, sel_o, win_o, gt_o, selk_o, selv_o, winb_o):
    tm = p_ref.shape[0]
    cos = cos_ref[...]
    sin = sin_ref[...]
    bd = _head_blockdiag()
    first = (_iota((tm, LANES), 1) % HEAD_DIM) < HEAD_DIM // 2
    nw = nw_ref[...]

    def norm(x, w):
        ms = _segsum(x * x, bd) * (1.0 / HEAD_DIM)
        return x * lax.rsqrt(ms + NORM_EPS) * w

    def rope(x):
        rot = jnp.where(first, pltpu.roll(x, LANES - HEAD_DIM // 2, 1), pltpu.roll(x, HEAD_DIM // 2, 1))
        return x * cos + rot * sin

    for j in range(NSA_WIDTH // LANES):
        sl = slice(LANES * j, LANES * (j + 1))
        xn = norm(p_ref[:, sl], nw[0:1])
        q_o[:, sl] = xn
        qr_o[:, sl] = rope(xn)
    o = NSA_WIDTH
    cmp_o[...] = p_ref[:, o:o + 2 * KV_WIDTH]
    ks = rope(norm(p_ref[:, o + 256:o + 384], nw[2:3]))
    vs = p_ref[:, o + 384:o + 512]
    kw = rope(norm(p_ref[:, o + 512:o + 640], nw[3:4]))
    vw = p_ref[:, o + 640:o + 768]
    sel_o[:, 0:LANES] = ks
    sel_o[:, LANES:2 * LANES] = vs
    win_o[:, 0:LANES] = kw
    win_o[:, LANES:2 * LANES] = vw
    gt_o[...] = _sigmoid(p_ref[:, o + 768:o + 896])
    lo = _iota((tm, LANES), 1) < HEAD_DIM
    selk_o[...] = ks.astype(BF16)
    selv_o[:, 0:LANES] = jnp.where(lo, vs, 1.0).astype(BF16)
    selv_o[:, LANES:2 * LANES] = jnp.where(lo, 1.0, vs).astype(BF16)
    winb_o[:, 0:LANES] = kw.astype(BF16)
    winb_o[:, LANES:2 * LANES] = vw.astype(BF16)


def _nsa_proj(p_n, cos_t, sin_t, nw, tm, tpb):
    m = p_n.shape[0]
    tt = cos_t.shape[0] // tm
    tab = pl.BlockSpec((tm, LANES), lambda i: (i % tt, 0))
    shapes = [(m, NSA_WIDTH), (m, NSA_WIDTH), (m, 256), (m, 256), (m, 256), (m, LANES),
              (m, LANES), (m, 256), (m, 256)]
    dtypes = [F32] * 6 + [BF16] * 3
    return pl.pallas_call(
        _nsa_proj_kernel,
        out_shape=[jax.ShapeDtypeStruct(s, dt) for s, dt in zip(shapes, dtypes)],
        grid=(m // tm,),
        in_specs=[pl.BlockSpec((tm, NSA_COLS_PAD), lambda i: (i, 0)), tab, tab,
                  pl.BlockSpec((4, LANES), lambda i: (0, 0))],
        out_specs=[pl.BlockSpec((tm, s[1]), lambda i: (i, 0)) for s in shapes],
        compiler_params=_cparams(("arbitrary",)),
        name="nsa_proj",
    )(p_n, cos_t, sin_t, nw)


def _cmp1_kernel(*refs, n_parts, paged):
    if paged:
        refs = refs[1:]
    parts = refs[:n_parts]
    w_ref, o_ref, xs = refs[n_parts:]
    gpp = parts[0].shape[1] // CMP_STRIDE
    for j, pr in enumerate(parts):
        for l in range(CMP_STRIDE):
            xs[gpp * j:gpp * (j + 1), LANES * l:LANES * (l + 1)] = pr[0, pl.ds(l, gpp, stride=CMP_STRIDE), :]
    res = _dg(_bf(xs[...]), w_ref[0])
    ng = o_ref.shape[2]
    for s in range(o_ref.shape[0]):
        o_ref[s, 0] = res[ng * s:ng * (s + 1)]


def _cmp1_prompt(cmp_rows, w):
    bsz, t, _ = cmp_rows.shape
    rows = min(2048, t)
    ng = rows // CMP_STRIDE
    kdim = CMP_STRIDE * LANES
    return pl.pallas_call(
        functools.partial(_cmp1_kernel, n_parts=1, paged=False),
        out_shape=jax.ShapeDtypeStruct((bsz, 2, t // CMP_STRIDE, 512), F32),
        grid=(bsz, t // rows, 2),
        in_specs=[pl.BlockSpec((1, rows, LANES), lambda b, i, kv: (b, i, kv)),
                  pl.BlockSpec((1, kdim, 512), lambda b, i, kv: (kv, 0, 0))],
        out_specs=pl.BlockSpec((1, 1, ng, 512), lambda b, i, kv: (b, kv, i, 0)),
        scratch_shapes=[pltpu.VMEM((ng, kdim), F32)],
        compiler_params=_cparams(("arbitrary", "arbitrary", "arbitrary")),
        name="cmp1_prompt",
    )(cmp_rows, w)


def _cmp1_sample(page_table, cache, page_off, w):
    bs, n_pages = page_table.shape
    page = cache.shape[1]
    ng = n_pages * page // CMP_STRIDE
    kdim = CMP_STRIDE * LANES
    sps = 4 if bs % 4 == 0 else 1

    def page_spec(s, j):
        return pl.BlockSpec((1, page, LANES), lambda b, kv, pt: (pt[b * sps + s, j] + page_off, 0, kv))

    return pl.pallas_call(
        functools.partial(_cmp1_kernel, n_parts=sps * n_pages, paged=True),
        out_shape=jax.ShapeDtypeStruct((bs, 2, ng, 512), F32),
        grid_spec=pltpu.PrefetchScalarGridSpec(
            num_scalar_prefetch=1,
            grid=(bs // sps, 2),
            in_specs=[page_spec(s, j) for s in range(sps) for j in range(n_pages)] + [
                pl.BlockSpec((1, kdim, 512), lambda b, kv, pt: (kv, 0, 0))],
            out_specs=pl.BlockSpec((sps, 1, ng, 512), lambda b, kv, pt: (b, kv, 0, 0)),
            scratch_shapes=[pltpu.VMEM((sps * ng, kdim), F32)]),
        compiler_params=_cparams(("arbitrary", "arbitrary")),
        name="cmp1_sample",
    )(page_table, *([cache] * (sps * n_pages)), w)


def _cmp_bias_kernel(pos_ref, w1_ref, b1_ref, o_ref):
    posb = jnp.broadcast_to(pos_ref[0], (8, pos_ref.shape[2]))
    o_ref[0] = _mm(posb, w1_ref[0])[0:1] + b1_ref[0]


def _cmp_bias(lw):
    kdim = CMP_BLOCK * HEAD_DIM
    return pl.pallas_call(
        _cmp_bias_kernel,
        out_shape=jax.ShapeDtypeStruct((2, 1, CMP_HIDDEN), F32),
        grid=(2,),
        in_specs=[pl.BlockSpec((1, 1, kdim), lambda kv: (kv, 0, 0)),
                  pl.BlockSpec((1, kdim, CMP_HIDDEN), lambda kv: (kv, 0, 0)),
                  pl.BlockSpec((1, 1, CMP_HIDDEN), lambda kv: (kv, 0, 0))],
        out_specs=pl.BlockSpec((1, 1, CMP_HIDDEN), lambda kv: (kv, 0, 0)),
        compiler_params=_cparams(("arbitrary",)),
        name="cmp_bias",
    )(lw['cmp_pos'], lw['cmp_w1'], lw['cmp_b1'])


def _cmp2_kernel(ab_ref, c1_ref, w2_ref, b2_ref, nw_ref, o_ref):
    kv = pl.program_id(1)
    spb, _, ng, width = ab_ref.shape
    ab = ab_ref[:, 0].reshape(spb * ng, width)
    a_part = ab[:, 0:256]
    b_next = pltpu.roll(ab[:, 256:512], spb * ng - 1, 0)
    c1 = c1_ref[0]
    h = a_part + b_next + jnp.concatenate([c1, c1], axis=1)
    h = 0.5 * h * (1.0 + jnp.tanh(np.sqrt(2.0 / np.pi) * (h + 0.044715 * (h * h * h))))
    out = _mm1(h, w2_ref[0]) + b2_ref[0]
    ms = _segsum(out * out, _head_blockdiag()) * (1.0 / HEAD_DIM)
    normed = out * lax.rsqrt(ms + NORM_EPS) * nw_ref[...]
    o_ref[:, 0] = jnp.where(kv == 0, normed, out).reshape(spb, ng, LANES)


def _cmp2(ab, c1, lw):
    bsz, _, ng, _ = ab.shape
    spb = 8 if bsz % 8 == 0 else 1
    return pl.pallas_call(
        _cmp2_kernel,
        out_shape=jax.ShapeDtypeStruct((bsz, 2, ng, LANES), F32),
        grid=(bsz // spb, 2),
        in_specs=[pl.BlockSpec((spb, 1, ng, 512), lambda b, kv: (b, kv, 0, 0)),
                  pl.BlockSpec((1, 1, CMP_HIDDEN), lambda b, kv: (kv, 0, 0)),
                  pl.BlockSpec((1, 2 * CMP_HIDDEN, LANES), lambda b, kv: (kv, 0, 0)),
                  pl.BlockSpec((1, 1, LANES), lambda b, kv: (kv, 0, 0)),
                  pl.BlockSpec((1, LANES), lambda b, kv: (0, 0))],
        out_specs=pl.BlockSpec((spb, 1, ng, LANES), lambda b, kv: (b, kv, 0, 0)),
        compiler_params=_cparams(("arbitrary", "arbitrary")),
        name="cmp2",
    )(ab, c1, lw['cmp_w2bd'], lw['cmp_b2t'], lw['nw_kc'])


def _masked_exp(s, mask):
    s = jnp.where(mask, s, NEG_INF)
    mx = jnp.maximum(jnp.max(s, axis=-1, keepdims=True), 0.1 * NEG_INF)
    e = jnp.exp(s - mx)
    den = jnp.sum(e, axis=-1, keepdims=True)
    return e, 1.0 / jnp.maximum(den, 1e-30)


def _masked_softmax(s, mask):
    e, rden = _masked_exp(s, mask)
    return e * rden


def _selection_mask(imp, qpos, axis):
    blk = _iota(imp.shape, axis)
    cur = qpos // SEL_BLOCK
    valid = blk * SEL_BLOCK <= qpos
    forced = (blk == 0) | (blk == cur) | (blk == cur - 1)
    score = jnp.where(valid, jnp.where(forced, FORCE, imp), NEG_INF)
    return _topk_mask(score, SEL_TOP_N, axis)


def _attn_prompt_kernel(q_ref, qr_ref, gt_ref, kc_ref, vc_ref, ks_ref, vs0_ref, vs1_ref, *rest, ck):
    nwb = WINDOW // Q_TILE + 1
    kw_refs = rest[0:nwb]
    vw_refs = rest[nwb:2 * nwb]
    ovt_ref, en_ref, o_ref, qg_s, qrg_s, m_s, acc_s, sc_s = rest[2 * nwb:]
    qi = pl.program_id(1)
    tq = Q_TILE
    hpg = NSA_HEADS // 2
    scale = HEAD_DIM ** -0.5
    ng = kc_ref.shape[2]
    lo_lanes = _iota((tq, LANES), 1) < HEAD_DIM
    qpos = qi * tq + _iota((tq, 1), 0)
    qpos_t = qi * tq + _iota((1, tq), 1)
    qpos_all = jnp.concatenate([qpos] * NSA_HEADS, axis=0)
    gt = gt_ref[0]
    kc = kc_ref[0, 0]
    vc = vc_ref[0, 0]
    kw = jnp.concatenate([r[0] for r in kw_refs], axis=0)
    vw = jnp.concatenate([r[0] for r in vw_refs], axis=0)
    kwpos = jnp.concatenate([(qi - (nwb - 1) + j) * tq + _iota((1, tq), 1) for j in range(nwb)], axis=1)
    nch = (qi * tq + tq + ck - 1) // ck
    rows = hpg * tq
    for h in range(NSA_HEADS):
        g = h // hpg
        cb = h // 2
        rs = slice(tq * h, tq * (h + 1))
        keep = lo_lanes if g == 0 else jnp.logical_not(lo_lanes)
        blk = q_ref[0, :, LANES * cb:LANES * (cb + 1)]
        blkr = qr_ref[0, :, LANES * cb:LANES * (cb + 1)]
        if (h % 2 == 0) != (g == 0):
            blk = pltpu.roll(blk, HEAD_DIM, 1)
            blkr = pltpu.roll(blkr, HEAD_DIM, 1)
        qg_s[rs, :] = (jnp.where(keep, blk, 0.0) * scale).astype(BF16)
        qrg_s[rs, :] = (jnp.where(keep, blkr, 0.0) * scale).astype(BF16)
    qrg = qrg_s[...]
    kcb = _bf(kc)
    s = _dg(qg_s[...], kcb, 1, 1)
    mc = (_iota((2 * rows, ng), 1) * CMP_STRIDE + (CMP_BLOCK - 1)) <= qpos_all
    p = _masked_softmax(s, mc)
    oc = _mm1(p, vc)
    nselb = []
    for g in range(2):
        b0 = rows * g
        psum = p[b0:b0 + tq] + p[b0 + tq:b0 + 2 * tq] + p[b0 + 2 * tq:b0 + 3 * tq] + p[b0 + 3 * tq:b0 + 4 * tq]
        imp_t = _mm_xl(ovt_ref[...], psum.T)
        sel_t = _selection_mask(imp_t, qpos_t, 0)
        nselb.append((1.0 - sel_t).T.astype(BF16))
    m_s[...] = jnp.full(m_s.shape, 0.1 * NEG_INF, F32)
    acc_s[...] = jnp.zeros(acc_s.shape, F32)

    def scores(c, slot):
        st = pl.multiple_of(c * ck, ck)
        sc_s[slot] = _dg(qrg, ks_ref[0, pl.ds(st, ck), :], 1, 1)

    def span(c, slot, off, width, diagonal):
        st = pl.multiple_of(c * ck + off, width)
        causal = jnp.where(st + _iota((tq, width), 1) <= qpos, 0.0, NEG_INF) if diagonal else None
        for g in range(2):
            vs_ref = vs0_ref if g == 0 else vs1_ref
            vch = vs_ref[0, pl.ds(st, width), :]
            bias = _dg(nselb[g], en_ref[c, :, off:off + width])
            if diagonal:
                bias = bias + causal
            for hh in range(hpg):
                rs = slice(rows * g + tq * hh, rows * g + tq * (hh + 1))
                sh = sc_s[slot, rs, off:off + width] + bias
                m_old = m_s[rs, :]
                m_new = jnp.maximum(m_old, jnp.max(sh, axis=-1, keepdims=True))
                alpha = jnp.exp(m_old - m_new)
                pe = jnp.exp(sh - m_new)
                acc_s[rs, :] = alpha * acc_s[rs, :] + _dg(pe.astype(BF16), vch)
                m_s[rs, :] = m_new

    def chunk(c, slot, diagonal):
        span(c, slot, 0, ck, diagonal)

    def body(c, carry):
        scores(c + 1, (c + 1) % 2)
        chunk(c, c % 2, False)
        return carry

    scores(0, 0)
    lax.fori_loop(0, nch - 1, body, 0)
    chunk(nch - 1, (nch - 1) % 2, True)
    acc = acc_s[...]
    osel = acc / pltpu.roll(acc, HEAD_DIM, 1)
    sw = _dg(qrg, kw, 1, 1)
    mw = (kwpos <= qpos_all) & (kwpos > qpos_all - WINDOW) & (kwpos >= 0)
    ew, rw = _masked_exp(sw, mw)
    ow = _dg(ew.astype(BF16), vw) * rw
    for h in range(NSA_HEADS):
        g = h // hpg
        rs = slice(tq * h, tq * (h + 1))
        og = (gt[:, 3 * h:3 * h + 1] * oc[rs] + gt[:, 3 * h + 1:3 * h + 2] * osel[rs]
              + gt[:, 3 * h + 2:3 * h + 3] * ow[rs])
        o_ref[0, :, HEAD_DIM * h:HEAD_DIM * (h + 1)] = og[:, HEAD_DIM * g:HEAD_DIM * (g + 1)]


def _overlap_matrix(ng):
    s = np.arange(ng)[:, None] * CMP_STRIDE
    j = np.arange(LANES)[None, :] * SEL_BLOCK
    return ((s < j + SEL_BLOCK) & (s + CMP_BLOCK > j)).astype(np.float32)


def _block_bias_matrix(t, ck):
    key_blk = (np.arange(t) // SEL_BLOCK).reshape(t // ck, 1, ck)
    blk = np.arange(LANES).reshape(1, LANES, 1)
    return jnp.asarray(np.where(key_blk == blk, NEG_INF, 0.0).astype(np.float32), dtype=BF16)


def _attn_prompt(q, qr, gt, ckv, selk, selv, win):
    bsz, t, _ = q.shape
    tq = Q_TILE
    ng = ckv.shape[2]
    ck = min(1024, t)
    nwb = WINDOW // tq + 1
    hpg = NSA_HEADS // 2
    assert t // SEL_BLOCK <= LANES and t % ck == 0
    ovt = jnp.asarray(_overlap_matrix(ng).T, dtype=BF16)

    def wspec(j, col):
        return pl.BlockSpec((1, tq, LANES), lambda b, i: (b, jnp.maximum(i - (nwb - 1) + j, 0), col))

    tile = lambda w: pl.BlockSpec((1, tq, w), lambda b, i: (b, i, 0))
    return pl.pallas_call(
        functools.partial(_attn_prompt_kernel, ck=ck),
        out_shape=jax.ShapeDtypeStruct((bsz, t, NSA_WIDTH), F32),
        grid=(bsz, t // tq),
        in_specs=[tile(NSA_WIDTH), tile(NSA_WIDTH), tile(LANES),
                  pl.BlockSpec((1, 1, ng, LANES), lambda b, i: (b, 0, 0, 0)),
                  pl.BlockSpec((1, 1, ng, LANES), lambda b, i: (b, 1, 0, 0)),
                  pl.BlockSpec((1, t, LANES), lambda b, i: (b, 0, 0)),
                  pl.BlockSpec((1, t, LANES), lambda b, i: (b, 0, 0)),
                  pl.BlockSpec((1, t, LANES), lambda b, i: (b, 0, 1))]
                 + [wspec(j, 0) for j in range(nwb)] + [wspec(j, 1) for j in range(nwb)]
                 + [pl.BlockSpec((LANES, ng), lambda b, i: (0, 0)),
                    pl.BlockSpec((t // ck, LANES, ck), lambda b, i: (0, 0, 0))],
        out_specs=tile(NSA_WIDTH),
        scratch_shapes=[pltpu.VMEM((NSA_HEADS * tq, LANES), BF16), pltpu.VMEM((NSA_HEADS * tq, LANES), BF16),
                        pltpu.VMEM((NSA_HEADS * tq, 1), F32), pltpu.VMEM((NSA_HEADS * tq, LANES), F32),
                        pltpu.VMEM((2, NSA_HEADS * tq, ck), F32)],
        compiler_params=_cparams(("arbitrary", "arbitrary")),
        name="attn_prompt",
    )(q, qr, gt, ckv, ckv, selk, selv, selv, *([win] * (2 * nwb)), ovt, _block_bias_matrix(t, ck))


def _sample_cmp_kernel(q_ref, ckv_ref, ov_ref, oc_ref, imp_ref, *, past_len):
    nh = NSA_HEADS
    scale = HEAD_DIM ** -0.5
    ng = ckv_ref.shape[2]
    mc = (_iota((nh, ng), 1) * CMP_STRIDE + (CMP_BLOCK - 1)) <= past_len
    same_group = (_iota((nh, nh), 0) // (nh // 2) == _iota((nh, nh), 1) // (nh // 2)).astype(BF16)
    for i in range(q_ref.shape[0]):
        s = _mm1(q_ref[i], ckv_ref[i, 0], 1, 1) * scale
        p = _masked_softmax(s, mc)
        oc_ref[i] = _mm1(p, ckv_ref[i, 1])
        imp_ref[i] = _mm_xr(_mm_xl(same_group, p), ov_ref[...])


def _sample_cmp(q, ckv, past_len):
    bs, nh, _ = q.shape
    ng = ckv.shape[2]
    spb = 8 if bs % 8 == 0 else 1
    vec = pl.BlockSpec((spb, nh, LANES), lambda b: (b, 0, 0))
    return pl.pallas_call(
        functools.partial(_sample_cmp_kernel, past_len=past_len),
        out_shape=[jax.ShapeDtypeStruct((bs, nh, LANES), F32)] * 2,
        grid=(bs // spb,),
        in_specs=[vec, pl.BlockSpec((spb, 2, ng, LANES), lambda b: (b, 0, 0, 0)),
                  pl.BlockSpec((ng, LANES), lambda b: (0, 0))],
        out_specs=[vec, vec],
        compiler_params=_cparams(("arbitrary",)),
        name="sample_cmp",
    )(q, ckv, jnp.asarray(_overlap_matrix(ng), dtype=BF16))


def _sample_topk_kernel(imp_ref, sel_ref, *, past_len):
    sel_ref[...] = _selection_mask(imp_ref[...], past_len, 1).astype(BF16)


def _sample_topk(imp, past_len):
    rows = imp.shape[0]
    return pl.pallas_call(
        functools.partial(_sample_topk_kernel, past_len=past_len),
        out_shape=jax.ShapeDtypeStruct((rows, LANES), BF16),
        grid=(1,),
        in_specs=[pl.BlockSpec((rows, LANES), lambda i: (0, 0))],
        out_specs=pl.BlockSpec((rows, LANES), lambda i: (0, 0)),
        compiler_params=_cparams(("arbitrary",)),
        name="sample_topk",
    )(imp)


def _attn_sample_kernel(pt_ref, qr_ref, gt_ref, oc_ref, selb_ref, *rest, past_len, n_pages):
    pages = rest[:n_pages]
    nsel_ref, win_ref, nwin_ref, o_ref, wout_ref = rest[n_pages:]
    scale = HEAD_DIM ** -0.5
    qpos = past_len
    qr = qr_ref[0]
    gt = gt_ref[0]
    oc = oc_ref[0]
    selb = selb_ref[0]
    page = pages[0].shape[1]
    nsel = nsel_ref[0]
    qrb = qr.astype(BF16)
    s_new = jnp.sum(qr * nsel[:, 0:LANES], axis=-1, keepdims=True) * scale
    k_all = jnp.concatenate([pg[0, :, 0:LANES].astype(BF16) for pg in pages], axis=0)
    v_all = jnp.concatenate([pg[0, :, LANES:2 * LANES].astype(BF16) for pg in pages], axis=0)
    nkeys = n_pages * page
    s_all = _dg(qrb, k_all, 1, 1) * scale
    expand = (_iota((LANES, nkeys), 0) == _iota((LANES, nkeys), 1) // SEL_BLOCK).astype(BF16)
    ok = _dg(selb, expand) > 0.5
    s_all = jnp.where(ok, s_all, NEG_INF)
    m = jnp.maximum(s_new, jnp.max(s_all, axis=-1, keepdims=True))
    e_new = jnp.exp(s_new - m)
    e = jnp.where(ok, jnp.exp(s_all - m), 0.0)
    den = e_new + jnp.sum(e, axis=-1, keepdims=True)
    osel = (e_new * nsel[:, LANES:2 * LANES] + _dg(e.astype(BF16), v_all)) / den
    wb = win_ref.shape[1]
    wnd = win_ref[0]
    nwin = nwin_ref[0]
    sw = _dg(qrb, wnd[:, 0:LANES].astype(BF16), 1, 1) * scale
    kwpos = past_len - wb + _iota((1, wb), 1)
    mw = (kwpos <= qpos) & (kwpos > qpos - WINDOW) & (kwpos >= 0)
    sw = jnp.where(mw, sw, NEG_INF)
    sw_new = jnp.sum(qr * nwin[:, 0:LANES], axis=-1, keepdims=True) * scale
    mx = jnp.maximum(jnp.max(sw, axis=-1, keepdims=True), sw_new)
    ew = jnp.where(mw, jnp.exp(sw - mx), 0.0)
    ew_new = jnp.exp(sw_new - mx)
    ow = (_dg(ew.astype(BF16), wnd[:, LANES:2 * LANES].astype(BF16)) + ew_new * nwin[:, LANES:2 * LANES]) / (
        jnp.sum(ew, axis=-1, keepdims=True) + ew_new)
    o_ref[0] = gt[:, 0:1] * oc + gt[:, 1:2] * osel + gt[:, 2:3] * ow
    shifted = pltpu.roll(wnd, wb - 1, 0)
    wout_ref[0] = jnp.where(_iota((wb, 1), 0) == wb - 1, nwin, shifted)


def _attn_sample(page_table, q, qr, gt, ckv, cache_sel, page_off, nsel, cache_win, win_off, nwin, past_len):
    bs, n_pages = page_table.shape
    page = cache_sel.shape[1]
    wb = cache_win.shape[1]
    nh = NSA_HEADS
    assert wb == WINDOW and past_len // SEL_BLOCK + 1 <= LANES
    oc, imp = _sample_cmp(q, ckv, past_len)
    selb = _sample_topk(imp.reshape(bs * nh, LANES), past_len).reshape(bs, nh, LANES)

    def page_spec(j):
        return pl.BlockSpec((1, page, 256), lambda b, pt: (pt[b, j] + page_off, 0, 0))

    vec = lambda w: pl.BlockSpec((1, nh, w), lambda b, pt: (b, 0, 0))
    rowspec = pl.BlockSpec((1, 1, 256), lambda b, pt: (b, 0, 0))
    return pl.pallas_call(
        functools.partial(_attn_sample_kernel, past_len=past_len, n_pages=n_pages),
        out_shape=[jax.ShapeDtypeStruct((bs, nh, LANES), F32), jax.ShapeDtypeStruct((bs, wb, 256), F32)],
        grid_spec=pltpu.PrefetchScalarGridSpec(
            num_scalar_prefetch=1,
            grid=(bs,),
            in_specs=[vec(LANES), vec(LANES), vec(LANES), vec(LANES)]
                     + [page_spec(j) for j in range(n_pages)]
                     + [rowspec,
                        pl.BlockSpec((1, wb, 256), lambda b, pt: (b + win_off, 0, 0)),
                        rowspec],
            out_specs=[vec(LANES), pl.BlockSpec((1, wb, 256), lambda b, pt: (b, 0, 0))]),
        compiler_params=_cparams(("arbitrary",)),
        name="attn_sample",
    )(page_table, qr, gt, oc, selb, *([cache_sel] * n_pages), nsel, cache_win, nwin)


def _hilo(w):
    hi = w.astype(BF16)
    return hi, (w - hi.astype(F32)).astype(BF16)


def _tile_heads(v, reps):
    return jnp.tile(v.reshape(1, -1), (1, reps))


def _layer_weights(l, norm_mix, norm_ffn, w_in, rwkv_mu, rwkv_w0, rwkv_w_up, rwkv_a0, rwkv_a_up, rwkv_g_up,
                   rwkv_k_k, rwkv_k_a, rwkv_r_k, rwkv_ln_w, rwkv_ln_b, qk_norm, cmp_pos, cmp_w1, cmp_b1, cmp_w2,
                   cmp_b2, w_out, ffn_w_up, ffn_w_down):
    lw = {}
    d = w_in.shape[1]
    lw['norm_mix'] = norm_mix[l].reshape(1, d)
    lw['norm_ffn'] = norm_ffn[l].reshape(1, d)
    lw['wr'] = w_in[l][:, :SHIFT_COLS].astype(BF16)
    w_n = jnp.pad(w_in[l][:, SHIFT_COLS:], ((0, 0), (0, NSA_COLS_PAD - NSA_COLS)))
    lw['wn'] = w_n.astype(BF16)
    lw['mu'] = rwkv_mu[l].reshape(1, -1)
    lw['w0'] = rwkv_w0[l].reshape(1, -1)
    lw['a0'] = rwkv_a0[l].reshape(1, -1)
    lw['k_k'] = rwkv_k_k[l].reshape(1, -1)
    lw['k_a'] = rwkv_k_a[l].reshape(1, -1)
    lw['r_k'] = rwkv_r_k[l].reshape(1, -1)
    lw['ln_w'] = rwkv_ln_w[l].reshape(1, -1)
    lw['ln_b'] = rwkv_ln_b[l].reshape(1, -1)
    zeros = jnp.zeros_like(rwkv_w_up[l])
    wc = jnp.concatenate([jnp.concatenate([rwkv_w_up[l], zeros], axis=1),
                          jnp.concatenate([zeros, rwkv_a_up[l]], axis=1)], axis=0)
    lw['wc_h'], lw['wc_l'] = _hilo(wc)
    lw['gu_h'], lw['gu_l'] = _hilo(rwkv_g_up[l])
    qn = qk_norm[l]
    lw['nw'] = jnp.tile(qn, (1, 2))
    lw['nw_kc'] = jnp.tile(qn[1:2], (1, 2))
    w1 = cmp_w1[l].reshape(2, 2, CMP_STRIDE, HEAD_DIM, CMP_HIDDEN)
    eye = jnp.eye(2, dtype=F32)
    w1g = jnp.einsum('khldj,gf->klgdhfj', w1, eye)
    w1g = w1g.reshape(2, CMP_STRIDE * 2 * HEAD_DIM, 2 * 2 * CMP_HIDDEN)
    lw['c1'] = w1g.astype(BF16)
    lw['cmp_pos'] = cmp_pos[l].reshape(2, 1, CMP_BLOCK * HEAD_DIM)
    lw['cmp_w1'] = cmp_w1[l]
    lw['cmp_b1'] = cmp_b1[l].reshape(2, 1, CMP_HIDDEN)
    w2 = cmp_w2[l]
    lw['cmp_w2bd'] = jnp.einsum('kjd,gf->kgjfd', w2, eye).reshape(2, 2 * CMP_HIDDEN, 2 * HEAD_DIM)
    lw['cmp_b2t'] = jnp.tile(cmp_b2[l].reshape(2, 1, HEAD_DIM), (1, 1, 2))
    lw['wo1'] = w_out[l][:RWKV_WIDTH].astype(BF16)
    lw['wo2'] = w_out[l][RWKV_WIDTH:].astype(BF16)
    lw['wu'] = ffn_w_up[l].astype(BF16)
    lw['wd'] = ffn_w_down[l].astype(BF16)
    return lw


def _rope_tables(pos):
    half = HEAD_DIM // 2
    inv_freq = ROPE_THETA ** (-jnp.arange(half, dtype=F32) / half)
    ang = pos.astype(F32)[:, None] * inv_freq[None, :]
    cos = jnp.cos(ang)
    sin = jnp.sin(ang)
    return jnp.tile(cos, (1, 4)), jnp.tile(jnp.concatenate([-sin, sin], axis=1), (1, 2))


def _mods6(m):
    d = m.shape[-1] // 6
    return [m[..., i * d:(i + 1) * d] for i in range(6)]


def _prompt_layer(x, mods, lw, tables):
    bsz, t, d = x.shape
    m = bsz * t
    tm = min(512, t)
    tpb = t // tm
    sh1, sc1, g1, sh2, sc2, g2 = [a.reshape(bsz, 1, d) for a in _mods6(mods)]
    x2 = x.reshape(m, d)
    p_r, p_n = _in_proj(x2, lw['norm_mix'], sc1, sh1, lw['wr'], lw['wn'], tm, tpb)
    shift0 = jnp.zeros((bsz, 1, SHIFT_COLS), F32)
    r, ld, k, v, a, b, g = _rwkv_prep(p_r, shift0, lw, tm, tpb, True)
    seq = lambda z: z.reshape(bsz, t, RWKV_WIDTH)
    q1, y0, gm, ha = _rwkv_chunk(seq(r), seq(ld), seq(k), seq(v), seq(a), seq(b))
    y, h_fin = _rwkv_scan(q1, y0, gm, ha)
    y_r = _rwkv_post(y.reshape(m, RWKV_WIDTH), r, k, v, g, lw, tm)
    s_new = h_fin.reshape(bsz, HEAD_DIM, RWKV_HEADS, HEAD_DIM).transpose(0, 2, 3, 1)
    shift_new = p_r.reshape(bsz, t, SHIFT_COLS)[:, -1]
    q, qr, cmp_rows, sel_rows, win_rows, gt, sel_k, sel_v, win_b = _nsa_proj(
        p_n, tables[0], tables[1], lw['nw'], tm, tpb)
    seqw = lambda z: z.reshape(bsz, t, z.shape[-1])
    ab = _cmp1_prompt(seqw(cmp_rows), lw['c1'])
    ckv = _cmp2(ab, lw['cmp_c1'], lw)
    y_n = _attn_prompt(seqw(q), seqw(qr), seqw(gt), ckv, seqw(sel_k), seqw(sel_v), seqw(win_b))
    x2 = _out_proj(x2, y_r, y_n.reshape(m, NSA_WIDTH), g1, lw['wo1'], lw['wo2'], tm, tpb)
    tmf = min(1024, t)
    x2 = _ffn(x2, lw['norm_ffn'], sc2, sh2, g2, lw['wu'], lw['wd'], tmf, t // tmf)
    rows6 = lambda z: z.reshape(bsz, t, 2, 2, HEAD_DIM)
    wlen = min(WINDOW, t)
    return (x2.reshape(bsz, t, d), s_new, shift_new, rows6(cmp_rows), rows6(sel_rows), rows6(win_rows)[:, -wlen:])


def _sample_layer(x, mods, lw, tables, l, state_rwkv, state_shift, page_table, cache_cmp, cache_sel, cache_win,
                  n_pool, past_len):
    bs, d = x.shape
    tm = bs
    sh1, sc1, g1, sh2, sc2, g2 = [a.reshape(1, bs, d) for a in _mods6(mods)]
    p_r, p_n = _in_proj(x, lw['norm_mix'], sc1, sh1, lw['wr'], lw['wn'], tm, 1)
    r, ld, k, v, a, b, g = _rwkv_prep(p_r, state_shift[l], lw, tm, 1, False)
    s_new, y = _rwkv_step(state_rwkv, l, r, ld, k, a, b, v)
    y_r = _rwkv_post(y.reshape(bs, RWKV_WIDTH), r, k, v, g, lw, tm)
    q, qr, cmp_rows, sel_rows, win_rows, gt, _, _, _ = _nsa_proj(p_n, tables[0], tables[1], lw['nw'], tm, 1)
    ab = _cmp1_sample(page_table, cache_cmp, l * n_pool, lw['c1'])
    ckv = _cmp2(ab, lw['cmp_c1'], lw)
    grp = (jnp.arange(NSA_HEADS) // (NSA_HEADS // 2))[None, :, None]
    half = (jnp.arange(LANES) // HEAD_DIM)[None, None, :]

    def pad_heads(z):
        z = z.reshape(bs, NSA_HEADS, HEAD_DIM)
        return jnp.where(grp == half, jnp.tile(z, (1, 1, 2)), 0.0)

    gates = jnp.pad(gt[:, :3 * NSA_HEADS].reshape(bs, NSA_HEADS, 3), ((0, 0), (0, 0), (0, LANES - 3)))
    o, win_new = _attn_sample(page_table, pad_heads(q), pad_heads(qr), gates, ckv, cache_sel, l * n_pool,
                              sel_rows.reshape(bs, 1, 256), cache_win, l * bs, win_rows.reshape(bs, 1, 256), past_len)
    o = o.reshape(bs, NSA_HEADS, 2, HEAD_DIM)
    y_n = jnp.concatenate([o[:, :NSA_HEADS // 2, 0], o[:, NSA_HEADS // 2:, 1]], axis=1).reshape(bs, NSA_WIDTH)
    x = _out_proj(x, y_r, y_n, g1, lw['wo1'], lw['wo2'], tm, 1)
    x = _ffn(x, lw['norm_ffn'], sc2, sh2, g2, lw['wu'], lw['wd'], tm, 1)
    rows6 = lambda z: z.reshape(bs, 1, 2, 2, HEAD_DIM)
    return (x, s_new, p_r, rows6(cmp_rows), rows6(sel_rows), win_new.reshape(bs, -1, 2, 2, HEAD_DIM))


def kernel(x_prompt, x_sample, cache_cmp_kv, cache_sel_kv, cache_win_kv, state_rwkv, state_shift, page_table,
           c_prompt, c_sample, ada_w, ada_b, norm_mix, norm_ffn, w_in, rwkv_mu, rwkv_w0, rwkv_w_up, rwkv_a0,
           rwkv_a_up, rwkv_g_up, rwkv_k_k, rwkv_k_a, rwkv_r_k, rwkv_ln_w, rwkv_ln_b, qk_norm, cmp_pos, cmp_w1,
           cmp_b1, cmp_w2, cmp_b2, w_out, ffn_w_up, ffn_w_down):
    depth = ada_w.shape[0]
    bsz, t, d = x_prompt.shape
    bs = x_sample.shape[0]
    n_pool, page = cache_cmp_kv.shape[1], cache_cmp_kv.shape[2]
    n_pages = page_table.shape[1]
    past_len = n_pages * page
    wb = cache_win_kv.shape[2]
    bc = -(-(bsz + bs) // 8) * 8
    c_all = jnp.pad(jnp.concatenate([c_prompt, c_sample], axis=0), ((0, bc - bsz - bs), (0, 0)))
    mods = _ada_mods(c_all, ada_w, ada_b)
    tab_p = _rope_tables(jnp.arange(t, dtype=jnp.int32))
    tab_s = _rope_tables(jnp.full((bs,), past_len, jnp.int32))
    cache_cmp = cache_cmp_kv.reshape(depth * n_pool, page, 256)
    cache_sel = cache_sel_kv.reshape(depth * n_pool, page, 256)
    cache_win = cache_win_kv.reshape(depth * bs, wb, 256)
    st_rwkv = state_rwkv.reshape((depth * bs,) + state_rwkv.shape[2:])
    y_p = x_prompt
    y_s = x_sample.reshape(bs, d)
    outs_p = []
    outs_s = []
    for l in range(depth):
        lw = _layer_weights(l, norm_mix, norm_ffn, w_in, rwkv_mu, rwkv_w0, rwkv_w_up, rwkv_a0, rwkv_a_up,
                            rwkv_g_up, rwkv_k_k, rwkv_k_a, rwkv_r_k, rwkv_ln_w, rwkv_ln_b, qk_norm, cmp_pos,
                            cmp_w1, cmp_b1, cmp_w2, cmp_b2, w_out, ffn_w_up, ffn_w_down)
        lw['cmp_c1'] = _cmp_bias(lw)
        res_p = _prompt_layer(y_p, mods[l, :bsz], lw, tab_p)
        y_p = res_p[0]
        outs_p.append(res_p[1:])
        res_s = _sample_layer(y_s, mods[l, bsz:bsz + bs], lw, tab_s, l, st_rwkv, state_shift, page_table,
                              cache_cmp, cache_sel, cache_win, n_pool, past_len)
        y_s = res_s[0]
        outs_s.append(res_s[1:])
    stack = lambda outs, i: jnp.stack([o[i] for o in outs])
    return (y_p, y_s.reshape(bs, 1, d),
            stack(outs_p, 2), stack(outs_s, 2),
            stack(outs_p, 3), stack(outs_s, 3),
            stack(outs_p, 4), stack(outs_s, 4),
            stack(outs_p, 0), stack(outs_s, 0),
            stack(outs_p, 1), stack(outs_s, 1))
```

```python
import functools

import numpy as np
import jax
import jax.numpy as jnp
from jax import lax
from jax.experimental import pallas as pl
from jax.experimental.pallas import tpu as pltpu

F32 = jnp.float32
BF16 = jnp.bfloat16

HEAD_DIM = 64
RWKV_WIDTH = 512
NSA_WIDTH = 512
RWKV_HEADS = 8
NSA_HEADS = 8
KV_WIDTH = 128
SHIFT_COLS = 1792
NSA_COLS = 1304
NSA_COLS_PAD = 1408
CMP_BLOCK = 32
CMP_STRIDE = 16
CMP_HIDDEN = 128
SEL_BLOCK = 64
SEL_TOP_N = 16
WINDOW = 512
ROPE_THETA = 10000.0
NORM_EPS = 1e-6
GN_EPS = 64e-5
NEG_INF = -1e30
FORCE = 1e9
LANES = 128
RWKV_CHUNK = 64
Q_TILE = 128
VMEM_LIMIT = 56 * 1024 * 1024


def _cparams(sem):
    return pltpu.CompilerParams(dimension_semantics=sem, vmem_limit_bytes=VMEM_LIMIT)


def _split2(x):
    hi = x.astype(BF16)
    lo = (x - hi.astype(F32)).astype(BF16)
    return hi, lo


def _split3(x):
    hi = x.astype(BF16)
    r = x - hi.astype(F32)
    mid = r.astype(BF16)
    lo = (r - mid.astype(F32)).astype(BF16)
    return hi, mid, lo


def _dg(a, b, ca=1, cb=0):
    return lax.dot_general(a, b, (((ca,), (cb,)), ((), ())), preferred_element_type=F32)


def _mm(a, b, ca=1, cb=0):
    ah, al = _split2(a)
    bh, bl = _split2(b)
    return _dg(ah, bh, ca, cb) + _dg(ah, bl, ca, cb) + _dg(al, bh, ca, cb)


def _bf(x):
    return x.astype(BF16)


def _mm1(a, b, ca=1, cb=0):
    return _dg(_bf(a), _bf(b), ca, cb)


def _mm_w(ah, al, wh, wl):
    return _dg(ah, wh) + _dg(ah, wl) + _dg(al, wh)


def _mmp(a2, b2, ca=1, cb=0):
    return _dg(a2[0], b2[0], ca, cb) + _dg(a2[0], b2[1], ca, cb) + _dg(a2[1], b2[0], ca, cb)


def _mm_xr(a, e, ca=1, cb=0):
    a0, a1, a2 = _split3(a)
    return _dg(a0, e, ca, cb) + _dg(a1, e, ca, cb) + _dg(a2, e, ca, cb)


def _mm_xl(e, b, ca=1, cb=0):
    b0, b1, b2 = _split3(b)
    return _dg(e, b0, ca, cb) + _dg(e, b1, ca, cb) + _dg(e, b2, ca, cb)


def _iota(shape, dim):
    return lax.broadcasted_iota(jnp.int32, shape, dim)


def _head_blockdiag():
    return (_iota((LANES, LANES), 0) // HEAD_DIM == _iota((LANES, LANES), 1) // HEAD_DIM).astype(BF16)


def _segsum(x, bd):
    w = x.shape[1]
    outs = [_mm_xr(x[:, j * LANES:(j + 1) * LANES], bd) for j in range(w // LANES)]
    return outs[0] if len(outs) == 1 else jnp.concatenate(outs, axis=1)


def _sigmoid(x):
    return 1.0 / (1.0 + jnp.exp(-x))


def _norm_mod(x, g, sc, sh):
    ms = jnp.mean(x * x, axis=-1, keepdims=True)
    y = x * lax.rsqrt(ms + NORM_EPS) * g
    return y * (1.0 + sc) + sh


def _topk_mask(score, n, axis):
    size = score.shape[axis]
    pos = _iota(score.shape, axis).astype(F32)
    sel = jnp.zeros_like(score)
    sc = score
    for _ in range(n):
        m = jnp.max(sc, axis=axis, keepdims=True)
        idx = jnp.min(jnp.where(sc == m, pos, float(size)), axis=axis, keepdims=True)
        hit = pos == idx
        sel = jnp.where(hit & (m > 0.5 * NEG_INF), 1.0, sel)
        sc = jnp.where(hit, -3e38, sc)
    return sel


def _ada_kernel(c_ref, w_ref, b_ref, o_ref):
    c = c_ref[...]
    o_ref[0] = _mm(c * _sigmoid(c), w_ref[0]) + b_ref[0]


def _ada_mods(c_all, ada_w, ada_b):
    depth, d, n = ada_w.shape
    bc = c_all.shape[0]
    tn = 1024
    return pl.pallas_call(
        _ada_kernel,
        out_shape=jax.ShapeDtypeStruct((depth, bc, n), F32),
        grid=(depth, n // tn),
        in_specs=[pl.BlockSpec((bc, d), lambda l, j: (0, 0)),
                  pl.BlockSpec((1, d, tn), lambda l, j: (l, 0, j)),
                  pl.BlockSpec((1, 1, tn), lambda l, j: (l, 0, j))],
        out_specs=pl.BlockSpec((1, bc, tn), lambda l, j: (l, 0, j)),
        compiler_params=_cparams(("arbitrary", "arbitrary")),
        name="ada_mods",
    )(c_all, ada_w, ada_b.reshape(depth, 1, n))


def _in_proj_kernel(x_ref, g_ref, sc_ref, sh_ref, wr_ref, wn_ref, or_ref, on_ref):
    h = _norm_mod(x_ref[...], g_ref[...], sc_ref[0], sh_ref[0]).astype(BF16)
    or_ref[...] = _dg(h, wr_ref[...])
    on_ref[...] = _dg(h, wn_ref[...])


def _in_proj(x, g, sc, sh, wr, wn, tm, tpb):
    m, d = x.shape
    nr = wr.shape[1]
    nn = wn.shape[1]
    r = sc.shape[1]
    return pl.pallas_call(
        _in_proj_kernel,
        out_shape=[jax.ShapeDtypeStruct((m, nr), F32), jax.ShapeDtypeStruct((m, nn), F32)],
        grid=(m // tm,),
        in_specs=[pl.BlockSpec((tm, d), lambda i: (i, 0)),
                  pl.BlockSpec((1, d), lambda i: (0, 0)),
                  pl.BlockSpec((1, r, d), lambda i: (i // tpb, 0, 0)),
                  pl.BlockSpec((1, r, d), lambda i: (i // tpb, 0, 0)),
                  pl.BlockSpec((d, nr), lambda i: (0, 0)),
                  pl.BlockSpec((d, nn), lambda i: (0, 0))],
        out_specs=[pl.BlockSpec((tm, nr), lambda i: (i, 0)), pl.BlockSpec((tm, nn), lambda i: (i, 0))],
        compiler_params=_cparams(("arbitrary",)),
        name="in_proj",
    )(x, g, sc, sh, wr, wn)


def _out_proj_kernel(x_ref, a1_ref, a2_ref, gate_ref, w1, w2, o_ref):
    acc = _dg(a1_ref[...].astype(BF16), w1[...]) + _dg(a2_ref[...].astype(BF16), w2[...])
    o_ref[...] = x_ref[...] + gate_ref[0] * acc


def _out_proj(x, a1, a2, gate, w1, w2, tm, tpb):
    m, d = x.shape
    k1 = a1.shape[1]
    k2 = a2.shape[1]
    r = gate.shape[1]
    return pl.pallas_call(
        _out_proj_kernel,
        out_shape=jax.ShapeDtypeStruct((m, d), F32),
        grid=(m // tm,),
        in_specs=[pl.BlockSpec((tm, d), lambda i: (i, 0)),
                  pl.BlockSpec((tm, k1), lambda i: (i, 0)),
                  pl.BlockSpec((tm, k2), lambda i: (i, 0)),
                  pl.BlockSpec((1, r, d), lambda i: (i // tpb, 0, 0)),
                  pl.BlockSpec((k1, d), lambda i: (0, 0)),
                  pl.BlockSpec((k2, d), lambda i: (0, 0))],
        out_specs=pl.BlockSpec((tm, d), lambda i: (i, 0)),
        compiler_params=_cparams(("arbitrary",)),
        name="out_proj",
    )(x, a1, a2, gate, w1, w2)


def _ffn_kernel(x_ref, g_ref, sc_ref, sh_ref, gate_ref, wg, wu, wd, o_ref, h_s, acc_s):
    j = pl.program_id(1)

    @pl.when(j == 0)
    def _():
        h_s[...] = _norm_mod(x_ref[...], g_ref[...], sc_ref[0], sh_ref[0]).astype(BF16)
        acc_s[...] = jnp.zeros_like(acc_s)

    h = h_s[...]
    gate = _dg(h, wg[...])
    up = _dg(h, wu[...])
    act = gate * _sigmoid(gate) * up
    acc_s[...] += _dg(act.astype(BF16), wd[...])

    @pl.when(j == pl.num_programs(1) - 1)
    def _():
        o_ref[...] = x_ref[...] + gate_ref[0] * acc_s[...]


def _ffn(x, g, sc, sh, gate, w_up, w_down, tm, tpb):
    m, d = x.shape
    dff = w_down.shape[0]
    tf = 256
    nf = dff // tf
    r = sc.shape[1]
    mod = pl.BlockSpec((1, r, d), lambda i, j: (i // tpb, 0, 0))
    return pl.pallas_call(
        _ffn_kernel,
        out_shape=jax.ShapeDtypeStruct((m, d), F32),
        grid=(m // tm, nf),
        in_specs=[pl.BlockSpec((tm, d), lambda i, j: (i, 0)),
                  pl.BlockSpec((1, d), lambda i, j: (0, 0)),
                  mod, mod, mod,
                  pl.BlockSpec((d, tf), lambda i, j: (0, j)),
                  pl.BlockSpec((d, tf), lambda i, j: (0, j + nf)),
                  pl.BlockSpec((tf, d), lambda i, j: (j, 0))],
        out_specs=pl.BlockSpec((tm, d), lambda i, j: (i, 0)),
        scratch_shapes=[pltpu.VMEM((tm, d), BF16), pltpu.VMEM((tm, d), F32)],
        compiler_params=_cparams(("arbitrary", "arbitrary")),
        name="ffn",
    )(x, g, sc, sh, gate, w_up, w_up, w_down)


def _rwkv_prep_kernel(p_ref, prev_ref, mu_ref, wch, wcl, guh, gul, w0_ref, a0_ref, kk_ref, ka_ref,
                      r_o, ld_o, k_o, v_o, a_o, b_o, g_o, *scratch, seq_mode, tpb):
    p = p_ref[...]
    tm = p.shape[0]
    if seq_mode:
        carry = scratch[0]
        i = pl.program_id(0)

        @pl.when(i % tpb == 0)
        def _():
            carry[...] = prev_ref[0]

        rolled = pltpu.roll(p, 1, 0)
        prev = jnp.where(_iota((tm, 1), 0) == 0, carry[...], rolled)
        carry[...] = p[tm - 1:tm, :]
    else:
        prev = prev_ref[...]
    xs = p + mu_ref[...] * (prev - p)
    w_ = RWKV_WIDTH
    r = xs[:, 0:w_]
    k = xs[:, w_:2 * w_]
    v = xs[:, 2 * w_:3 * w_]
    z = xs[:, 3 * w_:3 * w_ + LANES]
    gd = xs[:, 3 * w_ + LANES:3 * w_ + 2 * LANES]
    z = jnp.where(_iota((tm, LANES), 1) < 64, jnp.tanh(z), z)
    zh, zl = _split2(z)
    wa = _mm_w(zh, zl, wch[...], wcl[...])
    zw = -(w0_ref[...] + wa[:, 0:w_])
    softplus = jnp.maximum(zw, 0.0) + jnp.log1p(jnp.exp(-jnp.abs(zw)))
    wlog = -softplus - 0.5
    a_sig = _sigmoid(a0_ref[...] + wa[:, w_:2 * w_])
    sh_, sl_ = _split2(_sigmoid(gd))
    g = _mm_w(sh_, sl_, guh[...], gul[...])
    bd = _head_blockdiag()
    kkv = k * kk_ref[...]
    nrm = jnp.sqrt(_segsum(kkv * kkv, bd))
    kkn = kkv / jnp.maximum(nrm, 1e-12)
    r_o[...] = r
    ld_o[...] = -jnp.exp(wlog)
    k_o[...] = k * (1.0 + (a_sig - 1.0) * ka_ref[...])
    v_o[...] = v
    a_o[...] = -kkn
    b_o[...] = kkn * a_sig
    g_o[...] = g


def _rwkv_prep(p_r, prev, lw, tm, tpb, seq_mode):
    m = p_r.shape[0]
    w_ = RWKV_WIDTH
    row = lambda n: pl.BlockSpec((1, n), lambda i: (0, 0))
    full = lambda a: pl.BlockSpec(a.shape, lambda i: (0, 0))
    if seq_mode:
        prev_spec = pl.BlockSpec((1, 1, SHIFT_COLS), lambda i: (i // tpb, 0, 0))
        scratch = [pltpu.VMEM((1, SHIFT_COLS), F32)]
    else:
        prev_spec = pl.BlockSpec((tm, SHIFT_COLS), lambda i: (i, 0))
        scratch = []
    out = jax.ShapeDtypeStruct((m, w_), F32)
    ospec = pl.BlockSpec((tm, w_), lambda i: (i, 0))
    return pl.pallas_call(
        functools.partial(_rwkv_prep_kernel, seq_mode=seq_mode, tpb=tpb),
        out_shape=[out] * 7,
        grid=(m // tm,),
        in_specs=[pl.BlockSpec((tm, SHIFT_COLS), lambda i: (i, 0)), prev_spec, row(SHIFT_COLS),
                  full(lw['wc_h']), full(lw['wc_l']), full(lw['gu_h']), full(lw['gu_l']),
                  row(w_), row(w_), row(w_), row(w_)],
        out_specs=[ospec] * 7,
        scratch_shapes=scratch,
        compiler_params=_cparams(("arbitrary",)),
        name="rwkv_prep",
    )(p_r, prev, lw['mu'], lw['wc_h'], lw['wc_l'], lw['gu_h'], lw['gu_l'], lw['w0'], lw['a0'], lw['k_k'], lw['k_a'])


def _rwkv_chunk_kernel(r_ref, ld_ref, k_ref, v_ref, a_ref, b_ref, q1_o, y0_o, g_o, ha_o):
    c = RWKV_CHUNK
    c2 = 2 * c
    nb = r_ref.shape[0]
    tri = (_iota((c, c), 0) >= _iota((c, c), 1)).astype(BF16)
    at, rt, bt, kt, bh, kh, v, p_end = ([] for _ in range(8))
    for bi in range(nb):
        ld = ld_ref[bi]
        lc = _mm_xl(tri, ld)
        lend = lc[c - 1:c, :]
        inv_p = jnp.exp(-lc)
        p_to_end = jnp.exp(lend - lc)
        p_end.append(jnp.exp(lend))
        v.append(v_ref[bi])
        rt.append(r_ref[bi] * jnp.exp(lc))
        at.append(a_ref[bi] * jnp.exp(lc - ld))
        bt.append(b_ref[bi] * inv_p)
        kt.append(k_ref[bi] * inv_p)
        bh.append(b_ref[bi] * p_to_end)
        kh.append(k_ref[bi] * p_to_end)
    row = _iota((c2, c2), 0)
    col = _iota((c2, c2), 1)
    strict = row % c > col % c
    incl = row % c >= col % c
    eye = row == col
    lo = _iota((c, LANES), 1) < HEAD_DIM

    def pair_rows(x):
        return jnp.concatenate([jnp.where(lo, x, 0.0), jnp.where(lo, 0.0, x)], axis=0)

    chains = [(bi, slice(LANES * j, LANES * (j + 1))) for bi in range(nb) for j in range(RWKV_WIDTH // LANES)]
    pairs = range(len(chains))
    at2 = [pair_rows(at[bi][:, sl]) for bi, sl in chains]
    rt2 = [pair_rows(rt[bi][:, sl]) for bi, sl in chains]
    v2s = [_bf(pair_rows(v[bi][:, sl])) for bi, sl in chains]
    lhs = [_bf(jnp.concatenate([at2[j], rt2[j]], axis=0)) for j in pairs]
    xb = [_dg(lhs[j], _bf(pair_rows(bt[bi][:, sl])), 1, 1) for j, (bi, sl) in enumerate(chains)]
    xk = [_dg(lhs[j], _bf(pair_rows(kt[bi][:, sl])), 1, 1) for j, (bi, sl) in enumerate(chains)]
    a_ab = [jnp.where(strict, xb[j][0:c2], 0.0) for j in pairs]
    a_rb = [_bf(jnp.where(incl, xb[j][c2:2 * c2], 0.0)) for j in pairs]
    a_ak = [_bf(jnp.where(strict, xk[j][0:c2], 0.0)) for j in pairs]
    a_rk = [_bf(jnp.where(incl, xk[j][c2:2 * c2], 0.0)) for j in pairs]
    akv = [_dg(a_ak[j], v2s[j]) for j in pairs]
    x = [jnp.where(eye, 1.0, 0.0) + a_ab[j] for j in pairs]
    mp = a_ab
    for _ in range(int(np.log2(c)) - 1):
        mps = [_bf(mp[j]) for j in pairs]
        mp = [_dg(mps[j], mps[j]) for j in pairs]
        x = [x[j] + _dg(_bf(mp[j]), _bf(x[j])) for j in pairs]
    wu = [_bf(_dg(_bf(x[j]), _bf(jnp.concatenate([at2[j], akv[j]], axis=1)))) for j in pairs]
    qy = [_dg(a_rb[j], wu[j]) for j in pairs]
    ark_v = [_dg(a_rk[j], v2s[j]) for j in pairs]
    gh = [_dg(_bf(pair_rows(bh[bi][:, sl])), wu[j], 0, 0) for j, (bi, sl) in enumerate(chains)]
    khv = [_dg(_bf(pair_rows(kh[bi][:, sl])), v2s[j], 0, 0) for j, (bi, sl) in enumerate(chains)]
    for j, (bi, sl) in enumerate(chains):
        q1 = rt2[j] + qy[j][:, 0:LANES]
        y0 = qy[j][:, LANES:2 * LANES] + ark_v[j]
        q1_o[bi, :, sl] = q1[0:c] + q1[c:c2]
        y0_o[bi, :, sl] = y0[0:c] + y0[c:c2]
        g_o[bi, 0, :, sl] = jnp.where(eye, p_end[bi][:, sl], 0.0) + gh[j][:, 0:LANES]
        ha_o[bi, 0, :, sl] = gh[j][:, LANES:2 * LANES] + khv[j]


def _rwkv_chunk(r, ld, k, v, a, b):
    bsz, t, w_ = r.shape
    c = RWKV_CHUNK
    nch = t // c
    nb = 2 if bsz % 2 == 0 else 1
    ispec = pl.BlockSpec((nb, c, w_), lambda i, j: (i, j, 0))
    sspec = pl.BlockSpec((nb, 1, 2 * HEAD_DIM, w_), lambda i, j: (i, j, 0, 0))
    return pl.pallas_call(
        _rwkv_chunk_kernel,
        out_shape=[jax.ShapeDtypeStruct((bsz, t, w_), F32)] * 2
                  + [jax.ShapeDtypeStruct((bsz, nch, 2 * HEAD_DIM, w_), F32)] * 2,
        grid=(bsz // nb, nch),
        in_specs=[ispec] * 6,
        out_specs=[ispec, ispec, sspec, sspec],
        compiler_params=_cparams(("arbitrary", "arbitrary")),
        name="rwkv_chunk",
    )(r, ld, k, v, a, b)


def _rwkv_scan_kernel(q1_ref, y0_ref, g_ref, ha_ref, y_o, hf_o, h_s):
    j = pl.program_id(0)

    @pl.when(j == 0)
    def _():
        h_s[...] = jnp.zeros_like(h_s)

    c = RWKV_CHUNK
    cps = g_ref.shape[1]
    src = j % 2
    dst = (j + 1) % 2
    chains = [(bi, slice(LANES * p, LANES * (p + 1)))
              for bi in range(q1_ref.shape[0]) for p in range(RWKV_WIDTH // LANES)]
    h = [h_s[src, bi, :, sl] for bi, sl in chains]
    for cc in range(cps):
        rows = slice(c * cc, c * (cc + 1))
        for n, (bi, sl) in enumerate(chains):
            h0 = _split2(h[n])
            y_o[bi, rows, sl] = _mmp(_split2(q1_ref[bi, rows, sl]), h0) + y0_ref[bi, rows, sl]
            h[n] = _mmp(_split2(g_ref[bi, cc, :, sl]), h0) + ha_ref[bi, cc, :, sl]
    for n, (bi, sl) in enumerate(chains):
        h_s[dst, bi, :, sl] = h[n]

    @pl.when(j == pl.num_programs(0) - 1)
    def _():
        for n, (bi, sl) in enumerate(chains):
            hf_o[bi, :, sl] = h[n][0:HEAD_DIM] + h[n][HEAD_DIM:2 * HEAD_DIM]


def _rwkv_scan(q1, y0, g, ha):
    bsz, t, w_ = q1.shape
    nch = t // RWKV_CHUNK
    cps = 8 if nch % 8 == 0 else (4 if nch % 4 == 0 else 1)
    c = cps * RWKV_CHUNK
    ispec = pl.BlockSpec((bsz, c, w_), lambda j: (0, j, 0))
    sspec = pl.BlockSpec((bsz, cps, 2 * HEAD_DIM, w_), lambda j: (0, j, 0, 0))
    return pl.pallas_call(
        _rwkv_scan_kernel,
        out_shape=[jax.ShapeDtypeStruct((bsz, t, w_), F32), jax.ShapeDtypeStruct((bsz, HEAD_DIM, w_), F32)],
        grid=(t // c,),
        in_specs=[ispec, ispec, sspec, sspec],
        out_specs=[ispec, pl.BlockSpec((bsz, HEAD_DIM, w_), lambda j: (0, 0, 0))],
        scratch_shapes=[pltpu.VMEM((2, bsz, 2 * HEAD_DIM, w_), F32)],
        compiler_params=_cparams(("arbitrary",)),
        name="rwkv_scan",
    )(q1, y0, g, ha)


def _rwkv_step_kernel(s_ref, r_ref, ld_ref, k_ref, a_ref, b_ref, v_ref, so_ref, y_ref):
    s = s_ref[...]
    sa = jnp.sum(s * a_ref[...], axis=-1, keepdims=True)
    s_new = s * jnp.exp(ld_ref[...]) + sa * b_ref[...] + v_ref[...] * k_ref[...]
    so_ref[...] = s_new
    y_ref[...] = jnp.sum(s_new * r_ref[...], axis=-1, keepdims=True)


def _rwkv_step(state, layer, r, ld, k, a, b, v):
    bs = r.shape[0]
    nh, hd = RWKV_HEADS, HEAD_DIM
    bt = min(8, bs)
    rowv = lambda x: x.reshape(bs, nh, 1, hd)
    vspec = pl.BlockSpec((bt, nh, 1, hd), lambda i: (i, 0, 0, 0))
    cspec = pl.BlockSpec((bt, nh, hd, 1), lambda i: (i, 0, 0, 0))
    off = layer * (bs // bt)
    return pl.pallas_call(
        _rwkv_step_kernel,
        out_shape=[jax.ShapeDtypeStruct((bs, nh, hd, hd), F32), jax.ShapeDtypeStruct((bs, nh, hd, 1), F32)],
        grid=(bs // bt,),
        in_specs=[pl.BlockSpec((bt, nh, hd, hd), lambda i: (i + off, 0, 0, 0)),
                  vspec, vspec, vspec, vspec, vspec, cspec],
        out_specs=[pl.BlockSpec((bt, nh, hd, hd), lambda i: (i, 0, 0, 0)), cspec],
        compiler_params=_cparams(("arbitrary",)),
        name="rwkv_step",
    )(state, rowv(r), rowv(ld), rowv(k), rowv(a), rowv(b), v.reshape(bs, nh, hd, 1))


def _rwkv_post_kernel(y_ref, r_ref, k_ref, v_ref, g_ref, rk_ref, lw_ref, lb_ref, o_ref):
    bd = _head_blockdiag()
    y = y_ref[...]
    inv = 1.0 / HEAD_DIM
    mu = _segsum(y, bd) * inv
    d = y - mu
    var = _segsum(d * d, bd) * inv
    yn = d * lax.rsqrt(var + GN_EPS) * lw_ref[...] + lb_ref[...]
    v = v_ref[...]
    bonus = _segsum(r_ref[...] * k_ref[...] * rk_ref[...], bd) * v
    o_ref[...] = (yn + bonus) * g_ref[...]


def _rwkv_post(y, r, k, v, g, lw, tm):
    m, w_ = y.shape
    tile = pl.BlockSpec((tm, w_), lambda i: (i, 0))
    row = pl.BlockSpec((1, w_), lambda i: (0, 0))
    return pl.pallas_call(
        _rwkv_post_kernel,
        out_shape=jax.ShapeDtypeStruct((m, w_), F32),
        grid=(m // tm,),
        in_specs=[tile] * 5 + [row] * 3,
        out_specs=tile,
        compiler_params=_cparams(("arbitrary",)),
        name="rwkv_post",
    )(y, r, k, v, g, lw['r_k'], lw['ln_w'], lw['ln_b'])


def _nsa_proj_kernel(p_ref, cos_ref, sin_ref, nw_ref, q_o, qr_o, cmp_o, sel_o, win_o, gt_o, selk_o, selv_o, winb_o):
    tm = p_ref.shape[0]
    cos = cos_ref[...]
    sin = sin_ref[...]
    bd = _head_blockdiag()
    first = (_iota((tm, LANES), 1) % HEAD_DIM) < HEAD_DIM // 2
    nw = nw_ref[...]

    def norm(x, w):
        ms = _segsum(x * x, bd) * (1.0 / HEAD_DIM)
        return x * lax.rsqrt(ms + NORM_EPS) * w

    def rope(x):
        rot = jnp.where(first, pltpu.roll(x, LANES - HEAD_DIM // 2, 1), pltpu.roll(x, HEAD_DIM // 2, 1))
        return x * cos + rot * sin

    for j in range(NSA_WIDTH // LANES):
        sl = slice(LANES * j, LANES * (j + 1))
        xn = norm(p_ref[:, sl], nw[0:1])
        q_o[:, sl] = xn
        qr_o[:, sl] = rope(xn)
    o = NSA_WIDTH
    cmp_o[...] = p_ref[:, o:o + 2 * KV_WIDTH]
    ks = rope(norm(p_ref[:, o + 256:o + 384], nw[2:3]))
    vs = p_ref[:, o + 384:o + 512]
    kw = rope(norm(p_ref[:, o + 512:o + 640], nw[3:4]))
    vw = p_ref[:, o + 640:o + 768]
    sel_o[:, 0:LANES] = ks
    sel_o[:, LANES:2 * LANES] = vs
    win_o[:, 0:LANES] = kw
    win_o[:, LANES:2 * LANES] = vw
    gt_o[...] = _sigmoid(p_ref[:, o + 768:o + 896])
    lo = _iota((tm, LANES), 1) < HEAD_DIM
    selk_o[...] = ks.astype(BF16)
    selv_o[:, 0:LANES] = jnp.where(lo, vs, 1.0).astype(BF16)
    selv_o[:, LANES:2 * LANES] = jnp.where(lo, 1.0, vs).astype(BF16)
    winb_o[:, 0:LANES] = kw.astype(BF16)
    winb_o[:, LANES:2 * LANES] = vw.astype(BF16)


def _nsa_proj(p_n, cos_t, sin_t, nw, tm, tpb):
    m = p_n.shape[0]
    tt = cos_t.shape[0] // tm
    tab = pl.BlockSpec((tm, LANES), lambda i: (i % tt, 0))
    shapes = [(m, NSA_WIDTH), (m, NSA_WIDTH), (m, 256), (m, 256), (m, 256), (m, LANES),
              (m, LANES), (m, 256), (m, 256)]
    dtypes = [F32] * 6 + [BF16] * 3
    return pl.pallas_call(
        _nsa_proj_kernel,
        out_shape=[jax.ShapeDtypeStruct(s, dt) for s, dt in zip(shapes, dtypes)],
        grid=(m // tm,),
        in_specs=[pl.BlockSpec((tm, NSA_COLS_PAD), lambda i: (i, 0)), tab, tab,
                  pl.BlockSpec((4, LANES), lambda i: (0, 0))],
        out_specs=[pl.BlockSpec((tm, s[1]), lambda i: (i, 0)) for s in shapes],
        compiler_params=_cparams(("arbitrary",)),
        name="nsa_proj",
    )(p_n, cos_t, sin_t, nw)


def _cmp1_kernel(*refs, n_parts, paged):
    if paged:
        refs = refs[1:]
    parts = refs[:n_parts]
    w_ref, o_ref, xs = refs[n_parts:]
    gpp = parts[0].shape[1] // CMP_STRIDE
    for j, pr in enumerate(parts):
        for l in range(CMP_STRIDE):
            xs[gpp * j:gpp * (j + 1), LANES * l:LANES * (l + 1)] = pr[0, pl.ds(l, gpp, stride=CMP_STRIDE), :]
    res = _dg(_bf(xs[...]), w_ref[0])
    ng = o_ref.shape[2]
    for s in range(o_ref.shape[0]):
        o_ref[s, 0] = res[ng * s:ng * (s + 1)]


def _cmp1_prompt(cmp_rows, w):
    bsz, t, _ = cmp_rows.shape
    rows = min(2048, t)
    ng = rows // CMP_STRIDE
    kdim = CMP_STRIDE * LANES
    return pl.pallas_call(
        functools.partial(_cmp1_kernel, n_parts=1, paged=False),
        out_shape=jax.ShapeDtypeStruct((bsz, 2, t // CMP_STRIDE, 512), F32),
        grid=(bsz, t // rows, 2),
        in_specs=[pl.BlockSpec((1, rows, LANES), lambda b, i, kv: (b, i, kv)),
                  pl.BlockSpec((1, kdim, 512), lambda b, i, kv: (kv, 0, 0))],
        out_specs=pl.BlockSpec((1, 1, ng, 512), lambda b, i, kv: (b, kv, i, 0)),
        scratch_shapes=[pltpu.VMEM((ng, kdim), F32)],
        compiler_params=_cparams(("arbitrary", "arbitrary", "arbitrary")),
        name="cmp1_prompt",
    )(cmp_rows, w)


def _cmp1_sample(page_table, cache, page_off, w):
    bs, n_pages = page_table.shape
    page = cache.shape[1]
    ng = n_pages * page // CMP_STRIDE
    kdim = CMP_STRIDE * LANES
    sps = 8 if bs % 8 == 0 else 1

    def page_spec(s, j):
        return pl.BlockSpec((1, page, LANES), lambda b, kv, pt: (pt[b * sps + s, j] + page_off, 0, kv))

    return pl.pallas_call(
        functools.partial(_cmp1_kernel, n_parts=sps * n_pages, paged=True),
        out_shape=jax.ShapeDtypeStruct((bs, 2, ng, 512), F32),
        grid_spec=pltpu.PrefetchScalarGridSpec(
            num_scalar_prefetch=1,
            grid=(bs // sps, 2),
            in_specs=[page_spec(s, j) for s in range(sps) for j in range(n_pages)] + [
                pl.BlockSpec((1, kdim, 512), lambda b, kv, pt: (kv, 0, 0))],
            out_specs=pl.BlockSpec((sps, 1, ng, 512), lambda b, kv, pt: (b, kv, 0, 0)),
            scratch_shapes=[pltpu.VMEM((sps * ng, kdim), F32)]),
        compiler_params=_cparams(("arbitrary", "arbitrary")),
        name="cmp1_sample",
    )(page_table, *([cache] * (sps * n_pages)), w)


def _cmp_bias_kernel(pos_ref, w1_ref, b1_ref, o_ref):
    posb = jnp.broadcast_to(pos_ref[0], (8, pos_ref.shape[2]))
    o_ref[0] = _mm(posb, w1_ref[0])[0:1] + b1_ref[0]


def _cmp_bias(lw):
    kdim = CMP_BLOCK * HEAD_DIM
    return pl.pallas_call(
        _cmp_bias_kernel,
        out_shape=jax.ShapeDtypeStruct((2, 1, CMP_HIDDEN), F32),
        grid=(2,),
        in_specs=[pl.BlockSpec((1, 1, kdim), lambda kv: (kv, 0, 0)),
                  pl.BlockSpec((1, kdim, CMP_HIDDEN), lambda kv: (kv, 0, 0)),
                  pl.BlockSpec((1, 1, CMP_HIDDEN), lambda kv: (kv, 0, 0))],
        out_specs=pl.BlockSpec((1, 1, CMP_HIDDEN), lambda kv: (kv, 0, 0)),
        compiler_params=_cparams(("arbitrary",)),
        name="cmp_bias",
    )(lw['cmp_pos'], lw['cmp_w1'], lw['cmp_b1'])


def _cmp2_kernel(ab_ref, c1_ref, w2_ref, b2_ref, nw_ref, o_ref):
    kv = pl.program_id(1)
    spb, _, ng, width = ab_ref.shape
    ab = ab_ref[:, 0].reshape(spb * ng, width)
    a_part = ab[:, 0:256]
    b_next = pltpu.roll(ab[:, 256:512], spb * ng - 1, 0)
    c1 = c1_ref[0]
    h = a_part + b_next + jnp.concatenate([c1, c1], axis=1)
    h = 0.5 * h * (1.0 + jnp.tanh(np.sqrt(2.0 / np.pi) * (h + 0.044715 * (h * h * h))))
    out = _mm1(h, w2_ref[0]) + b2_ref[0]
    ms = _segsum(out * out, _head_blockdiag()) * (1.0 / HEAD_DIM)
    normed = out * lax.rsqrt(ms + NORM_EPS) * nw_ref[...]
    o_ref[:, 0] = jnp.where(kv == 0, normed, out).reshape(spb, ng, LANES)


def _cmp2(ab, c1, lw):
    bsz, _, ng, _ = ab.shape
    spb = 8 if bsz % 8 == 0 else 1
    return pl.pallas_call(
        _cmp2_kernel,
        out_shape=jax.ShapeDtypeStruct((bsz, 2, ng, LANES), F32),
        grid=(bsz // spb, 2),
        in_specs=[pl.BlockSpec((spb, 1, ng, 512), lambda b, kv: (b, kv, 0, 0)),
                  pl.BlockSpec((1, 1, CMP_HIDDEN), lambda b, kv: (kv, 0, 0)),
                  pl.BlockSpec((1, 2 * CMP_HIDDEN, LANES), lambda b, kv: (kv, 0, 0)),
                  pl.BlockSpec((1, 1, LANES), lambda b, kv: (kv, 0, 0)),
                  pl.BlockSpec((1, LANES), lambda b, kv: (0, 0))],
        out_specs=pl.BlockSpec((spb, 1, ng, LANES), lambda b, kv: (b, kv, 0, 0)),
        compiler_params=_cparams(("arbitrary", "arbitrary")),
        name="cmp2",
    )(ab, c1, lw['cmp_w2bd'], lw['cmp_b2t'], lw['nw_kc'])


def _masked_exp(s, mask):
    s = jnp.where(mask, s, NEG_INF)
    mx = jnp.maximum(jnp.max(s, axis=-1, keepdims=True), 0.1 * NEG_INF)
    e = jnp.exp(s - mx)
    den = jnp.sum(e, axis=-1, keepdims=True)
    return e, 1.0 / jnp.maximum(den, 1e-30)


def _masked_softmax(s, mask):
    e, rden = _masked_exp(s, mask)
    return e * rden


def _selection_mask(imp, qpos, axis):
    blk = _iota(imp.shape, axis)
    cur = qpos // SEL_BLOCK
    valid = blk * SEL_BLOCK <= qpos
    forced = (blk == 0) | (blk == cur) | (blk == cur - 1)
    score = jnp.where(valid, jnp.where(forced, FORCE, imp), NEG_INF)
    return _topk_mask(score, SEL_TOP_N, axis)


def _attn_prompt_kernel(q_ref, qr_ref, gt_ref, kc_ref, vc_ref, ks_ref, vs0_ref, vs1_ref, *rest, ck):
    nwb = WINDOW // Q_TILE + 1
    kw_refs = rest[0:nwb]
    vw_refs = rest[nwb:2 * nwb]
    ovt_ref, en_ref, o_ref, qg_s, qrg_s, m_s, acc_s, sc_s = rest[2 * nwb:]
    qi = pl.program_id(1)
    tq = Q_TILE
    hpg = NSA_HEADS // 2
    scale = HEAD_DIM ** -0.5
    ng = kc_ref.shape[2]
    lo_lanes = _iota((tq, LANES), 1) < HEAD_DIM
    qpos = qi * tq + _iota((tq, 1), 0)
    qpos_t = qi * tq + _iota((1, tq), 1)
    qpos_all = jnp.concatenate([qpos] * NSA_HEADS, axis=0)
    gt = gt_ref[0]
    kc = kc_ref[0, 0]
    vc = vc_ref[0, 0]
    kw = jnp.concatenate([r[0] for r in kw_refs], axis=0)
    vw = jnp.concatenate([r[0] for r in vw_refs], axis=0)
    kwpos = jnp.concatenate([(qi - (nwb - 1) + j) * tq + _iota((1, tq), 1) for j in range(nwb)], axis=1)
    nch = (qi * tq + tq + ck - 1) // ck
    rows = hpg * tq
    for h in range(NSA_HEADS):
        g = h // hpg
        cb = h // 2
        rs = slice(tq * h, tq * (h + 1))
        keep = lo_lanes if g == 0 else jnp.logical_not(lo_lanes)
        blk = q_ref[0, :, LANES * cb:LANES * (cb + 1)]
        blkr = qr_ref[0, :, LANES * cb:LANES * (cb + 1)]
        if (h % 2 == 0) != (g == 0):
            blk = pltpu.roll(blk, HEAD_DIM, 1)
            blkr = pltpu.roll(blkr, HEAD_DIM, 1)
        qg_s[rs, :] = (jnp.where(keep, blk, 0.0) * scale).astype(BF16)
        qrg_s[rs, :] = (jnp.where(keep, blkr, 0.0) * scale).astype(BF16)
    qrg = qrg_s[...]
    kcb = _bf(kc)
    s = _dg(qg_s[...], kcb, 1, 1)
    mc = (_iota((2 * rows, ng), 1) * CMP_STRIDE + (CMP_BLOCK - 1)) <= qpos_all
    p = _masked_softmax(s, mc)
    oc = _mm1(p, vc)
    nselb = []
    for g in range(2):
        b0 = rows * g
        psum = p[b0:b0 + tq] + p[b0 + tq:b0 + 2 * tq] + p[b0 + 2 * tq:b0 + 3 * tq] + p[b0 + 3 * tq:b0 + 4 * tq]
        imp_t = _mm_xl(ovt_ref[...], psum.T)
        sel_t = _selection_mask(imp_t, qpos_t, 0)
        nselb.append((1.0 - sel_t).T.astype(BF16))
    m_s[...] = jnp.full(m_s.shape, 0.1 * NEG_INF, F32)
    acc_s[...] = jnp.zeros(acc_s.shape, F32)

    def scores(c, slot):
        st = pl.multiple_of(c * ck, ck)
        sc_s[slot] = _dg(qrg, ks_ref[0, pl.ds(st, ck), :], 1, 1)

    def span(c, slot, off, width, diagonal):
        st = pl.multiple_of(c * ck + off, width)
        causal = jnp.where(st + _iota((tq, width), 1) <= qpos, 0.0, NEG_INF) if diagonal else None
        for g in range(2):
            vs_ref = vs0_ref if g == 0 else vs1_ref
            vch = vs_ref[0, pl.ds(st, width), :]
            bias = _dg(nselb[g], en_ref[c, :, off:off + width])
            if diagonal:
                bias = bias + causal
            for hh in range(hpg):
                rs = slice(rows * g + tq * hh, rows * g + tq * (hh + 1))
                sh = sc_s[slot, rs, off:off + width] + bias
                m_old = m_s[rs, :]
                m_new = jnp.maximum(m_old, jnp.max(sh, axis=-1, keepdims=True))
                alpha = jnp.exp(m_old - m_new)
                pe = jnp.exp(sh - m_new)
                acc_s[rs, :] = alpha * acc_s[rs, :] + _dg(pe.astype(BF16), vch)
                m_s[rs, :] = m_new

    def chunk(c, slot, diagonal):
        span(c, slot, 0, ck, diagonal)

    def body(c, carry):
        scores(c + 1, (c + 1) % 2)
        chunk(c, c % 2, False)
        return carry

    scores(0, 0)
    lax.fori_loop(0, nch - 1, body, 0)
    chunk(nch - 1, (nch - 1) % 2, True)
    acc = acc_s[...]
    osel = acc / pltpu.roll(acc, HEAD_DIM, 1)
    sw = _dg(qrg, kw, 1, 1)
    mw = (kwpos <= qpos_all) & (kwpos > qpos_all - WINDOW) & (kwpos >= 0)
    ew, rw = _masked_exp(sw, mw)
    ow = _dg(ew.astype(BF16), vw) * rw
    for h in range(NSA_HEADS):
        g = h // hpg
        rs = slice(tq * h, tq * (h + 1))
        og = (gt[:, 3 * h:3 * h + 1] * oc[rs] + gt[:, 3 * h + 1:3 * h + 2] * osel[rs]
              + gt[:, 3 * h + 2:3 * h + 3] * ow[rs])
        o_ref[0, :, HEAD_DIM * h:HEAD_DIM * (h + 1)] = og[:, HEAD_DIM * g:HEAD_DIM * (g + 1)]


def _overlap_matrix(ng):
    s = np.arange(ng)[:, None] * CMP_STRIDE
    j = np.arange(LANES)[None, :] * SEL_BLOCK
    return ((s < j + SEL_BLOCK) & (s + CMP_BLOCK > j)).astype(np.float32)


def _block_bias_matrix(t, ck):
    key_blk = (np.arange(t) // SEL_BLOCK).reshape(t // ck, 1, ck)
    blk = np.arange(LANES).reshape(1, LANES, 1)
    return jnp.asarray(np.where(key_blk == blk, NEG_INF, 0.0).astype(np.float32), dtype=BF16)


def _attn_prompt(q, qr, gt, ckv, selk, selv, win):
    bsz, t, _ = q.shape
    tq = Q_TILE
    ng = ckv.shape[2]
    ck = min(1024, t)
    nwb = WINDOW // tq + 1
    hpg = NSA_HEADS // 2
    assert t // SEL_BLOCK <= LANES and t % ck == 0
    ovt = jnp.asarray(_overlap_matrix(ng).T, dtype=BF16)

    def wspec(j, col):
        return pl.BlockSpec((1, tq, LANES), lambda b, i: (b, jnp.maximum(i - (nwb - 1) + j, 0), col))

    tile = lambda w: pl.BlockSpec((1, tq, w), lambda b, i: (b, i, 0))
    return pl.pallas_call(
        functools.partial(_attn_prompt_kernel, ck=ck),
        out_shape=jax.ShapeDtypeStruct((bsz, t, NSA_WIDTH), F32),
        grid=(bsz, t // tq),
        in_specs=[tile(NSA_WIDTH), tile(NSA_WIDTH), tile(LANES),
                  pl.BlockSpec((1, 1, ng, LANES), lambda b, i: (b, 0, 0, 0)),
                  pl.BlockSpec((1, 1, ng, LANES), lambda b, i: (b, 1, 0, 0)),
                  pl.BlockSpec((1, t, LANES), lambda b, i: (b, 0, 0)),
                  pl.BlockSpec((1, t, LANES), lambda b, i: (b, 0, 0)),
                  pl.BlockSpec((1, t, LANES), lambda b, i: (b, 0, 1))]
                 + [wspec(j, 0) for j in range(nwb)] + [wspec(j, 1) for j in range(nwb)]
                 + [pl.BlockSpec((LANES, ng), lambda b, i: (0, 0)),
                    pl.BlockSpec((t // ck, LANES, ck), lambda b, i: (0, 0, 0))],
        out_specs=tile(NSA_WIDTH),
        scratch_shapes=[pltpu.VMEM((NSA_HEADS * tq, LANES), BF16), pltpu.VMEM((NSA_HEADS * tq, LANES), BF16),
                        pltpu.VMEM((NSA_HEADS * tq, 1), F32), pltpu.VMEM((NSA_HEADS * tq, LANES), F32),
                        pltpu.VMEM((2, NSA_HEADS * tq, ck), F32)],
        compiler_params=_cparams(("arbitrary", "arbitrary")),
        name="attn_prompt",
    )(q, qr, gt, ckv, ckv, selk, selv, selv, *([win] * (2 * nwb)), ovt, _block_bias_matrix(t, ck))


def _sample_cmp_kernel(q_ref, ckv_ref, ov_ref, oc_ref, imp_ref, *, past_len):
    nh = NSA_HEADS
    scale = HEAD_DIM ** -0.5
    ng = ckv_ref.shape[2]
    mc = (_iota((nh, ng), 1) * CMP_STRIDE + (CMP_BLOCK - 1)) <= past_len
    same_group = (_iota((nh, nh), 0) // (nh // 2) == _iota((nh, nh), 1) // (nh // 2)).astype(BF16)
    for i in range(q_ref.shape[0]):
        s = _mm1(q_ref[i], ckv_ref[i, 0], 1, 1) * scale
        p = _masked_softmax(s, mc)
        oc_ref[i] = _mm1(p, ckv_ref[i, 1])
        imp_ref[i] = _mm_xr(_mm_xl(same_group, p), ov_ref[...])


def _sample_cmp(q, ckv, past_len):
    bs, nh, _ = q.shape
    ng = ckv.shape[2]
    spb = 8 if bs % 8 == 0 else 1
    vec = pl.BlockSpec((spb, nh, LANES), lambda b: (b, 0, 0))
    return pl.pallas_call(
        functools.partial(_sample_cmp_kernel, past_len=past_len),
        out_shape=[jax.ShapeDtypeStruct((bs, nh, LANES), F32)] * 2,
        grid=(bs // spb,),
        in_specs=[vec, pl.BlockSpec((spb, 2, ng, LANES), lambda b: (b, 0, 0, 0)),
                  pl.BlockSpec((ng, LANES), lambda b: (0, 0))],
        out_specs=[vec, vec],
        compiler_params=_cparams(("arbitrary",)),
        name="sample_cmp",
    )(q, ckv, jnp.asarray(_overlap_matrix(ng), dtype=BF16))


def _sample_topk_kernel(imp_ref, sel_ref, *, past_len):
    sel_ref[...] = _selection_mask(imp_ref[...], past_len, 1).astype(BF16)


def _sample_topk(imp, past_len):
    rows = imp.shape[0]
    return pl.pallas_call(
        functools.partial(_sample_topk_kernel, past_len=past_len),
        out_shape=jax.ShapeDtypeStruct((rows, LANES), BF16),
        grid=(1,),
        in_specs=[pl.BlockSpec((rows, LANES), lambda i: (0, 0))],
        out_specs=pl.BlockSpec((rows, LANES), lambda i: (0, 0)),
        compiler_params=_cparams(("arbitrary",)),
        name="sample_topk",
    )(imp)


def _attn_sample_kernel(pt_ref, qr_ref, gt_ref, oc_ref, selb_ref, *rest, past_len, n_pages):
    pages = rest[:n_pages]
    nsel_ref, win_ref, nwin_ref, o_ref, wout_ref = rest[n_pages:]
    scale = HEAD_DIM ** -0.5
    qpos = past_len
    qr = qr_ref[0]
    gt = gt_ref[0]
    oc = oc_ref[0]
    selb = selb_ref[0]
    page = pages[0].shape[1]
    nsel = nsel_ref[0]
    qrb = qr.astype(BF16)
    s_new = jnp.sum(qr * nsel[:, 0:LANES], axis=-1, keepdims=True) * scale
    k_all = jnp.concatenate([pg[0, :, 0:LANES].astype(BF16) for pg in pages], axis=0)
    v_all = jnp.concatenate([pg[0, :, LANES:2 * LANES].astype(BF16) for pg in pages], axis=0)
    nkeys = n_pages * page
    s_all = _dg(qrb, k_all, 1, 1) * scale
    expand = (_iota((LANES, nkeys), 0) == _iota((LANES, nkeys), 1) // SEL_BLOCK).astype(BF16)
    ok = _dg(selb, expand) > 0.5
    s_all = jnp.where(ok, s_all, NEG_INF)
    m = jnp.maximum(s_new, jnp.max(s_all, axis=-1, keepdims=True))
    e_new = jnp.exp(s_new - m)
    e = jnp.where(ok, jnp.exp(s_all - m), 0.0)
    den = e_new + jnp.sum(e, axis=-1, keepdims=True)
    osel = (e_new * nsel[:, LANES:2 * LANES] + _dg(e.astype(BF16), v_all)) / den
    wb = win_ref.shape[1]
    wnd = win_ref[0]
    nwin = nwin_ref[0]
    sw = _dg(qrb, wnd[:, 0:LANES].astype(BF16), 1, 1) * scale
    kwpos = past_len - wb + _iota((1, wb), 1)
    mw = (kwpos <= qpos) & (kwpos > qpos - WINDOW) & (kwpos >= 0)
    sw = jnp.where(mw, sw, NEG_INF)
    sw_new = jnp.sum(qr * nwin[:, 0:LANES], axis=-1, keepdims=True) * scale
    mx = jnp.maximum(jnp.max(sw, axis=-1, keepdims=True), sw_new)
    ew = jnp.where(mw, jnp.exp(sw - mx), 0.0)
    ew_new = jnp.exp(sw_new - mx)
    ow = (_dg(ew.astype(BF16), wnd[:, LANES:2 * LANES].astype(BF16)) + ew_new * nwin[:, LANES:2 * LANES]) / (
        jnp.sum(ew, axis=-1, keepdims=True) + ew_new)
    o_ref[0] = gt[:, 0:1] * oc + gt[:, 1:2] * osel + gt[:, 2:3] * ow
    shifted = pltpu.roll(wnd, wb - 1, 0)
    wout_ref[0] = jnp.where(_iota((wb, 1), 0) == wb - 1, nwin, shifted)


def _attn_sample(page_table, q, qr, gt, ckv, cache_sel, page_off, nsel, cache_win, win_off, nwin, past_len):
    bs, n_pages = page_table.shape
    page = cache_sel.shape[1]
    wb = cache_win.shape[1]
    nh = NSA_HEADS
    assert wb == WINDOW and past_len // SEL_BLOCK + 1 <= LANES
    oc, imp = _sample_cmp(q, ckv, past_len)
    selb = _sample_topk(imp.reshape(bs * nh, LANES), past_len).reshape(bs, nh, LANES)

    def page_spec(j):
        return pl.BlockSpec((1, page, 256), lambda b, pt: (pt[b, j] + page_off, 0, 0))

    vec = lambda w: pl.BlockSpec((1, nh, w), lambda b, pt: (b, 0, 0))
    rowspec = pl.BlockSpec((1, 1, 256), lambda b, pt: (b, 0, 0))
    return pl.pallas_call(
        functools.partial(_attn_sample_kernel, past_len=past_len, n_pages=n_pages),
        out_shape=[jax.ShapeDtypeStruct((bs, nh, LANES), F32), jax.ShapeDtypeStruct((bs, wb, 256), F32)],
        grid_spec=pltpu.PrefetchScalarGridSpec(
            num_scalar_prefetch=1,
            grid=(bs,),
            in_specs=[vec(LANES), vec(LANES), vec(LANES), vec(LANES)]
                     + [page_spec(j) for j in range(n_pages)]
                     + [rowspec,
                        pl.BlockSpec((1, wb, 256), lambda b, pt: (b + win_off, 0, 0)),
                        rowspec],
            out_specs=[vec(LANES), pl.BlockSpec((1, wb, 256), lambda b, pt: (b, 0, 0))]),
        compiler_params=_cparams(("arbitrary",)),
        name="attn_sample",
    )(page_table, qr, gt, oc, selb, *([cache_sel] * n_pages), nsel, cache_win, nwin)


def _hilo(w):
    hi = w.astype(BF16)
    return hi, (w - hi.astype(F32)).astype(BF16)


def _tile_heads(v, reps):
    return jnp.tile(v.reshape(1, -1), (1, reps))


def _layer_weights(l, norm_mix, norm_ffn, w_in, rwkv_mu, rwkv_w0, rwkv_w_up, rwkv_a0, rwkv_a_up, rwkv_g_up,
                   rwkv_k_k, rwkv_k_a, rwkv_r_k, rwkv_ln_w, rwkv_ln_b, qk_norm, cmp_pos, cmp_w1, cmp_b1, cmp_w2,
                   cmp_b2, w_out, ffn_w_up, ffn_w_down):
    lw = {}
    d = w_in.shape[1]
    lw['norm_mix'] = norm_mix[l].reshape(1, d)
    lw['norm_ffn'] = norm_ffn[l].reshape(1, d)
    lw['wr'] = w_in[l][:, :SHIFT_COLS].astype(BF16)
    w_n = jnp.pad(w_in[l][:, SHIFT_COLS:], ((0, 0), (0, NSA_COLS_PAD - NSA_COLS)))
    lw['wn'] = w_n.astype(BF16)
    lw['mu'] = rwkv_mu[l].reshape(1, -1)
    lw['w0'] = rwkv_w0[l].reshape(1, -1)
    lw['a0'] = rwkv_a0[l].reshape(1, -1)
    lw['k_k'] = rwkv_k_k[l].reshape(1, -1)
    lw['k_a'] = rwkv_k_a[l].reshape(1, -1)
    lw['r_k'] = rwkv_r_k[l].reshape(1, -1)
    lw['ln_w'] = rwkv_ln_w[l].reshape(1, -1)
    lw['ln_b'] = rwkv_ln_b[l].reshape(1, -1)
    zeros = jnp.zeros_like(rwkv_w_up[l])
    wc = jnp.concatenate([jnp.concatenate([rwkv_w_up[l], zeros], axis=1),
                          jnp.concatenate([zeros, rwkv_a_up[l]], axis=1)], axis=0)
    lw['wc_h'], lw['wc_l'] = _hilo(wc)
    lw['gu_h'], lw['gu_l'] = _hilo(rwkv_g_up[l])
    qn = qk_norm[l]
    lw['nw'] = jnp.tile(qn, (1, 2))
    lw['nw_kc'] = jnp.tile(qn[1:2], (1, 2))
    w1 = cmp_w1[l].reshape(2, 2, CMP_STRIDE, HEAD_DIM, CMP_HIDDEN)
    eye = jnp.eye(2, dtype=F32)
    w1g = jnp.einsum('khldj,gf->klgdhfj', w1, eye)
    w1g = w1g.reshape(2, CMP_STRIDE * 2 * HEAD_DIM, 2 * 2 * CMP_HIDDEN)
    lw['c1'] = w1g.astype(BF16)
    lw['cmp_pos'] = cmp_pos[l].reshape(2, 1, CMP_BLOCK * HEAD_DIM)
    lw['cmp_w1'] = cmp_w1[l]
    lw['cmp_b1'] = cmp_b1[l].reshape(2, 1, CMP_HIDDEN)
    w2 = cmp_w2[l]
    lw['cmp_w2bd'] = jnp.einsum('kjd,gf->kgjfd', w2, eye).reshape(2, 2 * CMP_HIDDEN, 2 * HEAD_DIM)
    lw['cmp_b2t'] = jnp.tile(cmp_b2[l].reshape(2, 1, HEAD_DIM), (1, 1, 2))
    lw['wo1'] = w_out[l][:RWKV_WIDTH].astype(BF16)
    lw['wo2'] = w_out[l][RWKV_WIDTH:].astype(BF16)
    lw['wu'] = ffn_w_up[l].astype(BF16)
    lw['wd'] = ffn_w_down[l].astype(BF16)
    return lw


def _rope_tables(pos):
    half = HEAD_DIM // 2
    inv_freq = ROPE_THETA ** (-jnp.arange(half, dtype=F32) / half)
    ang = pos.astype(F32)[:, None] * inv_freq[None, :]
    cos = jnp.cos(ang)
    sin = jnp.sin(ang)
    return jnp.tile(cos, (1, 4)), jnp.tile(jnp.concatenate([-sin, sin], axis=1), (1, 2))


def _mods6(m):
    d = m.shape[-1] // 6
    return [m[..., i * d:(i + 1) * d] for i in range(6)]


def _prompt_layer(x, mods, lw, tables):
    bsz, t, d = x.shape
    m = bsz * t
    tm = min(512, t)
    tpb = t // tm
    sh1, sc1, g1, sh2, sc2, g2 = [a.reshape(bsz, 1, d) for a in _mods6(mods)]
    x2 = x.reshape(m, d)
    p_r, p_n = _in_proj(x2, lw['norm_mix'], sc1, sh1, lw['wr'], lw['wn'], tm, tpb)
    shift0 = jnp.zeros((bsz, 1, SHIFT_COLS), F32)
    r, ld, k, v, a, b, g = _rwkv_prep(p_r, shift0, lw, tm, tpb, True)
    seq = lambda z: z.reshape(bsz, t, RWKV_WIDTH)
    q1, y0, gm, ha = _rwkv_chunk(seq(r), seq(ld), seq(k), seq(v), seq(a), seq(b))
    y, h_fin = _rwkv_scan(q1, y0, gm, ha)
    y_r = _rwkv_post(y.reshape(m, RWKV_WIDTH), r, k, v, g, lw, tm)
    s_new = h_fin.reshape(bsz, HEAD_DIM, RWKV_HEADS, HEAD_DIM).transpose(0, 2, 3, 1)
    shift_new = p_r.reshape(bsz, t, SHIFT_COLS)[:, -1]
    q, qr, cmp_rows, sel_rows, win_rows, gt, sel_k, sel_v, win_b = _nsa_proj(
        p_n, tables[0], tables[1], lw['nw'], tm, tpb)
    seqw = lambda z: z.reshape(bsz, t, z.shape[-1])
    ab = _cmp1_prompt(seqw(cmp_rows), lw['c1'])
    ckv = _cmp2(ab, lw['cmp_c1'], lw)
    y_n = _attn_prompt(seqw(q), seqw(qr), seqw(gt), ckv, seqw(sel_k), seqw(sel_v), seqw(win_b))
    x2 = _out_proj(x2, y_r, y_n.reshape(m, NSA_WIDTH), g1, lw['wo1'], lw['wo2'], tm, tpb)
    tmf = min(1024, t)
    x2 = _ffn(x2, lw['norm_ffn'], sc2, sh2, g2, lw['wu'], lw['wd'], tmf, t // tmf)
    rows6 = lambda z: z.reshape(bsz, t, 2, 2, HEAD_DIM)
    wlen = min(WINDOW, t)
    return (x2.reshape(bsz, t, d), s_new, shift_new, rows6(cmp_rows), rows6(sel_rows), rows6(win_rows)[:, -wlen:])


def _sample_layer(x, mods, lw, tables, l, state_rwkv, state_shift, page_table, cache_cmp, cache_sel, cache_win,
                  n_pool, past_len):
    bs, d = x.shape
    tm = bs
    sh1, sc1, g1, sh2, sc2, g2 = [a.reshape(1, bs, d) for a in _mods6(mods)]
    p_r, p_n = _in_proj(x, lw['norm_mix'], sc1, sh1, lw['wr'], lw['wn'], tm, 1)
    r, ld, k, v, a, b, g = _rwkv_prep(p_r, state_shift[l], lw, tm, 1, False)
    s_new, y = _rwkv_step(state_rwkv, l, r, ld, k, a, b, v)
    y_r = _rwkv_post(y.reshape(bs, RWKV_WIDTH), r, k, v, g, lw, tm)
    q, qr, cmp_rows, sel_rows, win_rows, gt, _, _, _ = _nsa_proj(p_n, tables[0], tables[1], lw['nw'], tm, 1)
    ab = _cmp1_sample(page_table, cache_cmp, l * n_pool, lw['c1'])
    ckv = _cmp2(ab, lw['cmp_c1'], lw)
    grp = (jnp.arange(NSA_HEADS) // (NSA_HEADS // 2))[None, :, None]
    half = (jnp.arange(LANES) // HEAD_DIM)[None, None, :]

    def pad_heads(z):
        z = z.reshape(bs, NSA_HEADS, HEAD_DIM)
        return jnp.where(grp == half, jnp.tile(z, (1, 1, 2)), 0.0)

    gates = jnp.pad(gt[:, :3 * NSA_HEADS].reshape(bs, NSA_HEADS, 3), ((0, 0), (0, 0), (0, LANES - 3)))
    o, win_new = _attn_sample(page_table, pad_heads(q), pad_heads(qr), gates, ckv, cache_sel, l * n_pool,
                              sel_rows.reshape(bs, 1, 256), cache_win, l * bs, win_rows.reshape(bs, 1, 256), past_len)
    o = o.reshape(bs, NSA_HEADS, 2, HEAD_DIM)
    y_n = jnp.concatenate([o[:, :NSA_HEADS // 2, 0], o[:, NSA_HEADS // 2:, 1]], axis=1).reshape(bs, NSA_WIDTH)
    x = _out_proj(x, y_r, y_n, g1, lw['wo1'], lw['wo2'], tm, 1)
    x = _ffn(x, lw['norm_ffn'], sc2, sh2, g2, lw['wu'], lw['wd'], tm, 1)
    rows6 = lambda z: z.reshape(bs, 1, 2, 2, HEAD_DIM)
    return (x, s_new, p_r, rows6(cmp_rows), rows6(sel_rows), win_new.reshape(bs, -1, 2, 2, HEAD_DIM))


def kernel(x_prompt, x_sample, cache_cmp_kv, cache_sel_kv, cache_win_kv, state_rwkv, state_shift, page_table,
           c_prompt, c_sample, ada_w, ada_b, norm_mix, norm_ffn, w_in, rwkv_mu, rwkv_w0, rwkv_w_up, rwkv_a0,
           rwkv_a_up, rwkv_g_up, rwkv_k_k, rwkv_k_a, rwkv_r_k, rwkv_ln_w, rwkv_ln_b, qk_norm, cmp_pos, cmp_w1,
           cmp_b1, cmp_w2, cmp_b2, w_out, ffn_w_up, ffn_w_down):
    depth = ada_w.shape[0]
    bsz, t, d = x_prompt.shape
    bs = x_sample.shape[0]
    n_pool, page = cache_cmp_kv.shape[1], cache_cmp_kv.shape[2]
    n_pages = page_table.shape[1]
    past_len = n_pages * page
    wb = cache_win_kv.shape[2]
    bc = -(-(bsz + bs) // 8) * 8
    c_all = jnp.pad(jnp.concatenate([c_prompt, c_sample], axis=0), ((0, bc - bsz - bs), (0, 0)))
    mods = _ada_mods(c_all, ada_w, ada_b)
    tab_p = _rope_tables(jnp.arange(t, dtype=jnp.int32))
    tab_s = _rope_tables(jnp.full((bs,), past_len, jnp.int32))
    cache_cmp = cache_cmp_kv.reshape(depth * n_pool, page, 256)
    cache_sel = cache_sel_kv.reshape(depth * n_pool, page, 256)
    cache_win = cache_win_kv.reshape(depth * bs, wb, 256)
    st_rwkv = state_rwkv.reshape((depth * bs,) + state_rwkv.shape[2:])
    y_p = x_prompt
    y_s = x_sample.reshape(bs, d)
    outs_p = []
    outs_s = []
    for l in range(depth):
        lw = _layer_weights(l, norm_mix, norm_ffn, w_in, rwkv_mu, rwkv_w0, rwkv_w_up, rwkv_a0, rwkv_a_up,
                            rwkv_g_up, rwkv_k_k, rwkv_k_a, rwkv_r_k, rwkv_ln_w, rwkv_ln_b, qk_norm, cmp_pos,
                            cmp_w1, cmp_b1, cmp_w2, cmp_b2, w_out, ffn_w_up, ffn_w_down)
        lw['cmp_c1'] = _cmp_bias(lw)
        res_p = _prompt_layer(y_p, mods[l, :bsz], lw, tab_p)
        y_p = res_p[0]
        outs_p.append(res_p[1:])
        res_s = _sample_layer(y_s, mods[l, bsz:bsz + bs], lw, tab_s, l, st_rwkv, state_shift, page_table,
                              cache_cmp, cache_sel, cache_win, n_pool, past_len)
        y_s = res_s[0]
        outs_s.append(res_s[1:])
    stack = lambda outs, i: jnp.stack([o[i] for o in outs])
    return (y_p, y_s.reshape(bs, 1, d),
            stack(outs_p, 2), stack(outs_s, 2),
            stack(outs_p, 3), stack(outs_s, 3),
            stack(outs_p, 4), stack(outs_s, 4),
            stack(outs_p, 0), stack(outs_s, 0),
            stack(outs_p, 1), stack(outs_s, 1))
```
